```python
import math
import jax, jax.numpy as jnp
from jax import lax
import numpy as np

D_MODEL = 1024
BATCH = 8
SEQ = 2048
DEPTH = 4

DA_HEADS = 4
DA_HEAD_DIM = 64
DA_QK_WIDTH = DA_HEADS * 2 * DA_HEAD_DIM
DA_WIDTH = DA_HEADS * 2 * DA_HEAD_DIM
MLA_HEADS = 8
MLA_Q_RANK = 256
MLA_KV_RANK = 128
MLA_NOPE = 64
MLA_ROPE = 32
MLA_V = 64
MLA_QK = MLA_NOPE + MLA_ROPE
MLA_WIDTH = MLA_HEADS * MLA_V
SSM_D_INNER = D_MODEL
SSM_HEAD_DIM = 64
SSM_HEADS = SSM_D_INNER // SSM_HEAD_DIM
SSM_GROUPS = 4
SSM_STATE = 128
SSM_CONV = 4
SSM_CHUNK = 128
SSM_CONV_DIM = SSM_D_INNER + 2 * SSM_GROUPS * SSM_STATE
FFN_DIM = 2816
FFN_CONV = 3
REL_BUCKETS = 32
REL_MAX_DIST = 128
ROPE_THETA = 10000.0
Q_BLOCK = 128
EPS = 1e-6

_IN_SIZES = (DA_QK_WIDTH, DA_QK_WIDTH, DA_WIDTH, MLA_Q_RANK, MLA_KV_RANK, MLA_ROPE,
             SSM_D_INNER, SSM_CONV_DIM, SSM_HEADS, 3 * D_MODEL)
IN_COLS = sum(_IN_SIZES)
_IN_OFFSETS = tuple(int(v) for v in np.cumsum(_IN_SIZES)[:-1])

kernel_name = 'hybrid_diffattn_mla_ssd_gated_trunk'

f32 = jnp.float32


def _rmsnorm(x, w):
    xf = x.astype(f32)
    y = xf * lax.rsqrt(jnp.mean(xf * xf, axis=-1, keepdims=True) + EPS)
    return (y * w.astype(f32)).astype(x.dtype)


def _causal_dwconv(u, w, b):
    k = w.shape[0]
    out = lax.conv_general_dilated(u, w[:, None, :].astype(u.dtype), window_strides=(1,),
                                   padding=[(k - 1, 0)], dimension_numbers=('NWC', 'WIO', 'NWC'),
                                   feature_group_count=u.shape[-1])
    return out + b.astype(u.dtype)


def _rope(x, cos, sin):
    half = x.shape[-1] // 2
    x1, x2 = x[..., :half], x[..., half:]
    return jnp.concatenate([x1 * cos - x2 * sin, x2 * cos + x1 * sin], axis=-1).astype(x.dtype)


def _t5_bucket(dist):
    max_exact = REL_BUCKETS // 2
    d = jnp.maximum(dist, 0)
    large = max_exact + (jnp.log(jnp.maximum(d, 1).astype(f32) / max_exact)
                         / math.log(REL_MAX_DIST / max_exact) * (REL_BUCKETS - max_exact)).astype(jnp.int32)
    large = jnp.minimum(large, REL_BUCKETS - 1)
    return jnp.where(d < max_exact, d, large)


def _to_blocks(a):
    b, h, s = a.shape[:3]
    return jnp.moveaxis(a.reshape(b, h, s // Q_BLOCK, Q_BLOCK, *a.shape[3:]), 2, 0)


def _from_blocks(a):
    a = jnp.moveaxis(a, 0, 2)
    return a.reshape(a.shape[0], a.shape[1], a.shape[2] * a.shape[3], *a.shape[4:])


def _causal_mask(blk, seq):
    q_idx = blk * Q_BLOCK + jnp.arange(Q_BLOCK)
    return jnp.arange(seq)[None, :] <= q_idx[:, None]


def _diff_attention(q, k, v, positions, rel_table, lam):
    bsz, _, seq = q.shape[:3]
    nb = seq // Q_BLOCK
    scale = DA_HEAD_DIM ** -0.5
    pos_blocks = jnp.moveaxis(positions.reshape(bsz, nb, Q_BLOCK), 1, 0)

    def block(args):
        qb, qpos, blk = args
        s = jnp.einsum('bhqmd,bhkmd->mbhqk', qb, k).astype(f32) * scale
        dist = qpos[:, :, None] - positions[:, None, :]
        bias = jnp.take(rel_table, _t5_bucket(dist), axis=0)
        s = s + jnp.transpose(bias, (0, 3, 1, 2)).astype(f32)[None]
        s = jnp.where(_causal_mask(blk, seq), s, -jnp.inf)
        p = jax.nn.softmax(s, axis=-1)
        w = p[0] - lam * p[1]
        return jnp.einsum('bhqk,bhkv->bhqv', w.astype(v.dtype), v)

    return _from_blocks(lax.map(block, (_to_blocks(q), pos_blocks, jnp.arange(nb))))


def _mla_attention(q, k, v):
    seq = q.shape[2]
    nb = seq // Q_BLOCK
    scale = MLA_QK ** -0.5

    def block(args):
        qb, blk = args
        s = jnp.einsum('bhqd,bhkd->bhqk', qb, k).astype(f32) * scale
        s = jnp.where(_causal_mask(blk, seq), s, -jnp.inf)
        p = jax.nn.softmax(s, axis=-1)
        return jnp.einsum('bhqk,bhkd->bhqd', p.astype(v.dtype), v)

    return _from_blocks(lax.map(block, (_to_blocks(q), jnp.arange(nb))))


def _ssd(xs, dt, a, bm, cm):
    bsz, seq = xs.shape[:2]
    nc, L = seq // SSM_CHUNK, SSM_CHUNK
    G, R = SSM_GROUPS, SSM_HEADS // SSM_GROUPS
    xdt = (xs.astype(f32) * dt[..., None]).reshape(bsz, nc, L, G, R, SSM_HEAD_DIM)
    da = (dt * a).reshape(bsz, nc, L, G, R)
    bm = bm.astype(f32).reshape(bsz, nc, L, G, SSM_STATE)
    cm = cm.astype(f32).reshape(bsz, nc, L, G, SSM_STATE)
    cs = jnp.cumsum(da, axis=2)
    causal = jnp.tril(jnp.ones((L, L), bool))[None, None, :, :, None, None]
    seg = cs[:, :, :, None] - cs[:, :, None, :]
    decay = jnp.exp(jnp.where(causal, seg, -jnp.inf))
    cb = jnp.einsum('bclgn,bcsgn->bclsg', cm, bm)
    y_diag = jnp.einsum('bclsgr,bcsgrp->bclgrp', cb[..., None] * decay, xdt)
    to_end = jnp.exp(cs[:, :, -1:] - cs)
    states = jnp.einsum('bclgn,bclgrp->bcgrpn', bm, xdt * to_end[..., None])
    chunk_decay = jnp.exp(cs[:, :, -1])

    def step(h, inp):
        st, dec = inp
        return dec[..., None, None] * h + st, h

    init = jnp.zeros((bsz, G, R, SSM_HEAD_DIM, SSM_STATE), f32)
    _, prev = lax.scan(step, init, (jnp.moveaxis(states, 1, 0), jnp.moveaxis(chunk_decay, 1, 0)))
    prev = jnp.moveaxis(prev, 0, 1)
    y_off = jnp.einsum('bclgn,bcgrpn->bclgrp', cm, prev) * jnp.exp(cs)[..., None]
    return (y_diag + y_off).reshape(bsz, seq, SSM_HEADS, SSM_HEAD_DIM)


def setup_inputs(seed: int = 0) -> dict:
    key = jax.random.key(seed)
    ks = iter(jax.random.split(key, 40))

    def nrm(shape, scale):
        return jax.random.normal(next(ks), shape, f32) * scale

    def gain(shape):
        return 1.0 + nrm(shape, 0.02)

    L = DEPTH
    x = nrm((BATCH, SEQ, D_MODEL), 1.0)
    c = nrm((BATCH, D_MODEL), 1.0)
    positions = jnp.arange(SEQ, dtype=jnp.int32)[None, :] + jax.random.randint(
        next(ks), (BATCH, 1), 0, 1024, dtype=jnp.int32)
    rel_bias = nrm((REL_BUCKETS, DA_HEADS), 0.5)
    ada_w = nrm((L, D_MODEL, 6 * D_MODEL), D_MODEL ** -0.5)
    ada_b = nrm((L, 6 * D_MODEL), 0.02)
    norm_mix = gain((L, D_MODEL))
    norm_ffn = gain((L, D_MODEL))
    w_in = nrm((L, D_MODEL, IN_COLS), D_MODEL ** -0.5)
    da_q_norm = gain((L, DA_HEAD_DIM))
    da_k_norm = gain((L, DA_HEAD_DIM))
    da_lambda = nrm((L, 4, DA_HEAD_DIM), 0.1)
    da_subln = gain((L, 2 * DA_HEAD_DIM))
    mla_q_a_norm = gain((L, MLA_Q_RANK))
    mla_kv_a_norm = gain((L, MLA_KV_RANK))
    mla_w_uq = nrm((L, MLA_Q_RANK, MLA_HEADS * MLA_QK), MLA_Q_RANK ** -0.5)
    mla_w_ukv = nrm((L, MLA_KV_RANK, MLA_HEADS * (MLA_NOPE + MLA_V)), MLA_KV_RANK ** -0.5)
    mla_q_norm = gain((L, MLA_QK))
    mla_k_norm = gain((L, MLA_QK))
    ssm_conv_w = nrm((L, SSM_CONV, SSM_CONV_DIM), SSM_CONV ** -0.5)
    ssm_conv_b = nrm((L, SSM_CONV_DIM), 0.02)
    dt0 = jnp.exp(jax.random.uniform(next(ks), (L, SSM_HEADS), f32, math.log(1e-3), math.log(1e-1)))
    ssm_dt_bias = dt0 + jnp.log(-jnp.expm1(-dt0))
    ssm_a_log = jnp.log(jax.random.uniform(next(ks), (L, SSM_HEADS), f32, 1.0, 16.0))
    ssm_d = 1.0 + nrm((L, SSM_HEADS), 0.1)
    ssm_norm = gain((L, SSM_D_INNER))
    w_branch_a = nrm((L, DA_WIDTH, D_MODEL), DA_WIDTH ** -0.5)
    w_branch_b = nrm((L, MLA_WIDTH, D_MODEL), MLA_WIDTH ** -0.5)
    w_branch_c = nrm((L, SSM_D_INNER, D_MODEL), SSM_D_INNER ** -0.5)
    w_out = nrm((L, D_MODEL, D_MODEL), D_MODEL ** -0.5)
    ffn_w_up = nrm((L, D_MODEL, 2 * FFN_DIM), D_MODEL ** -0.5)
    ffn_conv_w = nrm((L, FFN_CONV, 2 * FFN_DIM), FFN_CONV ** -0.5)
    ffn_conv_b = nrm((L, 2 * FFN_DIM), 0.02)
    ffn_w_down = nrm((L, FFN_DIM, D_MODEL), FFN_DIM ** -0.5)
    return {'x': x, 'c': c, 'positions': positions, 'rel_bias': rel_bias,
            'ada_w': ada_w, 'ada_b': ada_b, 'norm_mix': norm_mix, 'norm_ffn': norm_ffn, 'w_in': w_in,
            'da_q_norm': da_q_norm, 'da_k_norm': da_k_norm, 'da_lambda': da_lambda, 'da_subln': da_subln,
            'mla_q_a_norm': mla_q_a_norm, 'mla_kv_a_norm': mla_kv_a_norm, 'mla_w_uq': mla_w_uq,
            'mla_w_ukv': mla_w_ukv, 'mla_q_norm': mla_q_norm, 'mla_k_norm': mla_k_norm,
            'ssm_conv_w': ssm_conv_w, 'ssm_conv_b': ssm_conv_b, 'ssm_dt_bias': ssm_dt_bias,
            'ssm_a_log': ssm_a_log, 'ssm_d': ssm_d, 'ssm_norm': ssm_norm,
            'w_branch_a': w_branch_a, 'w_branch_b': w_branch_b, 'w_branch_c': w_branch_c, 'w_out': w_out,
            'ffn_w_up': ffn_w_up, 'ffn_conv_w': ffn_conv_w, 'ffn_conv_b': ffn_conv_b, 'ffn_w_down': ffn_w_down}


def reference(x, c, positions, rel_bias, ada_w, ada_b, norm_mix, norm_ffn, w_in,
              da_q_norm, da_k_norm, da_lambda, da_subln,
              mla_q_a_norm, mla_kv_a_norm, mla_w_uq, mla_w_ukv, mla_q_norm, mla_k_norm,
              ssm_conv_w, ssm_conv_b, ssm_dt_bias, ssm_a_log, ssm_d, ssm_norm,
              w_branch_a, w_branch_b, w_branch_c, w_out,
              ffn_w_up, ffn_conv_w, ffn_conv_b, ffn_w_down):
    bsz, seq, _ = x.shape
    inv_freq = ROPE_THETA ** (-jnp.arange(0, MLA_ROPE, 2, dtype=f32) / MLA_ROPE)
    ang = positions.astype(f32)[..., None] * inv_freq
    cos, sin = jnp.cos(ang), jnp.sin(ang)
    c_act = jax.nn.silu(c)

    for l in range(DEPTH):
        mod = c_act @ ada_w[l] + ada_b[l]
        sh_m, sc_m, g_m, sh_f, sc_f, g_f = jnp.split(mod[:, None, :], 6, axis=-1)

        h = _rmsnorm(x, norm_mix[l]) * (1 + sc_m) + sh_m
        proj = h @ w_in[l]
        qa, ka, va, cq, ckv, kr, z, xbc, dt_raw, gates = jnp.split(proj, _IN_OFFSETS, axis=-1)

        qa = _rmsnorm(qa.reshape(bsz, seq, DA_HEADS, 2, DA_HEAD_DIM), da_q_norm[l]).transpose(0, 2, 1, 3, 4)
        ka = _rmsnorm(ka.reshape(bsz, seq, DA_HEADS, 2, DA_HEAD_DIM), da_k_norm[l]).transpose(0, 2, 1, 3, 4)
        va = va.reshape(bsz, seq, DA_HEADS, 2 * DA_HEAD_DIM).transpose(0, 2, 1, 3)
        lam_init = 0.8 - 0.6 * math.exp(-0.3 * l)
        lp = da_lambda[l].astype(f32)
        lam = jnp.exp(jnp.sum(lp[0] * lp[1])) - jnp.exp(jnp.sum(lp[2] * lp[3])) + lam_init
        oa = _diff_attention(qa, ka, va, positions, rel_bias, lam)
        oa = _rmsnorm(oa, da_subln[l]) * (1.0 - lam_init)
        ya = oa.transpose(0, 2, 1, 3).reshape(bsz, seq, DA_WIDTH)

        cq = _rmsnorm(cq, mla_q_a_norm[l])
        qb = (cq @ mla_w_uq[l]).reshape(bsz, seq, MLA_HEADS, MLA_QK)
        ckv = _rmsnorm(ckv, mla_kv_a_norm[l])
        kv = (ckv @ mla_w_ukv[l]).reshape(bsz, seq, MLA_HEADS, MLA_NOPE + MLA_V)
        k_nope, vb = kv[..., :MLA_NOPE], kv[..., MLA_NOPE:]
        kb = jnp.concatenate([k_nope, jnp.broadcast_to(kr[:, :, None, :], (bsz, seq, MLA_HEADS, MLA_ROPE))], axis=-1)
        qb = _rmsnorm(qb, mla_q_norm[l])
        kb = _rmsnorm(kb, mla_k_norm[l])
        cos_h, sin_h = cos[:, :, None, :], sin[:, :, None, :]
        qb = jnp.concatenate([qb[..., :MLA_NOPE], _rope(qb[..., MLA_NOPE:], cos_h, sin_h)], axis=-1)
        kb = jnp.concatenate([kb[..., :MLA_NOPE], _rope(kb[..., MLA_NOPE:], cos_h, sin_h)], axis=-1)
        ob = _mla_attention(qb.transpose(0, 2, 1, 3), kb.transpose(0, 2, 1, 3), vb.transpose(0, 2, 1, 3))
        yb = ob.transpose(0, 2, 1, 3).reshape(bsz, seq, MLA_WIDTH)

        xbc = jax.nn.silu(_causal_dwconv(xbc, ssm_conv_w[l], ssm_conv_b[l]))
        xs, bm, cm = jnp.split(xbc, [SSM_D_INNER, SSM_D_INNER + SSM_GROUPS * SSM_STATE], axis=-1)
        xs = xs.reshape(bsz, seq, SSM_HEADS, SSM_HEAD_DIM)
        bm = bm.reshape(bsz, seq, SSM_GROUPS, SSM_STATE)
        cm = cm.reshape(bsz, seq, SSM_GROUPS, SSM_STATE)
        dt = jax.nn.softplus(dt_raw.astype(f32) + ssm_dt_bias[l].astype(f32))
        a = -jnp.exp(ssm_a_log[l].astype(f32))
        ys = _ssd(xs, dt, a, bm, cm) + ssm_d[l].astype(f32)[:, None] * xs.astype(f32)
        ys = ys.reshape(bsz, seq, SSM_D_INNER) * jax.nn.silu(z.astype(f32))
        ys = _rmsnorm(ys.reshape(bsz, seq, SSM_GROUPS, -1), ssm_norm[l].reshape(SSM_GROUPS, -1))
        ys = ys.reshape(bsz, seq, SSM_D_INNER).astype(x.dtype)

        ga, gb, gc = jnp.split(jax.nn.sigmoid(gates), 3, axis=-1)
        merged = ga * (ya @ w_branch_a[l]) + gb * (yb @ w_branch_b[l]) + gc * (ys @ w_branch_c[l])
        x = x + g_m * (merged @ w_out[l])

        h = _rmsnorm(x, norm_ffn[l]) * (1 + sc_f) + sh_f
        u = _causal_dwconv(h @ ffn_w_up[l], ffn_conv_w[l], ffn_conv_b[l])
        ua, uv = jnp.split(u, 2, axis=-1)
        x = x + g_f * ((jax.nn.silu(ua) * uv) @ ffn_w_down[l])
    return x
```

```python
import functools
import math

import jax
import jax.numpy as jnp
import numpy as np
from jax import lax
from jax.experimental import pallas as pl
from jax.experimental.pallas import tpu as pltpu

f32 = jnp.float32
bf16 = jnp.bfloat16

D_MODEL = 1024
DA_HEADS = 4
DA_HEAD_DIM = 64
MLA_HEADS = 8
MLA_Q_RANK = 256
MLA_KV_RANK = 128
MLA_NOPE = 64
MLA_ROPE = 32
MLA_V = 64
MLA_QK = MLA_NOPE + MLA_ROPE
SSM_D_INNER = 1024
SSM_HEAD_DIM = 64
SSM_HEADS = 16
SSM_GROUPS = 4
SSM_STATE = 128
SSM_CONV = 4
SSM_CHUNK = 128
SSM_CONV_DIM = SSM_D_INNER + 2 * SSM_GROUPS * SSM_STATE
FFN_DIM = 2816
FFN_CONV = 3
REL_BUCKETS = 32
REL_MAX_DIST = 128
ROPE_THETA = 10000.0
EPS = 1e-6

LANES = 128
SUBLANES = 8
VMEM_LIMIT = 56 * 1024 * 1024

IN_TN = 512
QKV_COLS = 3 * 512
R_XBC = 0
R_Z = R_XBC + SSM_CONV_DIM
R_GATES = R_Z + SSM_D_INNER
R_LAT = R_GATES + 3 * D_MODEL
REST_COLS = R_LAT + 512
MISC_KR_LANE = 64
NEG_BIG = -1e30

ATT_TQ = 256


def _cparams(sem):
    return pltpu.CompilerParams(dimension_semantics=sem, vmem_limit_bytes=VMEM_LIMIT)


def _lane_iota(shape):
    return lax.broadcasted_iota(jnp.int32, shape, len(shape) - 1)


def _dot(a, b):
    return jnp.dot(a, b, preferred_element_type=f32)


def _dot_nt(a, b):
    return lax.dot_general(a, b, (((1,), (1,)), ((), ())), preferred_element_type=f32)


def _sigmoid(x):
    return 1.0 / (1.0 + jnp.exp(-x))


def _silu(x):
    return x * _sigmoid(x)


def _ada_kernel(c_ref, w_ref, b_ref, o_ref):
    ca = _silu(c_ref[...]).astype(bf16)
    o_ref[...] = _dot(ca, w_ref[...].astype(bf16)) + b_ref[...]


def _ada_mod(c, ada_w, ada_b):
    nl, d, n = ada_w.shape
    bsz = c.shape[0]
    tn = 1024
    return pl.pallas_call(
        _ada_kernel,
        grid=(nl, n // tn),
        in_specs=[
            pl.BlockSpec((bsz, d), lambda l, j: (0, 0)),
            pl.BlockSpec((None, d, tn), lambda l, j: (l, 0, j)),
            pl.BlockSpec((None, 1, tn), lambda l, j: (l, 0, j)),
        ],
        out_specs=pl.BlockSpec((None, bsz, tn), lambda l, j: (l, 0, j)),
        out_shape=jax.ShapeDtypeStruct((nl, bsz, n), f32),
        compiler_params=_cparams(("arbitrary", "arbitrary")),
        name="ada_mod",
    )(c, ada_w, ada_b.reshape(nl, 1, n))


def _rope_kernel(pos_ref, freq_ref, c_ref, sp_ref, sm_ref):
    ang = pos_ref[...].astype(f32) * freq_ref[...]
    cos, sin = jnp.cos(ang), jnp.sin(ang)
    lane = _lane_iota(ang.shape)
    lo, mid, hi = MISC_KR_LANE, MISC_KR_LANE + MLA_ROPE // 2, MISC_KR_LANE + MLA_ROPE
    c_ref[...] = jnp.where((lane >= lo) & (lane < hi), cos, 1.0)
    sp_ref[...] = jnp.where((lane >= mid) & (lane < hi), sin, 0.0)
    sm_ref[...] = jnp.where((lane >= lo) & (lane < mid), -sin, 0.0)


def _rope_tables(positions):
    t = positions.size
    tm = min(t, 2048)
    inv_freq = ROPE_THETA ** (-jnp.arange(0, MLA_ROPE, 2, dtype=f32) / MLA_ROPE)
    freq = jnp.zeros((LANES,), f32).at[MISC_KR_LANE:MISC_KR_LANE + MLA_ROPE].set(jnp.tile(inv_freq, 2))
    out = jax.ShapeDtypeStruct((t, LANES), f32)
    spec = pl.BlockSpec((tm, LANES), lambda i: (i, 0))
    return pl.pallas_call(
        _rope_kernel,
        grid=(t // tm,),
        in_specs=[pl.BlockSpec((tm, 1), lambda i: (i, 0)), pl.BlockSpec((1, LANES), lambda i: (0, 0))],
        out_specs=[spec, spec, spec],
        out_shape=[out, out, out],
        compiler_params=_cparams(("arbitrary",)),
        name="rope_tables",
    )(positions.reshape(t, 1), freq.reshape(1, LANES))


def _bias_kernel(rb_ref, o_ref):
    h, which = pl.program_id(0), pl.program_id(1)
    tq = o_ref.shape[-1]
    row = lax.broadcasted_iota(jnp.int32, (tq, tq), 0)
    col = lax.broadcasted_iota(jnp.int32, (tq, tq), 1)
    d = jnp.maximum(row - col + which * tq, 0)
    max_exact = REL_BUCKETS // 2
    large = max_exact + (jnp.log(jnp.maximum(d, 1).astype(f32) / max_exact)
                         / math.log(REL_MAX_DIST / max_exact) * (REL_BUCKETS - max_exact)).astype(jnp.int32)
    large = jnp.minimum(large, REL_BUCKETS - 1)
    bucket = jnp.where(d < max_exact, d, large)
    val = jnp.full((tq, tq), rb_ref[0, h], f32)
    for k in range(1, REL_BUCKETS):
        val = jnp.where(bucket == k, rb_ref[k, h], val)
    o_ref[...] = val - rb_ref[REL_BUCKETS - 1, h]


def _bias_tiles(rel_bias, tq):
    return pl.pallas_call(
        _bias_kernel,
        grid=(DA_HEADS, 2),
        in_specs=[pl.BlockSpec(memory_space=pltpu.SMEM)],
        out_specs=pl.BlockSpec((None, None, tq, tq), lambda h, w: (h, w, 0, 0)),
        out_shape=jax.ShapeDtypeStruct((DA_HEADS, 2, tq, tq), f32),
        compiler_params=_cparams(("arbitrary", "arbitrary")),
        name="bias_tiles",
    )(rel_bias)


def _inproj_kernel(x_ref, nw_ref, mod_ref, w_ref, qkw_ref, bd_ref, qkv_ref, rest_ref, h_ref):
    j = pl.program_id(1)

    @pl.when(j == 0)
    def _():
        x = x_ref[...]
        y = x * lax.rsqrt(jnp.mean(x * x, axis=-1, keepdims=True) + EPS) * nw_ref[...]
        h_ref[...] = (y * (1.0 + mod_ref[1:2, :]) + mod_ref[0:1, :]).astype(bf16)

    acc = _dot(h_ref[...], w_ref[...])

    @pl.when(j < 2)
    def _():
        ss = _dot((acc * acc).astype(bf16), bd_ref[...]) * (1.0 / DA_HEAD_DIM)
        qkv_ref[...] = (acc * lax.rsqrt(ss + EPS) * qkw_ref[...]).astype(bf16)

    @pl.when(j == 2)
    def _():
        qkv_ref[...] = acc.astype(bf16)

    g_lo = 3 + R_GATES // IN_TN
    g_hi = 3 + R_LAT // IN_TN

    @pl.when((j >= 3) & ((j < g_lo) | (j >= g_hi)))
    def _():
        rest_ref[...] = acc

    @pl.when((j >= g_lo) & (j < g_hi))
    def _():
        rest_ref[...] = _sigmoid(acc)


def _inproj(x2, norm_w, mod, w_p, qk_w, bd, l, rows_per_batch):
    t, d = x2.shape
    tm = min(1024, rows_per_batch)
    tpb = rows_per_batch // tm
    nj = (QKV_COLS + REST_COLS) // IN_TN
    return pl.pallas_call(
        _inproj_kernel,
        grid=(t // tm, nj),
        in_specs=[
            pl.BlockSpec((tm, d), lambda i, j: (i, 0)),
            pl.BlockSpec((None, 1, d), lambda i, j: (l, 0, 0)),
            pl.BlockSpec((None, None, 6, d), lambda i, j: (l, i // tpb, 0, 0)),
            pl.BlockSpec((None, d, IN_TN), lambda i, j: (l, 0, j)),
            pl.BlockSpec((None, None, 1, IN_TN), lambda i, j: (l, jnp.minimum(j, 1), 0, 0)),
            pl.BlockSpec((IN_TN, IN_TN), lambda i, j: (0, 0)),
        ],
        out_specs=[
            pl.BlockSpec((tm, IN_TN), lambda i, j: (i, jnp.minimum(j, 2))),
            pl.BlockSpec((tm, IN_TN), lambda i, j: (i, jnp.maximum(j - 3, 0))),
        ],
        out_shape=[jax.ShapeDtypeStruct((t, QKV_COLS), bf16), jax.ShapeDtypeStruct((t, REST_COLS), f32)],
        scratch_shapes=[pltpu.VMEM((tm, d), bf16)],
        compiler_params=_cparams(("arbitrary", "arbitrary")),
        name="inproj",
    )(x2, norm_w, mod, w_p, qk_w, bd)


def _rope_apply(x, c, sp, sm):
    return x * c + pltpu.roll(x, 16, 1) * sp + pltpu.roll(x, LANES - 16, 1) * sm


def _mla_prep_kernel(lat_ref, qan_ref, kvan_ref, wuq_ref, wuk_ref, wuv_ref, qn_ref, kn_ref,
                     c_ref, sp_ref, sm_ref, q_ref, k_ref, v_ref):
    lat = lat_ref[...]
    c, sp, sm = c_ref[...], sp_ref[...], sm_ref[...]
    cq = lat[:, :MLA_Q_RANK]
    cqn = (cq * lax.rsqrt(jnp.mean(cq * cq, axis=-1, keepdims=True) + EPS) * qan_ref[...]).astype(bf16)
    ckv = lat[:, MLA_Q_RANK:MLA_Q_RANK + MLA_KV_RANK]
    ckvn = (ckv * lax.rsqrt(jnp.mean(ckv * ckv, axis=-1, keepdims=True) + EPS) * kvan_ref[...]).astype(bf16)
    misc = lat[:, MLA_Q_RANK + MLA_KV_RANK:]
    lane = _lane_iota(misc.shape)
    krb = jnp.where((lane >= MISC_KR_LANE) & (lane < MISC_KR_LANE + MLA_ROPE), misc, 0.0)
    qf = _dot(cqn, wuq_ref[...])
    kf = _dot(ckvn, wuk_ref[...])
    v_ref[...] = _dot(ckvn, wuv_ref[...]).astype(bf16)
    inv = 1.0 / MLA_QK
    for h in range(MLA_HEADS):
        sl = slice(h * LANES, (h + 1) * LANES)
        qh = qf[:, sl]
        qh = qh * lax.rsqrt(jnp.sum(qh * qh, axis=-1, keepdims=True) * inv + EPS) * qn_ref[...]
        q_ref[:, sl] = _rope_apply(qh, c, sp, sm).astype(bf16)
        kh = kf[:, sl] + krb
        kh = kh * lax.rsqrt(jnp.sum(kh * kh, axis=-1, keepdims=True) * inv + EPS) * kn_ref[...]
        k_ref[:, sl] = _rope_apply(kh, c, sp, sm).astype(bf16)


def _mla_prep(rest, qan, kvan, wuq, wuk, wuv, qn, kn, ctab, sptab, smtab, l):
    t = rest.shape[0]
    tm = min(512, t)
    lat_blk = R_LAT // 512
    hq = MLA_HEADS * LANES
    vec = lambda n: pl.BlockSpec((None, 1, n), lambda i: (l, 0, 0))
    tab = pl.BlockSpec((tm, LANES), lambda i: (i, 0))
    return pl.pallas_call(
        _mla_prep_kernel,
        grid=(t // tm,),
        in_specs=[
            pl.BlockSpec((tm, 512), lambda i: (i, lat_blk)),
            vec(MLA_Q_RANK), vec(MLA_KV_RANK),
            pl.BlockSpec((None, MLA_Q_RANK, hq), lambda i: (l, 0, 0)),
            pl.BlockSpec((None, MLA_KV_RANK, hq), lambda i: (l, 0, 0)),
            pl.BlockSpec((None, MLA_KV_RANK, MLA_HEADS * MLA_V), lambda i: (l, 0, 0)),
            vec(LANES), vec(LANES), tab, tab, tab,
        ],
        out_specs=[
            pl.BlockSpec((tm, hq), lambda i: (i, 0)),
            pl.BlockSpec((tm, hq), lambda i: (i, 0)),
            pl.BlockSpec((tm, MLA_HEADS * MLA_V), lambda i: (i, 0)),
        ],
        out_shape=[jax.ShapeDtypeStruct((t, hq), bf16), jax.ShapeDtypeStruct((t, hq), bf16),
                   jax.ShapeDtypeStruct((t, MLA_HEADS * MLA_V), bf16)],
        compiler_params=_cparams(("arbitrary",)),
        name="mla_prep",
    )(rest, qan, kvan, wuq, wuk, wuv, qn, kn, ctab, sptab, smtab)


def _flash(qs, k_ref, koffs, v_ref, m_ref, l_ref, acc_ref, bias_ref, qi, tq):
    nmap = len(qs)
    m_ref[...] = jnp.full(m_ref.shape, -jnp.inf, f32)
    l_ref[...] = jnp.zeros(l_ref.shape, f32)
    acc_ref[...] = jnp.zeros(acc_ref.shape, f32)

    def step(kj, bias, masked):
        start = pl.multiple_of(kj * tq, tq)
        v = v_ref[pl.ds(start, tq), :]
        if masked:
            row = lax.broadcasted_iota(jnp.int32, (tq, tq), 0)
            col = lax.broadcasted_iota(jnp.int32, (tq, tq), 1)
            keep = col <= row
        for a in range(nmap):
            k = k_ref[pl.ds(start, tq), koffs[a]:koffs[a] + LANES]
            s = _dot_nt(qs[a], k)
            if bias is not None:
                s = s + bias
            if masked:
                s = jnp.where(keep, s, NEG_BIG)
            m_prev = m_ref[a]
            m_new = jnp.maximum(m_prev, jnp.max(s, axis=-1, keepdims=True))
            alpha = jnp.exp(m_prev - m_new)
            e = jnp.exp(s - m_new)
            l_ref[a] = alpha * l_ref[a] + jnp.sum(e, axis=-1, keepdims=True)
            acc_ref[a] = alpha * acc_ref[a] + _dot(e.astype(bf16), v)
            m_ref[a] = m_new

    def body(kj, carry):
        step(kj, None, False)
        return carry

    lax.fori_loop(0, qi - 1, body, 0)

    @pl.when(qi >= 1)
    def _():
        step(qi - 1, None if bias_ref is None else bias_ref[1], False)

    step(qi, None if bias_ref is None else bias_ref[0], True)


def _diff_attn_kernel(q_ref, k_ref, v_ref, bias_ref, lp_ref, sub_ref, o_ref, m_ref, l_ref, acc_ref, *, lam_init, tq):
    qi = pl.program_id(2)
    q = q_ref[...]
    lane = _lane_iota(q.shape)
    zero = jnp.zeros_like(q)
    q0 = jnp.where(lane < DA_HEAD_DIM, q, zero)
    q1 = jnp.where(lane >= DA_HEAD_DIM, q, zero)
    _flash([q0, q1], k_ref, [0, 0], v_ref, m_ref, l_ref, acc_ref, bias_ref, qi, tq)
    lp = lp_ref[...]
    lam = (jnp.exp(jnp.sum(lp[0:1] * lp[1:2], axis=-1, keepdims=True))
           - jnp.exp(jnp.sum(lp[2:3] * lp[3:4], axis=-1, keepdims=True)) + lam_init)
    o = acc_ref[0] / l_ref[0] - lam * (acc_ref[1] / l_ref[1])
    o = o * lax.rsqrt(jnp.mean(o * o, axis=-1, keepdims=True) + EPS) * sub_ref[...]
    o_ref[...] = (o * (1.0 - lam_init)).astype(bf16)


def _diff_attn(qkv3, bias, da_lambda, da_subln, l, lam_init):
    bsz, s, _ = qkv3.shape
    tq = bias.shape[-1]
    nq = s // tq
    kern = functools.partial(_diff_attn_kernel, lam_init=lam_init, tq=tq)
    return pl.pallas_call(
        kern,
        grid=(bsz, DA_HEADS, nq),
        in_specs=[
            pl.BlockSpec((None, tq, LANES), lambda b, h, i: (b, i, h)),
            pl.BlockSpec((None, s, LANES), lambda b, h, i: (b, 0, DA_HEADS + h)),
            pl.BlockSpec((None, s, LANES), lambda b, h, i: (b, 0, 2 * DA_HEADS + h)),
            pl.BlockSpec((None, 2, tq, tq), lambda b, h, i: (h, 0, 0, 0)),
            pl.BlockSpec((None, 4, DA_HEAD_DIM), lambda b, h, i: (l, 0, 0)),
            pl.BlockSpec((None, 1, LANES), lambda b, h, i: (l, 0, 0)),
        ],
        out_specs=pl.BlockSpec((None, tq, LANES), lambda b, h, i: (b, i, h)),
        out_shape=jax.ShapeDtypeStruct((bsz, s, DA_HEADS * LANES), bf16),
        scratch_shapes=[pltpu.VMEM((2, tq, 1), f32), pltpu.VMEM((2, tq, 1), f32), pltpu.VMEM((2, tq, LANES), f32)],
        compiler_params=_cparams(("arbitrary", "arbitrary", "arbitrary")),
        name="diff_attn",
    )(qkv3, qkv3, qkv3, bias, da_lambda, da_subln)


def _mla_attn_kernel(q_ref, k_ref, v_ref, o_ref, m_ref, l_ref, acc_ref, *, tq):
    qi = pl.program_id(2)
    q = q_ref[...]
    qs = [q[:, :LANES], q[:, LANES:]]
    _flash(qs, k_ref, [0, LANES], v_ref, m_ref, l_ref, acc_ref, None, qi, tq)
    o0 = acc_ref[0] / l_ref[0]
    o1 = acc_ref[1] / l_ref[1]
    lane = _lane_iota(o0.shape)
    o_ref[...] = jnp.where(lane < MLA_V, o0, o1).astype(bf16)


def _mla_attn(q3, k3, v3, tq):
    bsz, s, _ = q3.shape
    nq = s // tq
    npair = MLA_HEADS // 2
    kern = functools.partial(_mla_attn_kernel, tq=tq)
    return pl.pallas_call(
        kern,
        grid=(bsz, npair, nq),
        in_specs=[
            pl.BlockSpec((None, tq, 2 * LANES), lambda b, p, i: (b, i, p)),
            pl.BlockSpec((None, s, 2 * LANES), lambda b, p, i: (b, 0, p)),
            pl.BlockSpec((None, s, LANES), lambda b, p, i: (b, 0, p)),
        ],
        out_specs=pl.BlockSpec((None, tq, LANES), lambda b, p, i: (b, i, p)),
        out_shape=jax.ShapeDtypeStruct((bsz, s, MLA_HEADS * MLA_V), bf16),
        scratch_shapes=[pltpu.VMEM((2, tq, 1), f32), pltpu.VMEM((2, tq, 1), f32), pltpu.VMEM((2, tq, LANES), f32)],
        compiler_params=_cparams(("arbitrary", "arbitrary", "arbitrary")),
        name="mla_attn",
    )(q3, k3, v3)


def _split3(x):
    hi = x.astype(bf16)
    r1 = x - hi.astype(f32)
    mid = r1.astype(bf16)
    lo = (r1 - mid.astype(f32)).astype(bf16)
    return hi, mid, lo


def _pair_bcast(x, j, lane):
    return jnp.where(lane < SSM_HEAD_DIM, x[:, 2 * j:2 * j + 1], x[:, 2 * j + 1:2 * j + 2])


def _ssd_kernel(xbc_ref, z_ref, misc_ref, cw_ref, cb_ref, dtb_ref, alog_ref, dvec_ref, nw_ref,
                o_ref, ext_ref, st_ref, y_ref):
    c = pl.program_id(1)
    L = SSM_CHUNK
    G, N = SSM_GROUPS, SSM_STATE

    @pl.when(c == 0)
    def _():
        ext_ref[0:SUBLANES, :] = jnp.zeros((SUBLANES, SSM_CONV_DIM), f32)
        st_ref[...] = jnp.zeros(st_ref.shape, f32)

    ext_ref[SUBLANES:, :] = xbc_ref[...]
    conv = cb_ref[...] + cw_ref[SSM_CONV - 1:SSM_CONV, :] * ext_ref[SUBLANES:, :]
    for s in range(1, SSM_CONV):
        conv = conv + cw_ref[SSM_CONV - 1 - s:SSM_CONV - s, :] * ext_ref[SUBLANES - s:SUBLANES - s + L, :]
    ext_ref[0:SUBLANES, :] = ext_ref[L:L + SUBLANES, :]
    xa = _silu(conv)
    xs = xa[:, :SSM_D_INNER]
    bm = xa[:, SSM_D_INNER:SSM_D_INNER + G * N]
    cm = xa[:, SSM_D_INNER + G * N:]

    lane = _lane_iota((L, LANES))
    lane1 = _lane_iota((1, LANES))
    dt = jax.nn.softplus(misc_ref[...] + dtb_ref[...])
    a_l = jnp.where(lane1 < SSM_HEADS, -jnp.exp(alog_ref[...]), 0.0)
    da = dt * a_l
    row = lax.broadcasted_iota(jnp.int32, (L, L), 0)
    col = lax.broadcasted_iota(jnp.int32, (L, L), 1)
    causal = col <= row
    tri = jnp.where(causal, 1.0, 0.0).astype(bf16)
    hi, mid, lo = _split3(da)
    cs = _dot(tri, hi) + _dot(tri, mid) + _dot(tri, lo)
    cs_t = cs.T
    cs_last = cs[L - 1:L, :]
    w_end = dt * jnp.exp(cs_last - cs)
    e_cs = jnp.exp(cs)
    cdec = jnp.exp(cs_last)

    for g in range(G):
        bm_g = bm[:, g * N:(g + 1) * N]
        cm_g = cm[:, g * N:(g + 1) * N].astype(bf16)
        cb = _dot_nt(cm_g, bm_g.astype(bf16))
        bm_t = bm_g.T.astype(bf16)
        for jj in range(2):
            j = 2 * g + jj
            sl = slice(j * LANES, (j + 1) * LANES)
            psl = slice(jj * LANES, (jj + 1) * LANES)
            xs_p = xs[:, sl]
            xdt = (xs_p * _pair_bcast(dt, j, lane)).astype(bf16)
            yd = []
            for hh in (2 * j, 2 * j + 1):
                seg = cs[:, hh:hh + 1] - cs_t[hh:hh + 1, :]
                decay = jnp.exp(jnp.where(causal, seg, -jnp.inf))
                yd.append(_dot((cb * decay).astype(bf16), xdt))
            y_diag = jnp.where(lane < SSM_HEAD_DIM, yd[0], yd[1])
            prev = st_ref[g, :, psl]
            y_off = _dot(cm_g, prev.astype(bf16)) * _pair_bcast(e_cs, j, lane)
            y_ref[:, sl] = y_diag + y_off + dvec_ref[:, sl] * xs_p
            xw = (xs_p * _pair_bcast(w_end, j, lane)).astype(bf16)
            cd = jnp.where(lane1 < SSM_HEAD_DIM, cdec[:, 2 * j:2 * j + 1], cdec[:, 2 * j + 1:2 * j + 2])
            st_ref[g, :, psl] = prev * cd + _dot(bm_t, xw)

    gw = SSM_D_INNER // G
    for g in range(G):
        sl = slice(g * gw, (g + 1) * gw)
        yz = y_ref[:, sl] * _silu(z_ref[:, sl])
        yz = yz * lax.rsqrt(jnp.mean(yz * yz, axis=-1, keepdims=True) + EPS) * nw_ref[:, sl]
        o_ref[:, sl] = yz.astype(bf16)


def _ssd(rest3, conv_w, conv_b, dt_bias, a_log, dvec, norm_w, l):
    bsz, s, _ = rest3.shape
    L = SSM_CHUNK
    nc = s // L
    misc_blk = (R_LAT + MLA_Q_RANK + MLA_KV_RANK) // LANES
    vec = lambda n: pl.BlockSpec((None, 1, n), lambda b, c: (l, 0, 0))
    return pl.pallas_call(
        _ssd_kernel,
        grid=(bsz, nc),
        in_specs=[
            pl.BlockSpec((None, L, SSM_CONV_DIM), lambda b, c: (b, c, R_XBC // SSM_CONV_DIM)),
            pl.BlockSpec((None, L, SSM_D_INNER), lambda b, c: (b, c, R_Z // SSM_D_INNER)),
            pl.BlockSpec((None, L, LANES), lambda b, c: (b, c, misc_blk)),
            pl.BlockSpec((None, SSM_CONV, SSM_CONV_DIM), lambda b, c: (l, 0, 0)),
            vec(SSM_CONV_DIM), vec(LANES), vec(LANES), vec(SSM_D_INNER), vec(SSM_D_INNER),
        ],
        out_specs=pl.BlockSpec((None, L, SSM_D_INNER), lambda b, c: (b, c, 0)),
        out_shape=jax.ShapeDtypeStruct((bsz, s, SSM_D_INNER), bf16),
        scratch_shapes=[
            pltpu.VMEM((L + SUBLANES, SSM_CONV_DIM), f32),
            pltpu.VMEM((SSM_GROUPS, SSM_STATE, 4 * SSM_HEAD_DIM), f32),
            pltpu.VMEM((L, SSM_D_INNER), f32),
        ],
        compiler_params=_cparams(("arbitrary", "arbitrary")),
        name="ssd",
    )(rest3, rest3, rest3, conv_w, conv_b, dt_bias, a_log, dvec, norm_w)


def _merge_kernel(ya_ref, yb_ref, ys_ref, ga_ref, gb_ref, gc_ref, x_ref, mod_ref,
                  wa_ref, wb_ref, wc_ref, wo_ref, o_ref):
    merged = (ga_ref[...] * _dot(ya_ref[...], wa_ref[...])
              + gb_ref[...] * _dot(yb_ref[...], wb_ref[...])
              + gc_ref[...] * _dot(ys_ref[...], wc_ref[...]))
    o_ref[...] = x_ref[...] + mod_ref[2:3, :] * _dot(merged.astype(bf16), wo_ref[...])


def _merge(ya, yb, ys, rest, x2, mod, wa, wb, wc, wo, l, rows_per_batch):
    t, d = x2.shape
    tm = min(512, rows_per_batch)
    tpb = rows_per_batch // tm
    gblk = R_GATES // d
    row = lambda n: pl.BlockSpec((tm, n), lambda i: (i, 0))
    wspec = lambda k: pl.BlockSpec((None, k, d), lambda i: (l, 0, 0))
    return pl.pallas_call(
        _merge_kernel,
        grid=(t // tm,),
        in_specs=[
            row(ya.shape[1]), row(yb.shape[1]), row(ys.shape[1]),
            pl.BlockSpec((tm, d), lambda i: (i, gblk)),
            pl.BlockSpec((tm, d), lambda i: (i, gblk + 1)),
            pl.BlockSpec((tm, d), lambda i: (i, gblk + 2)),
            row(d),
            pl.BlockSpec((None, None, 6, d), lambda i: (l, i // tpb, 0, 0)),
            wspec(ya.shape[1]), wspec(yb.shape[1]), wspec(ys.shape[1]), wspec(d),
        ],
        out_specs=row(d),
        out_shape=jax.ShapeDtypeStruct((t, d), f32),
        compiler_params=_cparams(("arbitrary",)),
        name="merge_out",
    )(ya, yb, ys, rest, rest, rest, x2, mod, wa, wb, wc, wo)


def _ffn_kernel(x_ref, halo_ref, nw_ref, mod_ref, wua_ref, wuv_ref, cwa_ref, cwv_ref, cba_ref, cbv_ref, wd_ref,
                o_ref, h_ref, acc_ref):
    mi, fi = pl.program_id(1), pl.program_id(2)
    tm = x_ref.shape[0]

    def norm_mod(x):
        y = x * lax.rsqrt(jnp.mean(x * x, axis=-1, keepdims=True) + EPS) * nw_ref[...]
        return (y * (1.0 + mod_ref[4:5, :]) + mod_ref[3:4, :]).astype(bf16)

    @pl.when(fi == 0)
    def _():
        keep = jnp.where(mi > 0, 1.0, 0.0)
        h_ref[0:SUBLANES, :] = norm_mod(halo_ref[...] * keep) * keep.astype(bf16)
        h_ref[SUBLANES:, :] = norm_mod(x_ref[...])
        acc_ref[...] = jnp.zeros(acc_ref.shape, f32)

    h = h_ref[...]

    def conv(u, cw_ref, cb_ref):
        out = cb_ref[...] + cw_ref[FFN_CONV - 1:FFN_CONV, :] * u[SUBLANES:, :]
        for s in range(1, FFN_CONV):
            out = out + cw_ref[FFN_CONV - 1 - s:FFN_CONV - s, :] * pltpu.roll(u, s, 0)[SUBLANES:, :]
        return out

    ua = conv(_dot(h, wua_ref[...]), cwa_ref, cba_ref)
    uv = conv(_dot(h, wuv_ref[...]), cwv_ref, cbv_ref)
    act = (_silu(ua) * uv).astype(bf16)
    acc_ref[...] += _dot(act, wd_ref[...])

    @pl.when(fi == pl.num_programs(2) - 1)
    def _():
        o_ref[...] = x_ref[...] + mod_ref[5:6, :] * acc_ref[...]


def _ffn(x3, norm_w, mod, wup, conv_w, conv_b, wdown, l):
    bsz, s, d = x3.shape
    tm = min(1024, s)
    tf = 256
    nf = FFN_DIM // tf
    hb = tm // SUBLANES
    return pl.pallas_call(
        _ffn_kernel,
        grid=(bsz, s // tm, nf),
        in_specs=[
            pl.BlockSpec((None, tm, d), lambda b, m, f: (b, m, 0)),
            pl.BlockSpec((None, SUBLANES, d), lambda b, m, f: (b, jnp.maximum(m * hb - 1, 0), 0)),
            pl.BlockSpec((None, 1, d), lambda b, m, f: (l, 0, 0)),
            pl.BlockSpec((None, None, 6, d), lambda b, m, f: (l, b, 0, 0)),
            pl.BlockSpec((None, d, tf), lambda b, m, f: (l, 0, f)),
            pl.BlockSpec((None, d, tf), lambda b, m, f: (l, 0, nf + f)),
            pl.BlockSpec((None, FFN_CONV, tf), lambda b, m, f: (l, 0, f)),
            pl.BlockSpec((None, FFN_CONV, tf), lambda b, m, f: (l, 0, nf + f)),
            pl.BlockSpec((None, 1, tf), lambda b, m, f: (l, 0, f)),
            pl.BlockSpec((None, 1, tf), lambda b, m, f: (l, 0, nf + f)),
            pl.BlockSpec((None, tf, d), lambda b, m, f: (l, f, 0)),
        ],
        out_specs=pl.BlockSpec((None, tm, d), lambda b, m, f: (b, m, 0)),
        out_shape=jax.ShapeDtypeStruct((bsz, s, d), f32),
        scratch_shapes=[pltpu.VMEM((tm + SUBLANES, d), bf16), pltpu.VMEM((tm, d), f32)],
        compiler_params=_cparams(("arbitrary", "arbitrary", "arbitrary")),
        name="ffn",
    )(x3, x3, norm_w, mod, wup, wup, conv_w, conv_w, conv_b, conv_b, wdown)


def _pack_w_in(w_in):
    nl, d, _ = w_in.shape
    sizes = (512, 512, 512, MLA_Q_RANK, MLA_KV_RANK, MLA_ROPE, SSM_D_INNER, SSM_CONV_DIM, SSM_HEADS, 3 * D_MODEL)
    offs = np.concatenate([[0], np.cumsum(sizes)])
    qa, ka, va, cq, ckv, kr, z, xbc, dt, gates = [w_in[..., int(offs[i]):int(offs[i + 1])] for i in range(10)]
    zeros = lambda n: jnp.zeros((nl, d, n), w_in.dtype)
    misc = jnp.concatenate([dt, zeros(MISC_KR_LANE - SSM_HEADS), kr, zeros(LANES - MISC_KR_LANE - MLA_ROPE)], axis=-1)
    return jnp.concatenate([qa, ka, va, xbc, z, gates, cq, ckv, misc], axis=-1).astype(bf16)


def _pad_lanes(v, n):
    return jnp.pad(v, [(0, 0)] * (v.ndim - 1) + [(0, n - v.shape[-1])])


def kernel(x, c, positions, rel_bias, ada_w, ada_b, norm_mix, norm_ffn, w_in, da_q_norm, da_k_norm, da_lambda, da_subln, mla_q_a_norm, mla_kv_a_norm, mla_w_uq, mla_w_ukv, mla_q_norm, mla_k_norm, ssm_conv_w, ssm_conv_b, ssm_dt_bias, ssm_a_log, ssm_d, ssm_norm, w_branch_a, w_branch_b, w_branch_c, w_out, ffn_w_up, ffn_conv_w, ffn_conv_b, ffn_w_down):
    bsz, seq, d = x.shape
    nl = w_in.shape[0]
    t = bsz * seq
    tq = min(ATT_TQ, seq)

    w_in_p = _pack_w_in(w_in)
    qk_w = jnp.stack([jnp.tile(da_q_norm, (1, 2 * DA_HEADS)) * (DA_HEAD_DIM ** -0.5),
                      jnp.tile(da_k_norm, (1, 2 * DA_HEADS))], axis=1).reshape(nl, 2, 1, IN_TN)
    blk = np.arange(IN_TN) // DA_HEAD_DIM
    bd = jnp.asarray(blk[:, None] == blk[None, :], bf16)
    wuq = _pad_lanes(mla_w_uq.reshape(nl, MLA_Q_RANK, MLA_HEADS, MLA_QK), LANES)
    wuq = wuq.reshape(nl, MLA_Q_RANK, MLA_HEADS * LANES).astype(bf16)
    wukv = mla_w_ukv.reshape(nl, MLA_KV_RANK, MLA_HEADS, MLA_NOPE + MLA_V)
    wuk = _pad_lanes(wukv[..., :MLA_NOPE], LANES).reshape(nl, MLA_KV_RANK, MLA_HEADS * LANES).astype(bf16)
    wuv = wukv[..., MLA_NOPE:].reshape(nl, MLA_KV_RANK, MLA_HEADS * MLA_V).astype(bf16)
    qn = (_pad_lanes(mla_q_norm, LANES) * (MLA_QK ** -0.5)).reshape(nl, 1, LANES)
    kn = _pad_lanes(mla_k_norm, LANES).reshape(nl, 1, LANES)
    dt_bias = _pad_lanes(ssm_dt_bias, LANES).reshape(nl, 1, LANES)
    a_log = _pad_lanes(ssm_a_log, LANES).reshape(nl, 1, LANES)
    dvec = jnp.repeat(ssm_d, SSM_HEAD_DIM, axis=-1).reshape(nl, 1, SSM_D_INNER)
    v3 = lambda a: a.reshape(nl, 1, a.shape[-1])
    wa, wb, wc, wo = (w.astype(bf16) for w in (w_branch_a, w_branch_b, w_branch_c, w_out))
    wup, wdown = ffn_w_up.astype(bf16), ffn_w_down.astype(bf16)

    mod = _ada_mod(c, ada_w, ada_b).reshape(nl, bsz, 6, d)
    ctab, sptab, smtab = _rope_tables(positions)
    bias = _bias_tiles(rel_bias, tq)

    x2 = x.reshape(t, d)
    for l in range(nl):
        lam_init = 0.8 - 0.6 * math.exp(-0.3 * l)
        qkv, rest = _inproj(x2, v3(norm_mix), mod, w_in_p, qk_w, bd, l, seq)
        q_b, k_b, v_b = _mla_prep(rest, v3(mla_q_a_norm), v3(mla_kv_a_norm), wuq, wuk, wuv, qn, kn,
                                  ctab, sptab, smtab, l)
        ya = _diff_attn(qkv.reshape(bsz, seq, QKV_COLS), bias, da_lambda, v3(da_subln), l, lam_init)
        yb = _mla_attn(q_b.reshape(bsz, seq, -1), k_b.reshape(bsz, seq, -1), v_b.reshape(bsz, seq, -1), tq)
        ys = _ssd(rest.reshape(bsz, seq, REST_COLS), ssm_conv_w, v3(ssm_conv_b), dt_bias, a_log, dvec,
                  v3(ssm_norm), l)
        x2 = _merge(ya.reshape(t, -1), yb.reshape(t, -1), ys.reshape(t, -1), rest, x2, mod, wa, wb, wc, wo, l, seq)
        x2 = _ffn(x2.reshape(bsz, seq, d), v3(norm_ffn), mod, wup, ffn_conv_w, v3(ffn_conv_b), wdown, l).reshape(t, d)
    return x2.reshape(bsz, seq, d)
```

```python
import functools
import math

import jax
import jax.numpy as jnp
import numpy as np
from jax import lax
from jax.experimental import pallas as pl
from jax.experimental.pallas import tpu as pltpu

f32 = jnp.float32
bf16 = jnp.bfloat16

D_MODEL = 1024
DA_HEADS = 4
DA_HEAD_DIM = 64
MLA_HEADS = 8
MLA_Q_RANK = 256
MLA_KV_RANK = 128
MLA_NOPE = 64
MLA_ROPE = 32
MLA_V = 64
MLA_QK = MLA_NOPE + MLA_ROPE
SSM_D_INNER = 1024
SSM_HEAD_DIM = 64
SSM_HEADS = 16
SSM_GROUPS = 4
SSM_STATE = 128
SSM_CONV = 4
SSM_CHUNK = 128
SSM_CONV_DIM = SSM_D_INNER + 2 * SSM_GROUPS * SSM_STATE
FFN_DIM = 2816
FFN_CONV = 3
REL_BUCKETS = 32
REL_MAX_DIST = 128
ROPE_THETA = 10000.0
EPS = 1e-6

LANES = 128
SUBLANES = 8
VMEM_LIMIT = 56 * 1024 * 1024

IN_TN = 512
QKV_COLS = 3 * 512
R_XBC = 0
R_Z = R_XBC + SSM_CONV_DIM
R_GATES = R_Z + SSM_D_INNER
R_LAT = R_GATES + 3 * D_MODEL
REST_COLS = R_LAT + 512
MISC_KR_LANE = 64
NEG_BIG = -1e30

ATT_TQ = 256


def _cparams(sem):
    return pltpu.CompilerParams(dimension_semantics=sem, vmem_limit_bytes=VMEM_LIMIT)


def _lane_iota(shape):
    return lax.broadcasted_iota(jnp.int32, shape, len(shape) - 1)


def _dot(a, b):
    return jnp.dot(a, b, preferred_element_type=f32)


def _dot_nt(a, b):
    return lax.dot_general(a, b, (((1,), (1,)), ((), ())), preferred_element_type=f32)


def _sigmoid(x):
    return 1.0 / (1.0 + jnp.exp(-x))


def _silu(x):
    return x * _sigmoid(x)


def _ada_kernel(c_ref, w_ref, b_ref, o_ref):
    ca = _silu(c_ref[...]).astype(bf16)
    o_ref[...] = _dot(ca, w_ref[...].astype(bf16)) + b_ref[...]


def _ada_mod(c, ada_w, ada_b):
    nl, d, n = ada_w.shape
    bsz = c.shape[0]
    tn = 1024
    return pl.pallas_call(
        _ada_kernel,
        grid=(nl, n // tn),
        in_specs=[
            pl.BlockSpec((bsz, d), lambda l, j: (0, 0)),
            pl.BlockSpec((None, d, tn), lambda l, j: (l, 0, j)),
            pl.BlockSpec((None, 1, tn), lambda l, j: (l, 0, j)),
        ],
        out_specs=pl.BlockSpec((None, bsz, tn), lambda l, j: (l, 0, j)),
        out_shape=jax.ShapeDtypeStruct((nl, bsz, n), f32),
        compiler_params=_cparams(("arbitrary", "arbitrary")),
        name="ada_mod",
    )(c, ada_w, ada_b.reshape(nl, 1, n))


def _rope_kernel(pos_ref, freq_ref, c_ref, sp_ref, sm_ref):
    ang = pos_ref[...].astype(f32) * freq_ref[...]
    cos, sin = jnp.cos(ang), jnp.sin(ang)
    lane = _lane_iota(ang.shape)
    lo, mid, hi = MISC_KR_LANE, MISC_KR_LANE + MLA_ROPE // 2, MISC_KR_LANE + MLA_ROPE
    c_ref[...] = jnp.where((lane >= lo) & (lane < hi), cos, 1.0)
    sp_ref[...] = jnp.where((lane >= mid) & (lane < hi), sin, 0.0)
    sm_ref[...] = jnp.where((lane >= lo) & (lane < mid), -sin, 0.0)


def _rope_tables(positions):
    t = positions.size
    tm = min(t, 2048)
    inv_freq = ROPE_THETA ** (-jnp.arange(0, MLA_ROPE, 2, dtype=f32) / MLA_ROPE)
    freq = jnp.zeros((LANES,), f32).at[MISC_KR_LANE:MISC_KR_LANE + MLA_ROPE].set(jnp.tile(inv_freq, 2))
    out = jax.ShapeDtypeStruct((t, LANES), f32)
    spec = pl.BlockSpec((tm, LANES), lambda i: (i, 0))
    return pl.pallas_call(
        _rope_kernel,
        grid=(t // tm,),
        in_specs=[pl.BlockSpec((tm, 1), lambda i: (i, 0)), pl.BlockSpec((1, LANES), lambda i: (0, 0))],
        out_specs=[spec, spec, spec],
        out_shape=[out, out, out],
        compiler_params=_cparams(("arbitrary",)),
        name="rope_tables",
    )(positions.reshape(t, 1), freq.reshape(1, LANES))


def _bias_kernel(rb_ref, o_ref):
    h, which = pl.program_id(0), pl.program_id(1)
    tq = o_ref.shape[-1]
    row = lax.broadcasted_iota(jnp.int32, (tq, tq), 0)
    col = lax.broadcasted_iota(jnp.int32, (tq, tq), 1)
    d = jnp.maximum(col - row + which * tq, 0)
    max_exact = REL_BUCKETS // 2
    large = max_exact + (jnp.log(jnp.maximum(d, 1).astype(f32) / max_exact)
                         / math.log(REL_MAX_DIST / max_exact) * (REL_BUCKETS - max_exact)).astype(jnp.int32)
    large = jnp.minimum(large, REL_BUCKETS - 1)
    bucket = jnp.where(d < max_exact, d, large)
    val = jnp.full((tq, tq), rb_ref[0, h], f32)
    for k in range(1, REL_BUCKETS):
        val = jnp.where(bucket == k, rb_ref[k, h], val)
    o_ref[...] = val - rb_ref[REL_BUCKETS - 1, h]


def _bias_tiles(rel_bias, tq):
    return pl.pallas_call(
        _bias_kernel,
        grid=(DA_HEADS, 2),
        in_specs=[pl.BlockSpec(memory_space=pltpu.SMEM)],
        out_specs=pl.BlockSpec((None, None, tq, tq), lambda h, w: (h, w, 0, 0)),
        out_shape=jax.ShapeDtypeStruct((DA_HEADS, 2, tq, tq), f32),
        compiler_params=_cparams(("arbitrary", "arbitrary")),
        name="bias_tiles",
    )(rel_bias)


def _inproj_kernel(x_ref, nw_ref, mod_ref, w_ref, qkw_ref, bd_ref, qkv_ref, rest_ref, h_ref):
    j = pl.program_id(1)

    @pl.when(j == 0)
    def _():
        x = x_ref[...]
        y = x * lax.rsqrt(jnp.mean(x * x, axis=-1, keepdims=True) + EPS) * nw_ref[...]
        h_ref[...] = (y * (1.0 + mod_ref[1:2, :]) + mod_ref[0:1, :]).astype(bf16)

    acc = _dot(h_ref[...], w_ref[...])

    @pl.when(j < 2)
    def _():
        ss = _dot((acc * acc).astype(bf16), bd_ref[...]) * (1.0 / DA_HEAD_DIM)
        qkv_ref[...] = (acc * lax.rsqrt(ss + EPS) * qkw_ref[...]).astype(bf16)

    @pl.when(j == 2)
    def _():
        qkv_ref[...] = acc.astype(bf16)

    g_lo = 3 + R_GATES // IN_TN
    g_hi = 3 + R_LAT // IN_TN

    @pl.when((j >= 3) & ((j < g_lo) | (j >= g_hi)))
    def _():
        rest_ref[...] = acc

    @pl.when((j >= g_lo) & (j < g_hi))
    def _():
        rest_ref[...] = _sigmoid(acc)


def _inproj(x2, norm_w, mod, w_p, qk_w, bd, l, rows_per_batch):
    t, d = x2.shape
    tm = min(1024, rows_per_batch)
    tpb = rows_per_batch // tm
    nj = (QKV_COLS + REST_COLS) // IN_TN
    return pl.pallas_call(
        _inproj_kernel,
        grid=(t // tm, nj),
        in_specs=[
            pl.BlockSpec((tm, d), lambda i, j: (i, 0)),
            pl.BlockSpec((None, 1, d), lambda i, j: (l, 0, 0)),
            pl.BlockSpec((None, None, 6, d), lambda i, j: (l, i // tpb, 0, 0)),
            pl.BlockSpec((None, d, IN_TN), lambda i, j: (l, 0, j)),
            pl.BlockSpec((None, None, 1, IN_TN), lambda i, j: (l, jnp.minimum(j, 1), 0, 0)),
            pl.BlockSpec((IN_TN, IN_TN), lambda i, j: (0, 0)),
        ],
        out_specs=[
            pl.BlockSpec((tm, IN_TN), lambda i, j: (i, jnp.minimum(j, 2))),
            pl.BlockSpec((tm, IN_TN), lambda i, j: (i, jnp.maximum(j - 3, 0))),
        ],
        out_shape=[jax.ShapeDtypeStruct((t, QKV_COLS), bf16), jax.ShapeDtypeStruct((t, REST_COLS), f32)],
        scratch_shapes=[pltpu.VMEM((tm, d), bf16)],
        compiler_params=_cparams(("arbitrary", "arbitrary")),
        name="inproj",
    )(x2, norm_w, mod, w_p, qk_w, bd)


def _rope_apply(x, c, sp, sm):
    return x * c + pltpu.roll(x, 16, 1) * sp + pltpu.roll(x, LANES - 16, 1) * sm


def _mla_prep_kernel(lat_ref, qan_ref, kvan_ref, wuq_ref, wuk_ref, wuv_ref, qn_ref, kn_ref,
                     c_ref, sp_ref, sm_ref, q_ref, k_ref, v_ref):
    lat = lat_ref[...]
    c, sp, sm = c_ref[...], sp_ref[...], sm_ref[...]
    cq = lat[:, :MLA_Q_RANK]
    cqn = (cq * lax.rsqrt(jnp.mean(cq * cq, axis=-1, keepdims=True) + EPS) * qan_ref[...]).astype(bf16)
    ckv = lat[:, MLA_Q_RANK:MLA_Q_RANK + MLA_KV_RANK]
    ckvn = (ckv * lax.rsqrt(jnp.mean(ckv * ckv, axis=-1, keepdims=True) + EPS) * kvan_ref[...]).astype(bf16)
    misc = lat[:, MLA_Q_RANK + MLA_KV_RANK:]
    lane = _lane_iota(misc.shape)
    krb = jnp.where((lane >= MISC_KR_LANE) & (lane < MISC_KR_LANE + MLA_ROPE), misc, 0.0)
    qf = _dot(cqn, wuq_ref[...])
    kf = _dot(ckvn, wuk_ref[...])
    v_ref[...] = _dot(ckvn, wuv_ref[...]).astype(bf16)
    inv = 1.0 / MLA_QK
    for h in range(MLA_HEADS):
        sl = slice(h * LANES, (h + 1) * LANES)
        qh = qf[:, sl]
        qh = qh * lax.rsqrt(jnp.sum(qh * qh, axis=-1, keepdims=True) * inv + EPS) * qn_ref[...]
        q_ref[:, sl] = _rope_apply(qh, c, sp, sm).astype(bf16)
        kh = kf[:, sl] + krb
        kh = kh * lax.rsqrt(jnp.sum(kh * kh, axis=-1, keepdims=True) * inv + EPS) * kn_ref[...]
        k_ref[:, sl] = _rope_apply(kh, c, sp, sm).astype(bf16)


def _mla_prep(rest, qan, kvan, wuq, wuk, wuv, qn, kn, ctab, sptab, smtab, l):
    t = rest.shape[0]
    tm = min(512, t)
    lat_blk = R_LAT // 512
    hq = MLA_HEADS * LANES
    vec = lambda n: pl.BlockSpec((None, 1, n), lambda i: (l, 0, 0))
    tab = pl.BlockSpec((tm, LANES), lambda i: (i, 0))
    return pl.pallas_call(
        _mla_prep_kernel,
        grid=(t // tm,),
        in_specs=[
            pl.BlockSpec((tm, 512), lambda i: (i, lat_blk)),
            vec(MLA_Q_RANK), vec(MLA_KV_RANK),
            pl.BlockSpec((None, MLA_Q_RANK, hq), lambda i: (l, 0, 0)),
            pl.BlockSpec((None, MLA_KV_RANK, hq), lambda i: (l, 0, 0)),
            pl.BlockSpec((None, MLA_KV_RANK, MLA_HEADS * MLA_V), lambda i: (l, 0, 0)),
            vec(LANES), vec(LANES), tab, tab, tab,
        ],
        out_specs=[
            pl.BlockSpec((tm, hq), lambda i: (i, 0)),
            pl.BlockSpec((tm, hq), lambda i: (i, 0)),
            pl.BlockSpec((tm, MLA_HEADS * MLA_V), lambda i: (i, 0)),
        ],
        out_shape=[jax.ShapeDtypeStruct((t, hq), bf16), jax.ShapeDtypeStruct((t, hq), bf16),
                   jax.ShapeDtypeStruct((t, MLA_HEADS * MLA_V), bf16)],
        compiler_params=_cparams(("arbitrary",)),
        name="mla_prep",
    )(rest, qan, kvan, wuq, wuk, wuv, qn, kn, ctab, sptab, smtab)


def _flash_t(q_list, k_ref, kcols, vt_ref, vrows, m_ref, l_ref, acc_ref, bias_fn, qi, tq):
    nmap = len(q_list)
    m_ref[...] = jnp.full(m_ref.shape, -jnp.inf, f32)
    l_ref[...] = jnp.zeros(l_ref.shape, f32)
    acc_ref[...] = jnp.zeros(acc_ref.shape, f32)

    def step(kj, which, masked):
        start = pl.multiple_of(kj * tq, tq)
        if masked:
            krow = lax.broadcasted_iota(jnp.int32, (tq, tq), 0)
            qcol = lax.broadcasted_iota(jnp.int32, (tq, tq), 1)
            keep = krow <= qcol
        s_list = []
        for a in range(nmap):
            k = k_ref[pl.ds(start, tq), kcols[a]:kcols[a] + LANES]
            s_list.append(_dot_nt(k, q_list[a]))
        e_list, alpha_list = [], []
        for a in range(nmap):
            s = s_list[a]
            if which is not None and bias_fn is not None:
                s = s + bias_fn(a, which)
            if masked:
                s = jnp.where(keep, s, NEG_BIG)
            m_prev = m_ref[a]
            m_new = jnp.maximum(m_prev, jnp.max(s, axis=0, keepdims=True))
            alpha = jnp.exp(m_prev - m_new)
            e = jnp.exp(s - m_new)
            l_ref[a] = alpha * l_ref[a] + jnp.sum(e, axis=0, keepdims=True)
            m_ref[a] = m_new
            e_list.append(e.astype(bf16))
            alpha_list.append(alpha)
        for a in range(nmap):
            vt = vt_ref[kj, vrows[a]:vrows[a] + LANES, :]
            acc_ref[a] = alpha_list[a] * acc_ref[a] + _dot(vt, e_list[a])

    def body(kj, carry):
        step(kj, None, False)
        return carry

    if bias_fn is None:
        lax.fori_loop(0, qi, body, 0)
    else:
        lax.fori_loop(0, qi - 1, body, 0)

        @pl.when(qi >= 1)
        def _():
            step(qi - 1, 1, False)

    step(qi, 0, True)


def _fill_vt(v_ref, vt_ref, tq):
    for c in range(vt_ref.shape[0]):
        vt_ref[c] = v_ref[c * tq:(c + 1) * tq, :].astype(f32).T.astype(bf16)


def _diff_attn_kernel(q_ref, k_ref, v_ref, bias_ref, lp_ref, sub_ref, o_ref, vt_ref, m_ref, l_ref, acc_ref,
                      *, lam_init, tq):
    qi = pl.program_id(1)

    @pl.when(qi == 0)
    def _():
        _fill_vt(v_ref, vt_ref, tq)

    lane = _lane_iota((tq, LANES))
    q_list, kcols, vrows = [], [], []
    for h in range(DA_HEADS):
        q = q_ref[:, h * LANES:(h + 1) * LANES]
        zero = jnp.zeros_like(q)
        q_list += [jnp.where(lane < DA_HEAD_DIM, q, zero), jnp.where(lane >= DA_HEAD_DIM, q, zero)]
        kcols += [h * LANES, h * LANES]
        vrows += [h * LANES, h * LANES]
    _flash_t(q_list, k_ref, kcols, vt_ref, vrows, m_ref, l_ref, acc_ref,
             lambda a, which: bias_ref[a // 2, which], qi, tq)
    lp = lp_ref[...]
    lam = (jnp.exp(jnp.sum(lp[0:1] * lp[1:2], axis=-1, keepdims=True))
           - jnp.exp(jnp.sum(lp[2:3] * lp[3:4], axis=-1, keepdims=True)) + lam_init)
    for h in range(DA_HEADS):
        o_t = acc_ref[2 * h] / l_ref[2 * h] - lam * (acc_ref[2 * h + 1] / l_ref[2 * h + 1])
        o = o_t.T
        o = o * lax.rsqrt(jnp.mean(o * o, axis=-1, keepdims=True) + EPS) * sub_ref[...]
        o_ref[:, h * LANES:(h + 1) * LANES] = (o * (1.0 - lam_init)).astype(bf16)


def _diff_attn(qkv3, bias, da_lambda, da_subln, l, lam_init):
    bsz, s, _ = qkv3.shape
    tq = bias.shape[-1]
    nq = s // tq
    hw = DA_HEADS * LANES
    kern = functools.partial(_diff_attn_kernel, lam_init=lam_init, tq=tq)
    return pl.pallas_call(
        kern,
        grid=(bsz, nq),
        in_specs=[
            pl.BlockSpec((None, tq, hw), lambda b, i: (b, i, 0)),
            pl.BlockSpec((None, s, hw), lambda b, i: (b, 0, 1)),
            pl.BlockSpec((None, s, hw), lambda b, i: (b, 0, 2)),
            pl.BlockSpec((DA_HEADS, 2, tq, tq), lambda b, i: (0, 0, 0, 0)),
            pl.BlockSpec((None, 4, DA_HEAD_DIM), lambda b, i: (l, 0, 0)),
            pl.BlockSpec((None, 1, LANES), lambda b, i: (l, 0, 0)),
        ],
        out_specs=pl.BlockSpec((None, tq, hw), lambda b, i: (b, i, 0)),
        out_shape=jax.ShapeDtypeStruct((bsz, s, hw), bf16),
        scratch_shapes=[pltpu.VMEM((nq, hw, tq), bf16),
                        pltpu.VMEM((2 * DA_HEADS, 1, tq), f32), pltpu.VMEM((2 * DA_HEADS, 1, tq), f32),
                        pltpu.VMEM((2 * DA_HEADS, LANES, tq), f32)],
        compiler_params=_cparams(("arbitrary", "arbitrary")),
        name="diff_attn",
    )(qkv3, qkv3, qkv3, bias, da_lambda, da_subln)


def _mla_attn_kernel(q_ref, k_ref, v_ref, o_ref, vt_ref, m_ref, l_ref, acc_ref, *, tq):
    qi = pl.program_id(1)

    @pl.when(qi == 0)
    def _():
        _fill_vt(v_ref, vt_ref, tq)

    q_list = [q_ref[:, h * LANES:(h + 1) * LANES] for h in range(MLA_HEADS)]
    kcols = [h * LANES for h in range(MLA_HEADS)]
    vrows = [(h // 2) * LANES for h in range(MLA_HEADS)]
    _flash_t(q_list, k_ref, kcols, vt_ref, vrows, m_ref, l_ref, acc_ref, None, qi, tq)
    row = lax.broadcasted_iota(jnp.int32, (LANES, tq), 0)
    for p in range(MLA_HEADS // 2):
        o_t = jnp.where(row < MLA_V, acc_ref[2 * p] / l_ref[2 * p], acc_ref[2 * p + 1] / l_ref[2 * p + 1])
        o_ref[:, p * LANES:(p + 1) * LANES] = o_t.T.astype(bf16)


def _mla_attn(q3, k3, v3, tq):
    bsz, s, hq = q3.shape
    nq = s // tq
    hv = v3.shape[-1]
    kern = functools.partial(_mla_attn_kernel, tq=tq)
    return pl.pallas_call(
        kern,
        grid=(bsz, nq),
        in_specs=[
            pl.BlockSpec((None, tq, hq), lambda b, i: (b, i, 0)),
            pl.BlockSpec((None, s, hq), lambda b, i: (b, 0, 0)),
            pl.BlockSpec((None, s, hv), lambda b, i: (b, 0, 0)),
        ],
        out_specs=pl.BlockSpec((None, tq, hv), lambda b, i: (b, i, 0)),
        out_shape=jax.ShapeDtypeStruct((bsz, s, hv), bf16),
        scratch_shapes=[pltpu.VMEM((nq, hv, tq), bf16),
                        pltpu.VMEM((MLA_HEADS, 1, tq), f32), pltpu.VMEM((MLA_HEADS, 1, tq), f32),
                        pltpu.VMEM((MLA_HEADS, LANES, tq), f32)],
        compiler_params=_cparams(("arbitrary", "arbitrary")),
        name="mla_attn",
    )(q3, k3, v3)


def _split3(x):
    hi = x.astype(bf16)
    r1 = x - hi.astype(f32)
    mid = r1.astype(bf16)
    lo = (r1 - mid.astype(f32)).astype(bf16)
    return hi, mid, lo


def _pair_bcast(x, j, lane):
    return jnp.where(lane < SSM_HEAD_DIM, x[:, 2 * j:2 * j + 1], x[:, 2 * j + 1:2 * j + 2])


def _ssd_kernel(xbc_ref, z_ref, misc_ref, cw_ref, cb_ref, dtb_ref, alog_ref, dvec_ref, nw_ref,
                o_ref, ext_ref, st_ref, y_ref):
    c = pl.program_id(1)
    L = SSM_CHUNK
    G, N = SSM_GROUPS, SSM_STATE

    @pl.when(c == 0)
    def _():
        ext_ref[0:SUBLANES, :] = jnp.zeros((SUBLANES, SSM_CONV_DIM), f32)
        st_ref[...] = jnp.zeros(st_ref.shape, f32)

    ext_ref[SUBLANES:, :] = xbc_ref[...]
    conv = cb_ref[...] + cw_ref[SSM_CONV - 1:SSM_CONV, :] * ext_ref[SUBLANES:, :]
    for s in range(1, SSM_CONV):
        conv = conv + cw_ref[SSM_CONV - 1 - s:SSM_CONV - s, :] * ext_ref[SUBLANES - s:SUBLANES - s + L, :]
    ext_ref[0:SUBLANES, :] = ext_ref[L:L + SUBLANES, :]
    xa = _silu(conv)
    xs = xa[:, :SSM_D_INNER]
    bm = xa[:, SSM_D_INNER:SSM_D_INNER + G * N]
    cm = xa[:, SSM_D_INNER + G * N:]

    lane = _lane_iota((L, LANES))
    lane1 = _lane_iota((1, LANES))
    dt = jax.nn.softplus(misc_ref[...] + dtb_ref[...])
    a_l = jnp.where(lane1 < SSM_HEADS, -jnp.exp(alog_ref[...]), 0.0)
    da = dt * a_l
    row = lax.broadcasted_iota(jnp.int32, (L, L), 0)
    col = lax.broadcasted_iota(jnp.int32, (L, L), 1)
    causal = col <= row
    tri = jnp.where(causal, 1.0, 0.0).astype(bf16)
    hi, mid, lo = _split3(da)
    cs = _dot(tri, hi) + _dot(tri, mid) + _dot(tri, lo)
    cs_t = cs.T
    cs_last = cs[L - 1:L, :]
    w_end = dt * jnp.exp(cs_last - cs)
    e_cs = jnp.exp(cs)
    cdec = jnp.exp(cs_last)

    for g in range(G):
        bm_g = bm[:, g * N:(g + 1) * N]
        cm_g = cm[:, g * N:(g + 1) * N].astype(bf16)
        cb = _dot_nt(cm_g, bm_g.astype(bf16))
        bm_t = bm_g.T.astype(bf16)
        for jj in range(2):
            j = 2 * g + jj
            sl = slice(j * LANES, (j + 1) * LANES)
            psl = slice(jj * LANES, (jj + 1) * LANES)
            xs_p = xs[:, sl]
            xdt = (xs_p * _pair_bcast(dt, j, lane)).astype(bf16)
            yd = []
            for hh in (2 * j, 2 * j + 1):
                seg = cs[:, hh:hh + 1] - cs_t[hh:hh + 1, :]
                decay = jnp.exp(jnp.where(causal, seg, -jnp.inf))
                yd.append(_dot((cb * decay).astype(bf16), xdt))
            y_diag = jnp.where(lane < SSM_HEAD_DIM, yd[0], yd[1])
            prev = st_ref[g, :, psl]
            y_off = _dot(cm_g, prev.astype(bf16)) * _pair_bcast(e_cs, j, lane)
            y_ref[:, sl] = y_diag + y_off + dvec_ref[:, sl] * xs_p
            xw = (xs_p * _pair_bcast(w_end, j, lane)).astype(bf16)
            cd = jnp.where(lane1 < SSM_HEAD_DIM, cdec[:, 2 * j:2 * j + 1], cdec[:, 2 * j + 1:2 * j + 2])
            st_ref[g, :, psl] = prev * cd + _dot(bm_t, xw)

    gw = SSM_D_INNER // G
    for g in range(G):
        sl = slice(g * gw, (g + 1) * gw)
        yz = y_ref[:, sl] * _silu(z_ref[:, sl])
        yz = yz * lax.rsqrt(jnp.mean(yz * yz, axis=-1, keepdims=True) + EPS) * nw_ref[:, sl]
        o_ref[:, sl] = yz.astype(bf16)


def _ssd(rest3, conv_w, conv_b, dt_bias, a_log, dvec, norm_w, l):
    bsz, s, _ = rest3.shape
    L = SSM_CHUNK
    nc = s // L
    misc_blk = (R_LAT + MLA_Q_RANK + MLA_KV_RANK) // LANES
    vec = lambda n: pl.BlockSpec((None, 1, n), lambda b, c: (l, 0, 0))
    return pl.pallas_call(
        _ssd_kernel,
        grid=(bsz, nc),
        in_specs=[
            pl.BlockSpec((None, L, SSM_CONV_DIM), lambda b, c: (b, c, R_XBC // SSM_CONV_DIM)),
            pl.BlockSpec((None, L, SSM_D_INNER), lambda b, c: (b, c, R_Z // SSM_D_INNER)),
            pl.BlockSpec((None, L, LANES), lambda b, c: (b, c, misc_blk)),
            pl.BlockSpec((None, SSM_CONV, SSM_CONV_DIM), lambda b, c: (l, 0, 0)),
            vec(SSM_CONV_DIM), vec(LANES), vec(LANES), vec(SSM_D_INNER), vec(SSM_D_INNER),
        ],
        out_specs=pl.BlockSpec((None, L, SSM_D_INNER), lambda b, c: (b, c, 0)),
        out_shape=jax.ShapeDtypeStruct((bsz, s, SSM_D_INNER), bf16),
        scratch_shapes=[
            pltpu.VMEM((L + SUBLANES, SSM_CONV_DIM), f32),
            pltpu.VMEM((SSM_GROUPS, SSM_STATE, 4 * SSM_HEAD_DIM), f32),
            pltpu.VMEM((L, SSM_D_INNER), f32),
        ],
        compiler_params=_cparams(("arbitrary", "arbitrary")),
        name="ssd",
    )(rest3, rest3, rest3, conv_w, conv_b, dt_bias, a_log, dvec, norm_w)


def _merge_kernel(ya_ref, yb_ref, ys_ref, ga_ref, gb_ref, gc_ref, x_ref, mod_ref,
                  wa_ref, wb_ref, wc_ref, wo_ref, o_ref):
    merged = (ga_ref[...] * _dot(ya_ref[...], wa_ref[...])
              + gb_ref[...] * _dot(yb_ref[...], wb_ref[...])
              + gc_ref[...] * _dot(ys_ref[...], wc_ref[...]))
    o_ref[...] = x_ref[...] + mod_ref[2:3, :] * _dot(merged.astype(bf16), wo_ref[...])


def _merge(ya, yb, ys, rest, x2, mod, wa, wb, wc, wo, l, rows_per_batch):
    t, d = x2.shape
    tm = min(512, rows_per_batch)
    tpb = rows_per_batch // tm
    gblk = R_GATES // d
    row = lambda n: pl.BlockSpec((tm, n), lambda i: (i, 0))
    wspec = lambda k: pl.BlockSpec((None, k, d), lambda i: (l, 0, 0))
    return pl.pallas_call(
        _merge_kernel,
        grid=(t // tm,),
        in_specs=[
            row(ya.shape[1]), row(yb.shape[1]), row(ys.shape[1]),
            pl.BlockSpec((tm, d), lambda i: (i, gblk)),
            pl.BlockSpec((tm, d), lambda i: (i, gblk + 1)),
            pl.BlockSpec((tm, d), lambda i: (i, gblk + 2)),
            row(d),
            pl.BlockSpec((None, None, 6, d), lambda i: (l, i // tpb, 0, 0)),
            wspec(ya.shape[1]), wspec(yb.shape[1]), wspec(ys.shape[1]), wspec(d),
        ],
        out_specs=row(d),
        out_shape=jax.ShapeDtypeStruct((t, d), f32),
        compiler_params=_cparams(("arbitrary",)),
        name="merge_out",
    )(ya, yb, ys, rest, rest, rest, x2, mod, wa, wb, wc, wo)


def _ffn_kernel(x_ref, halo_ref, nw_ref, mod_ref, wua_ref, wuv_ref, cwa_ref, cwv_ref, cba_ref, cbv_ref, wd_ref,
                o_ref, h_ref, acc_ref):
    mi, fi = pl.program_id(1), pl.program_id(2)
    tm = x_ref.shape[0]

    def norm_mod(x):
        y = x * lax.rsqrt(jnp.mean(x * x, axis=-1, keepdims=True) + EPS) * nw_ref[...]
        return (y * (1.0 + mod_ref[4:5, :]) + mod_ref[3:4, :]).astype(bf16)

    @pl.when(fi == 0)
    def _():
        keep = jnp.where(mi > 0, 1.0, 0.0)
        h_ref[0:SUBLANES, :] = norm_mod(halo_ref[...] * keep) * keep.astype(bf16)
        h_ref[SUBLANES:, :] = norm_mod(x_ref[...])
        acc_ref[...] = jnp.zeros(acc_ref.shape, f32)

    h = h_ref[...]

    def conv(u, cw_ref, cb_ref):
        out = cb_ref[...] + cw_ref[FFN_CONV - 1:FFN_CONV, :] * u[SUBLANES:, :]
        for s in range(1, FFN_CONV):
            out = out + cw_ref[FFN_CONV - 1 - s:FFN_CONV - s, :] * pltpu.roll(u, s, 0)[SUBLANES:, :]
        return out

    ua = conv(_dot(h, wua_ref[...]), cwa_ref, cba_ref)
    uv = conv(_dot(h, wuv_ref[...]), cwv_ref, cbv_ref)
    act = (_silu(ua) * uv).astype(bf16)
    acc_ref[...] += _dot(act, wd_ref[...])

    @pl.when(fi == pl.num_programs(2) - 1)
    def _():
        o_ref[...] = x_ref[...] + mod_ref[5:6, :] * acc_ref[...]


def _ffn(x3, norm_w, mod, wup, conv_w, conv_b, wdown, l):
    bsz, s, d = x3.shape
    tm = min(1024, s)
    tf = 256
    nf = FFN_DIM // tf
    hb = tm // SUBLANES
    return pl.pallas_call(
        _ffn_kernel,
        grid=(bsz, s // tm, nf),
        in_specs=[
            pl.BlockSpec((None, tm, d), lambda b, m, f: (b, m, 0)),
            pl.BlockSpec((None, SUBLANES, d), lambda b, m, f: (b, jnp.maximum(m * hb - 1, 0), 0)),
            pl.BlockSpec((None, 1, d), lambda b, m, f: (l, 0, 0)),
            pl.BlockSpec((None, None, 6, d), lambda b, m, f: (l, b, 0, 0)),
            pl.BlockSpec((None, d, tf), lambda b, m, f: (l, 0, f)),
            pl.BlockSpec((None, d, tf), lambda b, m, f: (l, 0, nf + f)),
            pl.BlockSpec((None, FFN_CONV, tf), lambda b, m, f: (l, 0, f)),
            pl.BlockSpec((None, FFN_CONV, tf), lambda b, m, f: (l, 0, nf + f)),
            pl.BlockSpec((None, 1, tf), lambda b, m, f: (l, 0, f)),
            pl.BlockSpec((None, 1, tf), lambda b, m, f: (l, 0, nf + f)),
            pl.BlockSpec((None, tf, d), lambda b, m, f: (l, f, 0)),
        ],
        out_specs=pl.BlockSpec((None, tm, d), lambda b, m, f: (b, m, 0)),
        out_shape=jax.ShapeDtypeStruct((bsz, s, d), f32),
        scratch_shapes=[pltpu.VMEM((tm + SUBLANES, d), bf16), pltpu.VMEM((tm, d), f32)],
        compiler_params=_cparams(("arbitrary", "arbitrary", "arbitrary")),
        name="ffn",
    )(x3, x3, norm_w, mod, wup, wup, conv_w, conv_w, conv_b, conv_b, wdown)


def _pack_w_in(w_in):
    nl, d, _ = w_in.shape
    sizes = (512, 512, 512, MLA_Q_RANK, MLA_KV_RANK, MLA_ROPE, SSM_D_INNER, SSM_CONV_DIM, SSM_HEADS, 3 * D_MODEL)
    offs = np.concatenate([[0], np.cumsum(sizes)])
    qa, ka, va, cq, ckv, kr, z, xbc, dt, gates = [w_in[..., int(offs[i]):int(offs[i + 1])] for i in range(10)]
    zeros = lambda n: jnp.zeros((nl, d, n), w_in.dtype)
    misc = jnp.concatenate([dt, zeros(MISC_KR_LANE - SSM_HEADS), kr, zeros(LANES - MISC_KR_LANE - MLA_ROPE)], axis=-1)
    return jnp.concatenate([qa, ka, va, xbc, z, gates, cq, ckv, misc], axis=-1).astype(bf16)


def _pad_lanes(v, n):
    return jnp.pad(v, [(0, 0)] * (v.ndim - 1) + [(0, n - v.shape[-1])])


def kernel(x, c, positions, rel_bias, ada_w, ada_b, norm_mix, norm_ffn, w_in, da_q_norm, da_k_norm, da_lambda, da_subln, mla_q_a_norm, mla_kv_a_norm, mla_w_uq, mla_w_ukv, mla_q_norm, mla_k_norm, ssm_conv_w, ssm_conv_b, ssm_dt_bias, ssm_a_log, ssm_d, ssm_norm, w_branch_a, w_branch_b, w_branch_c, w_out, ffn_w_up, ffn_conv_w, ffn_conv_b, ffn_w_down):
    bsz, seq, d = x.shape
    nl = w_in.shape[0]
    t = bsz * seq
    tq = min(ATT_TQ, seq)

    w_in_p = _pack_w_in(w_in)
    qk_w = jnp.stack([jnp.tile(da_q_norm, (1, 2 * DA_HEADS)) * (DA_HEAD_DIM ** -0.5),
                      jnp.tile(da_k_norm, (1, 2 * DA_HEADS))], axis=1).reshape(nl, 2, 1, IN_TN)
    blk = np.arange(IN_TN) // DA_HEAD_DIM
    bd = jnp.asarray(blk[:, None] == blk[None, :], bf16)
    wuq = _pad_lanes(mla_w_uq.reshape(nl, MLA_Q_RANK, MLA_HEADS, MLA_QK), LANES)
    wuq = wuq.reshape(nl, MLA_Q_RANK, MLA_HEADS * LANES).astype(bf16)
    wukv = mla_w_ukv.reshape(nl, MLA_KV_RANK, MLA_HEADS, MLA_NOPE + MLA_V)
    wuk = _pad_lanes(wukv[..., :MLA_NOPE], LANES).reshape(nl, MLA_KV_RANK, MLA_HEADS * LANES).astype(bf16)
    wuv = wukv[..., MLA_NOPE:].reshape(nl, MLA_KV_RANK, MLA_HEADS * MLA_V).astype(bf16)
    qn = (_pad_lanes(mla_q_norm, LANES) * (MLA_QK ** -0.5)).reshape(nl, 1, LANES)
    kn = _pad_lanes(mla_k_norm, LANES).reshape(nl, 1, LANES)
    dt_bias = _pad_lanes(ssm_dt_bias, LANES).reshape(nl, 1, LANES)
    a_log = _pad_lanes(ssm_a_log, LANES).reshape(nl, 1, LANES)
    dvec = jnp.repeat(ssm_d, SSM_HEAD_DIM, axis=-1).reshape(nl, 1, SSM_D_INNER)
    v3 = lambda a: a.reshape(nl, 1, a.shape[-1])
    wa, wb, wc, wo = (w.astype(bf16) for w in (w_branch_a, w_branch_b, w_branch_c, w_out))
    wup, wdown = ffn_w_up.astype(bf16), ffn_w_down.astype(bf16)

    mod = _ada_mod(c, ada_w, ada_b).reshape(nl, bsz, 6, d)
    ctab, sptab, smtab = _rope_tables(positions)
    bias = _bias_tiles(rel_bias, tq)

    x2 = x.reshape(t, d)
    for l in range(nl):
        lam_init = 0.8 - 0.6 * math.exp(-0.3 * l)
        qkv, rest = _inproj(x2, v3(norm_mix), mod, w_in_p, qk_w, bd, l, seq)
        q_b, k_b, v_b = _mla_prep(rest, v3(mla_q_a_norm), v3(mla_kv_a_norm), wuq, wuk, wuv, qn, kn,
                                  ctab, sptab, smtab, l)
        ya = _diff_attn(qkv.reshape(bsz, seq, QKV_COLS), bias, da_lambda, v3(da_subln), l, lam_init)
        yb = _mla_attn(q_b.reshape(bsz, seq, -1), k_b.reshape(bsz, seq, -1), v_b.reshape(bsz, seq, -1), tq)
        ys = _ssd(rest.reshape(bsz, seq, REST_COLS), ssm_conv_w, v3(ssm_conv_b), dt_bias, a_log, dvec,
                  v3(ssm_norm), l)
        x2 = _merge(ya.reshape(t, -1), yb.reshape(t, -1), ys.reshape(t, -1), rest, x2, mod, wa, wb, wc, wo, l, seq)
        x2 = _ffn(x2.reshape(bsz, seq, d), v3(norm_ffn), mod, wup, ffn_conv_w, v3(ffn_conv_b), wdown, l).reshape(t, d)
    return x2.reshape(bsz, seq, d)
```

```python
import functools
import math

import jax
import jax.numpy as jnp
import numpy as np
from jax import lax
from jax.experimental import pallas as pl
from jax.experimental.pallas import tpu as pltpu

f32 = jnp.float32
bf16 = jnp.bfloat16

D_MODEL = 1024
DA_HEADS = 4
DA_HEAD_DIM = 64
MLA_HEADS = 8
MLA_Q_RANK = 256
MLA_KV_RANK = 128
MLA_NOPE = 64
MLA_ROPE = 32
MLA_V = 64
MLA_QK = MLA_NOPE + MLA_ROPE
SSM_D_INNER = 1024
SSM_HEAD_DIM = 64
SSM_HEADS = 16
SSM_GROUPS = 4
SSM_STATE = 128
SSM_CONV = 4
SSM_CHUNK = 128
SSM_CONV_DIM = SSM_D_INNER + 2 * SSM_GROUPS * SSM_STATE
FFN_DIM = 2816
FFN_CONV = 3
REL_BUCKETS = 32
REL_MAX_DIST = 128
ROPE_THETA = 10000.0
EPS = 1e-6

LANES = 128
SUBLANES = 8
VMEM_LIMIT = 56 * 1024 * 1024

IN_TN = 512
QKV_COLS = 3 * 512
R_XBC = 0
R_Z = R_XBC + SSM_CONV_DIM
R_GATES = R_Z + SSM_D_INNER
R_LAT = R_GATES + 3 * D_MODEL
REST_COLS = R_LAT + 512
MISC_KR_LANE = 64
NEG_BIG = -1e30
LOG2E = math.log2(math.e)

ATT_TQ = 256
ATT_SKEW = 8


def _cparams(sem):
    return pltpu.CompilerParams(dimension_semantics=sem, vmem_limit_bytes=VMEM_LIMIT)


def _lane_iota(shape):
    return lax.broadcasted_iota(jnp.int32, shape, len(shape) - 1)


def _dot(a, b):
    return jnp.dot(a, b, preferred_element_type=f32)


def _dot_nt(a, b):
    return lax.dot_general(a, b, (((1,), (1,)), ((), ())), preferred_element_type=f32)


def _sigmoid(x):
    return 1.0 / (1.0 + jnp.exp(-x))


def _silu(x):
    return x * _sigmoid(x)


def _ada_kernel(c_ref, w_ref, b_ref, o_ref):
    ca = _silu(c_ref[...]).astype(bf16)
    o_ref[...] = _dot(ca, w_ref[...].astype(bf16)) + b_ref[...]


def _ada_mod(c, ada_w, ada_b):
    nl, d, n = ada_w.shape
    bsz = c.shape[0]
    tn = 1024
    return pl.pallas_call(
        _ada_kernel,
        grid=(nl, n // tn),
        in_specs=[
            pl.BlockSpec((bsz, d), lambda l, j: (0, 0)),
            pl.BlockSpec((None, d, tn), lambda l, j: (l, 0, j)),
            pl.BlockSpec((None, 1, tn), lambda l, j: (l, 0, j)),
        ],
        out_specs=pl.BlockSpec((None, bsz, tn), lambda l, j: (l, 0, j)),
        out_shape=jax.ShapeDtypeStruct((nl, bsz, n), f32),
        compiler_params=_cparams(("arbitrary", "arbitrary")),
        name="ada_mod",
    )(c, ada_w, ada_b.reshape(nl, 1, n))


def _rope_kernel(pos_ref, freq_ref, c_ref, sp_ref, sm_ref):
    ang = pos_ref[...].astype(f32) * freq_ref[...]
    cos, sin = jnp.cos(ang), jnp.sin(ang)
    lane = _lane_iota(ang.shape)
    lo, mid, hi = MISC_KR_LANE, MISC_KR_LANE + MLA_ROPE // 2, MISC_KR_LANE + MLA_ROPE
    c_ref[...] = jnp.where((lane >= lo) & (lane < hi), cos, 1.0)
    sp_ref[...] = jnp.where((lane >= mid) & (lane < hi), sin, 0.0)
    sm_ref[...] = jnp.where((lane >= lo) & (lane < mid), -sin, 0.0)


def _rope_tables(positions):
    t = positions.size
    tm = min(t, 2048)
    inv_freq = ROPE_THETA ** (-jnp.arange(0, MLA_ROPE, 2, dtype=f32) / MLA_ROPE)
    freq = jnp.zeros((LANES,), f32).at[MISC_KR_LANE:MISC_KR_LANE + MLA_ROPE].set(jnp.tile(inv_freq, 2))
    out = jax.ShapeDtypeStruct((t, LANES), f32)
    spec = pl.BlockSpec((tm, LANES), lambda i: (i, 0))
    return pl.pallas_call(
        _rope_kernel,
        grid=(t // tm,),
        in_specs=[pl.BlockSpec((tm, 1), lambda i: (i, 0)), pl.BlockSpec((1, LANES), lambda i: (0, 0))],
        out_specs=[spec, spec, spec],
        out_shape=[out, out, out],
        compiler_params=_cparams(("arbitrary",)),
        name="rope_tables",
    )(positions.reshape(t, 1), freq.reshape(1, LANES))


def _bias_kernel(rb_ref, o_ref):
    h, which = pl.program_id(0), pl.program_id(1)
    tq = o_ref.shape[-1]
    row = lax.broadcasted_iota(jnp.int32, (tq, tq), 0)
    col = lax.broadcasted_iota(jnp.int32, (tq, tq), 1)
    d = jnp.maximum(col - row + which * tq, 0)
    max_exact = REL_BUCKETS // 2
    large = max_exact + (jnp.log(jnp.maximum(d, 1).astype(f32) / max_exact)
                         / math.log(REL_MAX_DIST / max_exact) * (REL_BUCKETS - max_exact)).astype(jnp.int32)
    large = jnp.minimum(large, REL_BUCKETS - 1)
    bucket = jnp.where(d < max_exact, d, large)
    val = jnp.full((tq, tq), rb_ref[0, h], f32)
    for k in range(1, REL_BUCKETS):
        val = jnp.where(bucket == k, rb_ref[k, h], val)
    o_ref[...] = (val - rb_ref[REL_BUCKETS - 1, h]) * LOG2E


def _bias_tiles(rel_bias, tq):
    return pl.pallas_call(
        _bias_kernel,
        grid=(DA_HEADS, 2),
        in_specs=[pl.BlockSpec(memory_space=pltpu.SMEM)],
        out_specs=pl.BlockSpec((None, None, tq, tq), lambda h, w: (h, w, 0, 0)),
        out_shape=jax.ShapeDtypeStruct((DA_HEADS, 2, tq, tq), f32),
        compiler_params=_cparams(("arbitrary", "arbitrary")),
        name="bias_tiles",
    )(rel_bias)


def _inproj_kernel(x_ref, nw_ref, mod_ref, w_ref, qkw_ref, bd_ref, cw_ref, cb_ref, qkv_ref, rest_ref, h_ref):
    j = pl.program_id(1)

    @pl.when(j == 0)
    def _():
        x = x_ref[...]
        y = x * lax.rsqrt(jnp.mean(x * x, axis=-1, keepdims=True) + EPS) * nw_ref[...]
        h_ref[...] = (y * (1.0 + mod_ref[1:2, :]) + mod_ref[0:1, :]).astype(bf16)

    def proj():
        return _dot(h_ref[...], w_ref[...])

    @pl.when(j < 2)
    def _():
        acc = proj()
        ss = _dot((acc * acc).astype(bf16), bd_ref[...]) * (1.0 / DA_HEAD_DIM)
        qkv_ref[...] = (acc * lax.rsqrt(ss + EPS) * qkw_ref[...]).astype(bf16)

    @pl.when(j == 2)
    def _():
        qkv_ref[...] = proj().astype(bf16)

    x_lo, z_lo, g_lo, g_hi = (3 + off // IN_TN for off in (R_XBC, R_Z, R_GATES, R_LAT))

    @pl.when((j >= x_lo) & (j < z_lo))
    def _():
        u = proj()
        row = lax.broadcasted_iota(jnp.int32, u.shape, 0)
        conv = cb_ref[...] + cw_ref[SSM_CONV - 1:SSM_CONV, :] * u
        for s in range(1, SSM_CONV):
            shifted = jnp.where(row >= s, pltpu.roll(u, s, 0), 0.0)
            conv = conv + cw_ref[SSM_CONV - 1 - s:SSM_CONV - s, :] * shifted
        rest_ref[...] = _silu(conv)

    @pl.when((j >= z_lo) & (j < g_lo))
    def _():
        rest_ref[...] = _silu(proj())

    @pl.when((j >= g_lo) & (j < g_hi))
    def _():
        rest_ref[...] = 0.5 * jnp.tanh(0.5 * proj()) + 0.5

    @pl.when(j >= g_hi)
    def _():
        rest_ref[...] = proj()


def _inproj(x2, norm_w, mod, w_p, qk_w, bd, conv_w, conv_b, l, rows_per_batch):
    t, d = x2.shape
    tm = rows_per_batch
    nj = (QKV_COLS + REST_COLS) // IN_TN
    nxb = SSM_CONV_DIM // IN_TN
    xblk = lambda j: jnp.clip(j - 3, 0, nxb - 1)
    return pl.pallas_call(
        _inproj_kernel,
        grid=(t // tm, nj),
        in_specs=[
            pl.BlockSpec((tm, d), lambda i, j: (i, 0)),
            pl.BlockSpec((None, 1, d), lambda i, j: (l, 0, 0)),
            pl.BlockSpec((None, None, 6, d), lambda i, j: (l, i, 0, 0)),
            pl.BlockSpec((None, d, IN_TN), lambda i, j: (l, 0, j)),
            pl.BlockSpec((None, None, 1, IN_TN), lambda i, j: (l, jnp.minimum(j, 1), 0, 0)),
            pl.BlockSpec((IN_TN, IN_TN), lambda i, j: (0, 0)),
            pl.BlockSpec((None, SSM_CONV, IN_TN), lambda i, j: (l, 0, xblk(j))),
            pl.BlockSpec((None, 1, IN_TN), lambda i, j: (l, 0, xblk(j))),
        ],
        out_specs=[
            pl.BlockSpec((tm, IN_TN), lambda i, j: (i, jnp.minimum(j, 2))),
            pl.BlockSpec((tm, IN_TN), lambda i, j: (i, jnp.maximum(j - 3, 0))),
        ],
        out_shape=[jax.ShapeDtypeStruct((t, QKV_COLS), bf16), jax.ShapeDtypeStruct((t, REST_COLS), f32)],
        scratch_shapes=[pltpu.VMEM((tm, d), bf16)],
        compiler_params=_cparams(("arbitrary", "arbitrary")),
        name="inproj",
    )(x2, norm_w, mod, w_p, qk_w, bd, conv_w, conv_b)


def _rope_apply(x, c, sp, sm):
    return x * c + pltpu.roll(x, 16, 1) * sp + pltpu.roll(x, LANES - 16, 1) * sm


def _mla_prep_kernel(lat_ref, qan_ref, kvan_ref, wuq_ref, wuk_ref, wuv_ref, qn_ref, kn_ref,
                     c_ref, sp_ref, sm_ref, q_ref, k_ref, v_ref):
    lat = lat_ref[...]
    c, sp, sm = c_ref[...], sp_ref[...], sm_ref[...]
    cq = lat[:, :MLA_Q_RANK]
    cqn = (cq * lax.rsqrt(jnp.mean(cq * cq, axis=-1, keepdims=True) + EPS) * qan_ref[...]).astype(bf16)
    ckv = lat[:, MLA_Q_RANK:MLA_Q_RANK + MLA_KV_RANK]
    ckvn = (ckv * lax.rsqrt(jnp.mean(ckv * ckv, axis=-1, keepdims=True) + EPS) * kvan_ref[...]).astype(bf16)
    misc = lat[:, MLA_Q_RANK + MLA_KV_RANK:]
    lane = _lane_iota(misc.shape)
    krb = jnp.where((lane >= MISC_KR_LANE) & (lane < MISC_KR_LANE + MLA_ROPE), misc, 0.0)
    qf = _dot(cqn, wuq_ref[...])
    kf = _dot(ckvn, wuk_ref[...])
    v_ref[...] = _dot(ckvn, wuv_ref[...]).astype(bf16)
    inv = 1.0 / MLA_QK
    for h in range(MLA_HEADS):
        sl = slice(h * LANES, (h + 1) * LANES)
        qh = qf[:, sl]
        qh = qh * lax.rsqrt(jnp.sum(qh * qh, axis=-1, keepdims=True) * inv + EPS) * qn_ref[...]
        q_ref[:, sl] = _rope_apply(qh, c, sp, sm).astype(bf16)
        kh = kf[:, sl] + krb
        kh = kh * lax.rsqrt(jnp.sum(kh * kh, axis=-1, keepdims=True) * inv + EPS) * kn_ref[...]
        k_ref[:, sl] = _rope_apply(kh, c, sp, sm).astype(bf16)


def _mla_prep(rest, qan, kvan, wuq, wuk, wuv, qn, kn, ctab, sptab, smtab, l):
    t = rest.shape[0]
    tm = min(512, t)
    lat_blk = R_LAT // 512
    hq = MLA_HEADS * LANES
    vec = lambda n: pl.BlockSpec((None, 1, n), lambda i: (l, 0, 0))
    tab = pl.BlockSpec((tm, LANES), lambda i: (i, 0))
    return pl.pallas_call(
        _mla_prep_kernel,
        grid=(t // tm,),
        in_specs=[
            pl.BlockSpec((tm, 512), lambda i: (i, lat_blk)),
            vec(MLA_Q_RANK), vec(MLA_KV_RANK),
            pl.BlockSpec((None, MLA_Q_RANK, hq), lambda i: (l, 0, 0)),
            pl.BlockSpec((None, MLA_KV_RANK, hq), lambda i: (l, 0, 0)),
            pl.BlockSpec((None, MLA_KV_RANK, MLA_HEADS * MLA_V), lambda i: (l, 0, 0)),
            vec(LANES), vec(LANES), tab, tab, tab,
        ],
        out_specs=[
            pl.BlockSpec((tm, hq), lambda i: (i, 0)),
            pl.BlockSpec((tm, hq), lambda i: (i, 0)),
            pl.BlockSpec((tm, MLA_HEADS * MLA_V), lambda i: (i, 0)),
        ],
        out_shape=[jax.ShapeDtypeStruct((t, hq), bf16), jax.ShapeDtypeStruct((t, hq), bf16),
                   jax.ShapeDtypeStruct((t, MLA_HEADS * MLA_V), bf16)],
        compiler_params=_cparams(("arbitrary",)),
        name="mla_prep",
    )(rest, qan, kvan, wuq, wuk, wuv, qn, kn, ctab, sptab, smtab)


def _flash_t(q_list, k_ref, kcols, vt_ref, vrows, m_ref, l_ref, acc_ref, bias_fn, qi, tq):
    nmap = len(q_list)
    m_ref[...] = jnp.full(m_ref.shape, -jnp.inf, f32)
    l_ref[...] = jnp.zeros(l_ref.shape, f32)
    acc_ref[...] = jnp.zeros(acc_ref.shape, f32)

    def step(kj, which, masked):
        start = pl.multiple_of(kj * tq, tq)
        if masked:
            krow = lax.broadcasted_iota(jnp.int32, (tq, tq), 0)
            qcol = lax.broadcasted_iota(jnp.int32, (tq, tq), 1)
            keep = krow <= qcol
        def qk(a):
            k = k_ref[pl.ds(start, tq), kcols[a]:kcols[a] + LANES]
            return _dot_nt(k, q_list[a])

        s_list = [qk(a) for a in range(min(ATT_SKEW, nmap))]
        for a in range(nmap):
            if a + ATT_SKEW < nmap:
                s_list.append(qk(a + ATT_SKEW))
            s = s_list[a]
            if which is not None and bias_fn is not None:
                s = s + bias_fn(a, which)
            if masked:
                s = jnp.where(keep, s, NEG_BIG)
            m_prev = m_ref[a]
            m_new = jnp.maximum(m_prev, jnp.max(s, axis=0, keepdims=True))
            alpha = jnp.exp2(m_prev - m_new)
            e = jnp.exp2(s - m_new)
            l_ref[a] = alpha * l_ref[a] + jnp.sum(e, axis=0, keepdims=True)
            m_ref[a] = m_new
            vt = vt_ref[kj, vrows[a]:vrows[a] + LANES, :]
            acc_ref[a] = alpha * acc_ref[a] + _dot(vt, e.astype(bf16))

    def body(kj, carry):
        step(kj, None, False)
        return carry

    if bias_fn is None:
        lax.fori_loop(0, qi, body, 0)
    else:
        lax.fori_loop(0, qi - 1, body, 0)

        @pl.when(qi >= 1)
        def _():
            step(qi - 1, 1, False)

    step(qi, 0, True)


def _fill_vt(v_ref, vt_ref, tq):
    for c in range(vt_ref.shape[0]):
        vt_ref[c] = v_ref[c * tq:(c + 1) * tq, :].astype(f32).T.astype(bf16)


def _diff_attn_kernel(q_ref, k_ref, v_ref, bias_ref, lp_ref, sub_ref, o_ref, vt_ref, m_ref, l_ref, acc_ref,
                      *, lam_init, tq):
    qi = pl.program_id(1)

    @pl.when(qi == 0)
    def _():
        _fill_vt(v_ref, vt_ref, tq)

    lane = _lane_iota((tq, LANES))
    q_list, kcols, vrows = [], [], []
    for h in range(DA_HEADS):
        q = q_ref[:, h * LANES:(h + 1) * LANES]
        zero = jnp.zeros_like(q)
        q_list += [jnp.where(lane < DA_HEAD_DIM, q, zero), jnp.where(lane >= DA_HEAD_DIM, q, zero)]
        kcols += [h * LANES, h * LANES]
        vrows += [h * LANES, h * LANES]
    _flash_t(q_list, k_ref, kcols, vt_ref, vrows, m_ref, l_ref, acc_ref,
             lambda a, which: bias_ref[a // 2, which], qi, tq)
    lp = lp_ref[...]
    lam = (jnp.exp(jnp.sum(lp[0:1] * lp[1:2], axis=-1, keepdims=True))
           - jnp.exp(jnp.sum(lp[2:3] * lp[3:4], axis=-1, keepdims=True)) + lam_init)
    for h in range(DA_HEADS):
        o_t = acc_ref[2 * h] / l_ref[2 * h] - lam * (acc_ref[2 * h + 1] / l_ref[2 * h + 1])
        o = o_t.T
        o = o * lax.rsqrt(jnp.mean(o * o, axis=-1, keepdims=True) + EPS) * sub_ref[...]
        o_ref[:, h * LANES:(h + 1) * LANES] = (o * (1.0 - lam_init)).astype(bf16)


def _diff_attn(qkv3, bias, da_lambda, da_subln, l, lam_init):
    bsz, s, _ = qkv3.shape
    tq = bias.shape[-1]
    nq = s // tq
    hw = DA_HEADS * LANES
    kern = functools.partial(_diff_attn_kernel, lam_init=lam_init, tq=tq)
    return pl.pallas_call(
        kern,
        grid=(bsz, nq),
        in_specs=[
            pl.BlockSpec((None, tq, hw), lambda b, i: (b, i, 0)),
            pl.BlockSpec((None, s, hw), lambda b, i: (b, 0, 1)),
            pl.BlockSpec((None, s, hw), lambda b, i: (b, 0, 2)),
            pl.BlockSpec((DA_HEADS, 2, tq, tq), lambda b, i: (0, 0, 0, 0)),
            pl.BlockSpec((None, 4, DA_HEAD_DIM), lambda b, i: (l, 0, 0)),
            pl.BlockSpec((None, 1, LANES), lambda b, i: (l, 0, 0)),
        ],
        out_specs=pl.BlockSpec((None, tq, hw), lambda b, i: (b, i, 0)),
        out_shape=jax.ShapeDtypeStruct((bsz, s, hw), bf16),
        scratch_shapes=[pltpu.VMEM((nq, hw, tq), bf16),
                        pltpu.VMEM((2 * DA_HEADS, 1, tq), f32), pltpu.VMEM((2 * DA_HEADS, 1, tq), f32),
                        pltpu.VMEM((2 * DA_HEADS, LANES, tq), f32)],
        compiler_params=_cparams(("arbitrary", "arbitrary")),
        name="diff_attn",
    )(qkv3, qkv3, qkv3, bias, da_lambda, da_subln)


def _mla_attn_kernel(q_ref, k_ref, v_ref, o_ref, vt_ref, m_ref, l_ref, acc_ref, *, tq):
    qi = pl.program_id(1)

    @pl.when(qi == 0)
    def _():
        _fill_vt(v_ref, vt_ref, tq)

    q_list = [q_ref[:, h * LANES:(h + 1) * LANES] for h in range(MLA_HEADS)]
    kcols = [h * LANES for h in range(MLA_HEADS)]
    vrows = [(h // 2) * LANES for h in range(MLA_HEADS)]
    _flash_t(q_list, k_ref, kcols, vt_ref, vrows, m_ref, l_ref, acc_ref, None, qi, tq)
    row = lax.broadcasted_iota(jnp.int32, (LANES, tq), 0)
    for p in range(MLA_HEADS // 2):
        o_t = jnp.where(row < MLA_V, acc_ref[2 * p] / l_ref[2 * p], acc_ref[2 * p + 1] / l_ref[2 * p + 1])
        o_ref[:, p * LANES:(p + 1) * LANES] = o_t.T.astype(bf16)


def _mla_attn(q3, k3, v3, tq):
    bsz, s, hq = q3.shape
    nq = s // tq
    hv = v3.shape[-1]
    kern = functools.partial(_mla_attn_kernel, tq=tq)
    return pl.pallas_call(
        kern,
        grid=(bsz, nq),
        in_specs=[
            pl.BlockSpec((None, tq, hq), lambda b, i: (b, i, 0)),
            pl.BlockSpec((None, s, hq), lambda b, i: (b, 0, 0)),
            pl.BlockSpec((None, s, hv), lambda b, i: (b, 0, 0)),
        ],
        out_specs=pl.BlockSpec((None, tq, hv), lambda b, i: (b, i, 0)),
        out_shape=jax.ShapeDtypeStruct((bsz, s, hv), bf16),
        scratch_shapes=[pltpu.VMEM((nq, hv, tq), bf16),
                        pltpu.VMEM((MLA_HEADS, 1, tq), f32), pltpu.VMEM((MLA_HEADS, 1, tq), f32),
                        pltpu.VMEM((MLA_HEADS, LANES, tq), f32)],
        compiler_params=_cparams(("arbitrary", "arbitrary")),
        name="mla_attn",
    )(q3, k3, v3)


def _split3(x):
    hi = x.astype(bf16)
    r1 = x - hi.astype(f32)
    mid = r1.astype(bf16)
    lo = (r1 - mid.astype(f32)).astype(bf16)
    return hi, mid, lo


def _pair_bcast(x, j, lane):
    return jnp.where(lane < SSM_HEAD_DIM, x[:, 2 * j:2 * j + 1], x[:, 2 * j + 1:2 * j + 2])


def _ssd_kernel(xbc_ref, z_ref, misc_ref, dtb_ref, alog_ref, dvec_ref, nw_ref, o_ref, st_ref, y_ref):
    c = pl.program_id(1)
    L = SSM_CHUNK
    G, N = SSM_GROUPS, SSM_STATE

    @pl.when(c == 0)
    def _():
        st_ref[...] = jnp.zeros(st_ref.shape, f32)

    xs = xbc_ref[:, :SSM_D_INNER]
    bm = xbc_ref[:, SSM_D_INNER:SSM_D_INNER + G * N]
    cm = xbc_ref[:, SSM_D_INNER + G * N:]

    lane = _lane_iota((L, LANES))
    lane1 = _lane_iota((1, LANES))
    dt = jax.nn.softplus(misc_ref[...] + dtb_ref[...])
    a_l = jnp.where(lane1 < SSM_HEADS, -jnp.exp(alog_ref[...]), 0.0)
    da = dt * a_l
    row = lax.broadcasted_iota(jnp.int32, (L, L), 0)
    col = lax.broadcasted_iota(jnp.int32, (L, L), 1)
    causal = col <= row
    tri = jnp.where(causal, 1.0, 0.0).astype(bf16)
    hi, mid, lo = _split3(da)
    cs = _dot(tri, hi) + _dot(tri, mid) + _dot(tri, lo)
    cs_t = cs.T
    cs_last = cs[L - 1:L, :]
    w_end = dt * jnp.exp(cs_last - cs)
    e_cs = jnp.exp(cs)
    cdec = jnp.exp(cs_last)

    for g in range(G):
        bm_g = bm[:, g * N:(g + 1) * N]
        cm_g = cm[:, g * N:(g + 1) * N].astype(bf16)
        cb = _dot_nt(cm_g, bm_g.astype(bf16))
        bm_t = bm_g.T.astype(bf16)
        for jj in range(2):
            j = 2 * g + jj
            sl = slice(j * LANES, (j + 1) * LANES)
            psl = slice(jj * LANES, (jj + 1) * LANES)
            xs_p = xs[:, sl]
            xdt = (xs_p * _pair_bcast(dt, j, lane)).astype(bf16)
            yd = []
            for hh in (2 * j, 2 * j + 1):
                seg = cs[:, hh:hh + 1] - cs_t[hh:hh + 1, :]
                decay = jnp.exp(jnp.where(causal, seg, -jnp.inf))
                yd.append(_dot((cb * decay).astype(bf16), xdt))
            y_diag = jnp.where(lane < SSM_HEAD_DIM, yd[0], yd[1])
            prev = st_ref[g, :, psl]
            y_off = _dot(cm_g, prev.astype(bf16)) * _pair_bcast(e_cs, j, lane)
            y_ref[:, sl] = y_diag + y_off + dvec_ref[:, sl] * xs_p
            xw = (xs_p * _pair_bcast(w_end, j, lane)).astype(bf16)
            cd = jnp.where(lane1 < SSM_HEAD_DIM, cdec[:, 2 * j:2 * j + 1], cdec[:, 2 * j + 1:2 * j + 2])
            st_ref[g, :, psl] = prev * cd + _dot(bm_t, xw)

    gw = SSM_D_INNER // G
    for g in range(G):
        sl = slice(g * gw, (g + 1) * gw)
        yz = y_ref[:, sl] * z_ref[:, sl]
        yz = yz * lax.rsqrt(jnp.mean(yz * yz, axis=-1, keepdims=True) + EPS) * nw_ref[:, sl]
        o_ref[:, sl] = yz.astype(bf16)


def _ssd(rest3, dt_bias, a_log, dvec, norm_w, l):
    bsz, s, _ = rest3.shape
    L = SSM_CHUNK
    nc = s // L
    misc_blk = (R_LAT + MLA_Q_RANK + MLA_KV_RANK) // LANES
    vec = lambda n: pl.BlockSpec((None, 1, n), lambda b, c: (l, 0, 0))
    return pl.pallas_call(
        _ssd_kernel,
        grid=(bsz, nc),
        in_specs=[
            pl.BlockSpec((None, L, SSM_CONV_DIM), lambda b, c: (b, c, R_XBC // SSM_CONV_DIM)),
            pl.BlockSpec((None, L, SSM_D_INNER), lambda b, c: (b, c, R_Z // SSM_D_INNER)),
            pl.BlockSpec((None, L, LANES), lambda b, c: (b, c, misc_blk)),
            vec(LANES), vec(LANES), vec(SSM_D_INNER), vec(SSM_D_INNER),
        ],
        out_specs=pl.BlockSpec((None, L, SSM_D_INNER), lambda b, c: (b, c, 0)),
        out_shape=jax.ShapeDtypeStruct((bsz, s, SSM_D_INNER), bf16),
        scratch_shapes=[
            pltpu.VMEM((SSM_GROUPS, SSM_STATE, 4 * SSM_HEAD_DIM), f32),
            pltpu.VMEM((L, SSM_D_INNER), f32),
        ],
        compiler_params=_cparams(("arbitrary", "arbitrary")),
        name="ssd",
    )(rest3, rest3, rest3, dt_bias, a_log, dvec, norm_w)


def _merge_kernel(ya_ref, yb_ref, ys_ref, ga_ref, gb_ref, gc_ref, x_ref, mod_ref,
                  wa_ref, wb_ref, wc_ref, wo_ref, o_ref):
    merged = (ga_ref[...] * _dot(ya_ref[...], wa_ref[...])
              + gb_ref[...] * _dot(yb_ref[...], wb_ref[...])
              + gc_ref[...] * _dot(ys_ref[...], wc_ref[...]))
    o_ref[...] = x_ref[...] + mod_ref[2:3, :] * _dot(merged.astype(bf16), wo_ref[...])


def _merge(ya, yb, ys, rest, x2, mod, wa, wb, wc, wo, l, rows_per_batch):
    t, d = x2.shape
    tm = min(512, rows_per_batch)
    tpb = rows_per_batch // tm
    gblk = R_GATES // d
    row = lambda n: pl.BlockSpec((tm, n), lambda i: (i, 0))
    wspec = lambda k: pl.BlockSpec((None, k, d), lambda i: (l, 0, 0))
    return pl.pallas_call(
        _merge_kernel,
        grid=(t // tm,),
        in_specs=[
            row(ya.shape[1]), row(yb.shape[1]), row(ys.shape[1]),
            pl.BlockSpec((tm, d), lambda i: (i, gblk)),
            pl.BlockSpec((tm, d), lambda i: (i, gblk + 1)),
            pl.BlockSpec((tm, d), lambda i: (i, gblk + 2)),
            row(d),
            pl.BlockSpec((None, None, 6, d), lambda i: (l, i // tpb, 0, 0)),
            wspec(ya.shape[1]), wspec(yb.shape[1]), wspec(ys.shape[1]), wspec(d),
        ],
        out_specs=row(d),
        out_shape=jax.ShapeDtypeStruct((t, d), f32),
        compiler_params=_cparams(("arbitrary",)),
        name="merge_out",
    )(ya, yb, ys, rest, rest, rest, x2, mod, wa, wb, wc, wo)


FFN_TF = 256
FFN_NF = FFN_DIM // FFN_TF


FFN_HALO = 16
FFN_RB = 128


def _ffn_kernel(x_ref, halo_ref, nw_ref, mod_ref, wup_ref, cw_ref, cb_ref, wd_ref,
                o_ref, h_ref, ua0_ref, ua1_ref, uv0_ref, uv1_ref, a0_ref, a1_ref, acc_ref):
    mi = pl.program_id(1)
    tm = x_ref.shape[0]
    nf = FFN_NF
    nrb = tm // FFN_RB
    H = FFN_HALO

    def norm_mod(x):
        y = x * lax.rsqrt(jnp.mean(x * x, axis=-1, keepdims=True) + EPS) * nw_ref[...]
        return (y * (1.0 + mod_ref[4:5, :]) + mod_ref[3:4, :]).astype(bf16)

    keep = jnp.where(mi > 0, 1.0, 0.0)
    h_ref[0:H, :] = norm_mod(halo_ref[...] * keep) * keep.astype(bf16)
    h_ref[H:, :] = norm_mod(x_ref[...])

    ua_refs, uv_refs, a_refs = (ua0_ref, ua1_ref), (uv0_ref, uv1_ref), (a0_ref, a1_ref)

    def up(f, slot, r):
        lo = 0 if r == 0 else H + r * FFN_RB
        hi = H + (r + 1) * FFN_RB
        h = h_ref[lo:hi, :]
        ua_refs[slot][lo:hi, :] = _dot(h, wup_ref[f])
        uv_refs[slot][lo:hi, :] = _dot(h, wup_ref[nf + f])

    def conv(u_ref, g, r):
        base = H + r * FFN_RB
        out = cb_ref[g] + cw_ref[g, FFN_CONV - 1:FFN_CONV, :] * u_ref[base:base + FFN_RB, :]
        for s in range(1, FFN_CONV):
            out = out + cw_ref[g, FFN_CONV - 1 - s:FFN_CONV - s, :] * u_ref[base - s:base - s + FFN_RB, :]
        return out

    def act(f, slot, r):
        rows = slice(r * FFN_RB, (r + 1) * FFN_RB)
        a_refs[slot][rows, :] = (_silu(conv(ua_refs[slot], f, r)) * conv(uv_refs[slot], nf + f, r)).astype(bf16)

    def down(f, slot, r, first=False):
        rows = slice(r * FFN_RB, (r + 1) * FFN_RB)
        res = _dot(a_refs[slot][rows, :], wd_ref[f])
        if first:
            acc_ref[rows, :] = res
        else:
            acc_ref[rows, :] += res

    def stage(f_down, f_up, f_act, slot_act, first=False):
        for r in range(nrb):
            if f_down is not None:
                down(f_down, 1 - slot_act, r, first)
            if f_up is not None:
                up(f_up, 1 - slot_act, r)
            if f_act is not None:
                act(f_act, slot_act, r)

    for r in range(nrb):
        up(0, 0, r)
    stage(None, 1, 0, 0)
    stage(0, 2, 1, 1, first=True)

    assert (nf - 3) % 2 == 0

    def body(i, carry):
        f = 2 + 2 * i
        stage(f - 1, f + 1, f, 0)
        stage(f, f + 2, f + 1, 1)
        return carry

    lax.fori_loop(0, (nf - 3) // 2, body, 0)
    last = (nf - 1) % 2
    stage(nf - 2, None, nf - 1, last)
    for r in range(nrb):
        down(nf - 1, last, r)
    o_ref[...] = x_ref[...] + mod_ref[5:6, :] * acc_ref[...]


def _ffn(x3, norm_w, mod, wup3, conv_w3, conv_b3, wdown3, l):
    bsz, s, d = x3.shape
    tm = min(1024, s)
    tf, nf = FFN_TF, FFN_NF
    hb = tm // FFN_HALO
    resident = dict(pipeline_mode=pl.Buffered(1))
    return pl.pallas_call(
        _ffn_kernel,
        grid=(bsz, s // tm),
        in_specs=[
            pl.BlockSpec((None, tm, d), lambda b, m: (b, m, 0)),
            pl.BlockSpec((None, FFN_HALO, d), lambda b, m: (b, jnp.maximum(m * hb - 1, 0), 0)),
            pl.BlockSpec((None, 1, d), lambda b, m: (l, 0, 0)),
            pl.BlockSpec((None, None, 6, d), lambda b, m: (l, b, 0, 0)),
            pl.BlockSpec((None, 2 * nf, d, tf), lambda b, m: (l, 0, 0, 0), **resident),
            pl.BlockSpec((None, 2 * nf, FFN_CONV, tf), lambda b, m: (l, 0, 0, 0)),
            pl.BlockSpec((None, 2 * nf, 1, tf), lambda b, m: (l, 0, 0, 0)),
            pl.BlockSpec((None, nf, tf, d), lambda b, m: (l, 0, 0, 0), **resident),
        ],
        out_specs=pl.BlockSpec((None, tm, d), lambda b, m: (b, m, 0)),
        out_shape=jax.ShapeDtypeStruct((bsz, s, d), f32),
        scratch_shapes=[
            pltpu.VMEM((tm + FFN_HALO, d), bf16),
            pltpu.VMEM((tm + FFN_HALO, tf), f32), pltpu.VMEM((tm + FFN_HALO, tf), f32),
            pltpu.VMEM((tm + FFN_HALO, tf), f32), pltpu.VMEM((tm + FFN_HALO, tf), f32),
            pltpu.VMEM((tm, tf), bf16), pltpu.VMEM((tm, tf), bf16),
            pltpu.VMEM((tm, d), f32),
        ],
        compiler_params=_cparams(("arbitrary", "arbitrary")),
        name="ffn",
    )(x3, x3, norm_w, mod, wup3, conv_w3, conv_b3, wdown3)


def _pack_w_in(w_in):
    nl, d, _ = w_in.shape
    sizes = (512, 512, 512, MLA_Q_RANK, MLA_KV_RANK, MLA_ROPE, SSM_D_INNER, SSM_CONV_DIM, SSM_HEADS, 3 * D_MODEL)
    offs = np.concatenate([[0], np.cumsum(sizes)])
    qa, ka, va, cq, ckv, kr, z, xbc, dt, gates = [w_in[..., int(offs[i]):int(offs[i + 1])] for i in range(10)]
    zeros = lambda n: jnp.zeros((nl, d, n), w_in.dtype)
    misc = jnp.concatenate([dt, zeros(MISC_KR_LANE - SSM_HEADS), kr, zeros(LANES - MISC_KR_LANE - MLA_ROPE)], axis=-1)
    return jnp.concatenate([qa, ka, va, xbc, z, gates, cq, ckv, misc], axis=-1).astype(bf16)


def _pad_lanes(v, n):
    return jnp.pad(v, [(0, 0)] * (v.ndim - 1) + [(0, n - v.shape[-1])])


def kernel(x, c, positions, rel_bias, ada_w, ada_b, norm_mix, norm_ffn, w_in, da_q_norm, da_k_norm, da_lambda, da_subln, mla_q_a_norm, mla_kv_a_norm, mla_w_uq, mla_w_ukv, mla_q_norm, mla_k_norm, ssm_conv_w, ssm_conv_b, ssm_dt_bias, ssm_a_log, ssm_d, ssm_norm, w_branch_a, w_branch_b, w_branch_c, w_out, ffn_w_up, ffn_conv_w, ffn_conv_b, ffn_w_down):
    bsz, seq, d = x.shape
    nl = w_in.shape[0]
    t = bsz * seq
    tq = min(ATT_TQ, seq)

    w_in_p = _pack_w_in(w_in)
    qk_w = jnp.stack([jnp.tile(da_q_norm, (1, 2 * DA_HEADS)) * (DA_HEAD_DIM ** -0.5 * LOG2E),
                      jnp.tile(da_k_norm, (1, 2 * DA_HEADS))], axis=1).reshape(nl, 2, 1, IN_TN)
    blk = np.arange(IN_TN) // DA_HEAD_DIM
    bd = jnp.asarray(blk[:, None] == blk[None, :], bf16)
    wuq = _pad_lanes(mla_w_uq.reshape(nl, MLA_Q_RANK, MLA_HEADS, MLA_QK), LANES)
    wuq = wuq.reshape(nl, MLA_Q_RANK, MLA_HEADS * LANES).astype(bf16)
    wukv = mla_w_ukv.reshape(nl, MLA_KV_RANK, MLA_HEADS, MLA_NOPE + MLA_V)
    wuk = _pad_lanes(wukv[..., :MLA_NOPE], LANES).reshape(nl, MLA_KV_RANK, MLA_HEADS * LANES).astype(bf16)
    wuv = wukv[..., MLA_NOPE:].reshape(nl, MLA_KV_RANK, MLA_HEADS * MLA_V).astype(bf16)
    qn = (_pad_lanes(mla_q_norm, LANES) * (MLA_QK ** -0.5 * LOG2E)).reshape(nl, 1, LANES)
    kn = _pad_lanes(mla_k_norm, LANES).reshape(nl, 1, LANES)
    dt_bias = _pad_lanes(ssm_dt_bias, LANES).reshape(nl, 1, LANES)
    a_log = _pad_lanes(ssm_a_log, LANES).reshape(nl, 1, LANES)
    dvec = jnp.repeat(ssm_d, SSM_HEAD_DIM, axis=-1).reshape(nl, 1, SSM_D_INNER)
    v3 = lambda a: a.reshape(nl, 1, a.shape[-1])
    wa, wb, wc, wo = (w.astype(bf16) for w in (w_branch_a, w_branch_b, w_branch_c, w_out))
    nf2 = 2 * FFN_NF
    wup3 = ffn_w_up.astype(bf16).reshape(nl, d, nf2, FFN_TF).transpose(0, 2, 1, 3)
    wdown3 = ffn_w_down.astype(bf16).reshape(nl, FFN_NF, FFN_TF, d)
    fcw3 = ffn_conv_w.reshape(nl, FFN_CONV, nf2, FFN_TF).transpose(0, 2, 1, 3)
    fcb3 = ffn_conv_b.reshape(nl, nf2, 1, FFN_TF)

    mod = _ada_mod(c, ada_w, ada_b).reshape(nl, bsz, 6, d)
    ctab, sptab, smtab = _rope_tables(positions)
    bias = _bias_tiles(rel_bias, tq)

    x2 = x.reshape(t, d)
    for l in range(nl):
        lam_init = 0.8 - 0.6 * math.exp(-0.3 * l)
        qkv, rest = _inproj(x2, v3(norm_mix), mod, w_in_p, qk_w, bd, ssm_conv_w, v3(ssm_conv_b), l, seq)
        q_b, k_b, v_b = _mla_prep(rest, v3(mla_q_a_norm), v3(mla_kv_a_norm), wuq, wuk, wuv, qn, kn,
                                  ctab, sptab, smtab, l)
        ya = _diff_attn(qkv.reshape(bsz, seq, QKV_COLS), bias, da_lambda, v3(da_subln), l, lam_init)
        yb = _mla_attn(q_b.reshape(bsz, seq, -1), k_b.reshape(bsz, seq, -1), v_b.reshape(bsz, seq, -1), tq)
        ys = _ssd(rest.reshape(bsz, seq, REST_COLS), dt_bias, a_log, dvec, v3(ssm_norm), l)
        x2 = _merge(ya.reshape(t, -1), yb.reshape(t, -1), ys.reshape(t, -1), rest, x2, mod, wa, wb, wc, wo, l, seq)
        x2 = _ffn(x2.reshape(bsz, seq, d), v3(norm_ffn), mod, wup3, fcw3, fcb3, wdown3, l).reshape(t, d)
    return x2.reshape(bsz, seq, d)
```

```python
import functools
import math

import jax
import jax.numpy as jnp
import numpy as np
from jax import lax
from jax.experimental import pallas as pl
from jax.experimental.pallas import tpu as pltpu

f32 = jnp.float32
bf16 = jnp.bfloat16

D_MODEL = 1024
DA_HEADS = 4
DA_HEAD_DIM = 64
MLA_HEADS = 8
MLA_Q_RANK = 256
MLA_KV_RANK = 128
MLA_NOPE = 64
MLA_ROPE = 32
MLA_V = 64
MLA_QK = MLA_NOPE + MLA_ROPE
SSM_D_INNER = 1024
SSM_HEAD_DIM = 64
SSM_HEADS = 16
SSM_GROUPS = 4
SSM_STATE = 128
SSM_CONV = 4
SSM_CHUNK = 128
SSM_CONV_DIM = SSM_D_INNER + 2 * SSM_GROUPS * SSM_STATE
FFN_DIM = 2816
FFN_CONV = 3
REL_BUCKETS = 32
REL_MAX_DIST = 128
ROPE_THETA = 10000.0
EPS = 1e-6

LANES = 128
SUBLANES = 8
VMEM_LIMIT = 56 * 1024 * 1024

IN_TN = 512
QKV_COLS = 3 * 512
R_XBC = 0
R_Z = R_XBC + SSM_CONV_DIM
R_GATES = R_Z + SSM_D_INNER
R_LAT = R_GATES + 3 * D_MODEL
REST_COLS = R_LAT + 512
MISC_KR_LANE = 64
NEG_BIG = -1e30
LOG2E = math.log2(math.e)

ATT_TQ = 256
ATT_SKEW = 8


def _cparams(sem):
    return pltpu.CompilerParams(dimension_semantics=sem, vmem_limit_bytes=VMEM_LIMIT)


def _lane_iota(shape):
    return lax.broadcasted_iota(jnp.int32, shape, len(shape) - 1)


def _dot(a, b):
    return jnp.dot(a, b, preferred_element_type=f32)


def _dot_nt(a, b):
    return lax.dot_general(a, b, (((1,), (1,)), ((), ())), preferred_element_type=f32)


def _sigmoid(x):
    return 1.0 / (1.0 + jnp.exp(-x))


def _silu(x):
    return x * _sigmoid(x)


def _ada_kernel(c_ref, w_ref, b_ref, o_ref):
    ca = _silu(c_ref[...]).astype(bf16)
    o_ref[...] = _dot(ca, w_ref[...].astype(bf16)) + b_ref[...]


def _ada_mod(c, ada_w, ada_b):
    nl, d, n = ada_w.shape
    bsz = c.shape[0]
    tn = 1024
    return pl.pallas_call(
        _ada_kernel,
        grid=(nl, n // tn),
        in_specs=[
            pl.BlockSpec((bsz, d), lambda l, j: (0, 0)),
            pl.BlockSpec((None, d, tn), lambda l, j: (l, 0, j)),
            pl.BlockSpec((None, 1, tn), lambda l, j: (l, 0, j)),
        ],
        out_specs=pl.BlockSpec((None, bsz, tn), lambda l, j: (l, 0, j)),
        out_shape=jax.ShapeDtypeStruct((nl, bsz, n), f32),
        compiler_params=_cparams(("arbitrary", "arbitrary")),
        name="ada_mod",
    )(c, ada_w, ada_b.reshape(nl, 1, n))


def _rope_kernel(pos_ref, freq_ref, c_ref, sp_ref, sm_ref):
    ang = pos_ref[...].astype(f32) * freq_ref[...]
    cos, sin = jnp.cos(ang), jnp.sin(ang)
    lane = _lane_iota(ang.shape)
    lo, mid, hi = MISC_KR_LANE, MISC_KR_LANE + MLA_ROPE // 2, MISC_KR_LANE + MLA_ROPE
    c_ref[...] = jnp.where((lane >= lo) & (lane < hi), cos, 1.0)
    sp_ref[...] = jnp.where((lane >= mid) & (lane < hi), sin, 0.0)
    sm_ref[...] = jnp.where((lane >= lo) & (lane < mid), -sin, 0.0)


def _rope_tables(positions):
    t = positions.size
    tm = min(t, 2048)
    inv_freq = ROPE_THETA ** (-jnp.arange(0, MLA_ROPE, 2, dtype=f32) / MLA_ROPE)
    freq = jnp.zeros((LANES,), f32).at[MISC_KR_LANE:MISC_KR_LANE + MLA_ROPE].set(jnp.tile(inv_freq, 2))
    out = jax.ShapeDtypeStruct((t, LANES), f32)
    spec = pl.BlockSpec((tm, LANES), lambda i: (i, 0))
    return pl.pallas_call(
        _rope_kernel,
        grid=(t // tm,),
        in_specs=[pl.BlockSpec((tm, 1), lambda i: (i, 0)), pl.BlockSpec((1, LANES), lambda i: (0, 0))],
        out_specs=[spec, spec, spec],
        out_shape=[out, out, out],
        compiler_params=_cparams(("arbitrary",)),
        name="rope_tables",
    )(positions.reshape(t, 1), freq.reshape(1, LANES))


def _bias_kernel(rb_ref, o_ref):
    h, which = pl.program_id(0), pl.program_id(1)
    tq = o_ref.shape[-1]
    row = lax.broadcasted_iota(jnp.int32, (tq, tq), 0)
    col = lax.broadcasted_iota(jnp.int32, (tq, tq), 1)
    rel = col - row + which * tq
    d = jnp.maximum(rel, 0)
    max_exact = REL_BUCKETS // 2
    large = max_exact + (jnp.log(jnp.maximum(d, 1).astype(f32) / max_exact)
                         / math.log(REL_MAX_DIST / max_exact) * (REL_BUCKETS - max_exact)).astype(jnp.int32)
    large = jnp.minimum(large, REL_BUCKETS - 1)
    bucket = jnp.where(d < max_exact, d, large)
    val = jnp.full((tq, tq), rb_ref[0, h], f32)
    for k in range(1, REL_BUCKETS):
        val = jnp.where(bucket == k, rb_ref[k, h], val)
    val = (val - rb_ref[REL_BUCKETS - 1, h]) * LOG2E
    val = jnp.where(rel < 0, NEG_BIG, val)
    o_ref[...] = jnp.where(which >= 2, 0.0, val)


def _bias_tiles(rel_bias, tq):
    return pl.pallas_call(
        _bias_kernel,
        grid=(DA_HEADS, 3),
        in_specs=[pl.BlockSpec(memory_space=pltpu.SMEM)],
        out_specs=pl.BlockSpec((None, None, tq, tq), lambda h, w: (h, w, 0, 0)),
        out_shape=jax.ShapeDtypeStruct((DA_HEADS, 3, tq, tq), f32),
        compiler_params=_cparams(("arbitrary", "arbitrary")),
        name="bias_tiles",
    )(rel_bias)


def _inproj_kernel(x_ref, nw_ref, mod_ref, w_ref, qkw_ref, bd_ref, cw_ref, cb_ref, qkv_ref, rest_ref, h_ref, u_ref):
    j = pl.program_id(1)
    tm = x_ref.shape[0]

    @pl.when(j == 0)
    def _():
        x = x_ref[...]
        y = x * lax.rsqrt(jnp.mean(x * x, axis=-1, keepdims=True) + EPS) * nw_ref[...]
        h_ref[...] = (y * (1.0 + mod_ref[1:2, :]) + mod_ref[0:1, :]).astype(bf16)

    def proj():
        return _dot(h_ref[...], w_ref[...])

    @pl.when(j < 2)
    def _():
        acc = proj()
        ss = _dot((acc * acc).astype(bf16), bd_ref[...]) * (1.0 / DA_HEAD_DIM)
        qkv_ref[...] = (acc * lax.rsqrt(ss + EPS) * qkw_ref[...]).astype(bf16)

    @pl.when(j == 2)
    def _():
        qkv_ref[...] = proj().astype(bf16)

    x_lo, z_lo, g_lo, g_hi = (3 + off // IN_TN for off in (R_XBC, R_Z, R_GATES, R_LAT))

    @pl.when((j >= x_lo) & (j < z_lo))
    def _():
        u_ref[0:SUBLANES, :] = jnp.zeros((SUBLANES, IN_TN), f32)
        u_ref[SUBLANES:, :] = proj()
        conv = cb_ref[...] + cw_ref[SSM_CONV - 1:SSM_CONV, :] * u_ref[SUBLANES:, :]
        for s in range(1, SSM_CONV):
            conv = conv + cw_ref[SSM_CONV - 1 - s:SSM_CONV - s, :] * u_ref[SUBLANES - s:SUBLANES - s + tm, :]
        rest_ref[...] = _silu(conv)

    @pl.when((j >= z_lo) & (j < g_lo))
    def _():
        rest_ref[...] = _silu(proj())

    @pl.when((j >= g_lo) & (j < g_hi))
    def _():
        rest_ref[...] = 0.5 * jnp.tanh(0.5 * proj()) + 0.5

    @pl.when(j >= g_hi)
    def _():
        rest_ref[...] = proj()


def _inproj(x2, norm_w, mod, w_p, qk_w, bd, conv_w, conv_b, l, rows_per_batch):
    t, d = x2.shape
    tm = rows_per_batch
    nj = (QKV_COLS + REST_COLS) // IN_TN
    nxb = SSM_CONV_DIM // IN_TN
    xblk = lambda j: jnp.clip(j - 3, 0, nxb - 1)
    return pl.pallas_call(
        _inproj_kernel,
        grid=(t // tm, nj),
        in_specs=[
            pl.BlockSpec((tm, d), lambda i, j: (i, 0)),
            pl.BlockSpec((None, 1, d), lambda i, j: (l, 0, 0)),
            pl.BlockSpec((None, None, 6, d), lambda i, j: (l, i, 0, 0)),
            pl.BlockSpec((None, d, IN_TN), lambda i, j: (l, 0, j)),
            pl.BlockSpec((None, None, 1, IN_TN), lambda i, j: (l, jnp.minimum(j, 1), 0, 0)),
            pl.BlockSpec((IN_TN, IN_TN), lambda i, j: (0, 0)),
            pl.BlockSpec((None, SSM_CONV, IN_TN), lambda i, j: (l, 0, xblk(j))),
            pl.BlockSpec((None, 1, IN_TN), lambda i, j: (l, 0, xblk(j))),
        ],
        out_specs=[
            pl.BlockSpec((tm, IN_TN), lambda i, j: (i, jnp.minimum(j, 2))),
            pl.BlockSpec((tm, IN_TN), lambda i, j: (i, jnp.maximum(j - 3, 0))),
        ],
        out_shape=[jax.ShapeDtypeStruct((t, QKV_COLS), bf16), jax.ShapeDtypeStruct((t, REST_COLS), f32)],
        scratch_shapes=[pltpu.VMEM((tm, d), bf16), pltpu.VMEM((tm + SUBLANES, IN_TN), f32)],
        compiler_params=_cparams(("arbitrary", "arbitrary")),
        name="inproj",
    )(x2, norm_w, mod, w_p, qk_w, bd, conv_w, conv_b)


def _rope_apply(x, c, sp, sm):
    return x * c + pltpu.roll(x, 16, 1) * sp + pltpu.roll(x, LANES - 16, 1) * sm


def _mla_prep_kernel(lat_ref, qan_ref, kvan_ref, wuq_ref, wuk_ref, wuv_ref, qn_ref, kn_ref,
                     c_ref, sp_ref, sm_ref, q_ref, k_ref, v_ref):
    lat = lat_ref[...]
    c, sp, sm = c_ref[...], sp_ref[...], sm_ref[...]
    cq = lat[:, :MLA_Q_RANK]
    cqn = (cq * lax.rsqrt(jnp.mean(cq * cq, axis=-1, keepdims=True) + EPS) * qan_ref[...]).astype(bf16)
    ckv = lat[:, MLA_Q_RANK:MLA_Q_RANK + MLA_KV_RANK]
    ckvn = (ckv * lax.rsqrt(jnp.mean(ckv * ckv, axis=-1, keepdims=True) + EPS) * kvan_ref[...]).astype(bf16)
    misc = lat[:, MLA_Q_RANK + MLA_KV_RANK:]
    lane = _lane_iota(misc.shape)
    krb = jnp.where((lane >= MISC_KR_LANE) & (lane < MISC_KR_LANE + MLA_ROPE), misc, 0.0)
    qf = _dot(cqn, wuq_ref[...])
    kf = _dot(ckvn, wuk_ref[...])
    v_ref[...] = _dot(ckvn, wuv_ref[...]).astype(bf16)
    inv = 1.0 / MLA_QK
    qn, kn = qn_ref[...], kn_ref[...]
    for h in range(MLA_HEADS):
        sl = slice(h * LANES, (h + 1) * LANES)
        qh = qf[:, sl]
        qh = qh * lax.rsqrt(jnp.sum(qh * qh, axis=-1, keepdims=True) * inv + EPS) * qn
        q_ref[:, sl] = _rope_apply(qh, c, sp, sm).astype(bf16)
        kh = kf[:, sl] + krb
        kh = kh * lax.rsqrt(jnp.sum(kh * kh, axis=-1, keepdims=True) * inv + EPS) * kn
        k_ref[:, sl] = _rope_apply(kh, c, sp, sm).astype(bf16)


def _mla_prep(rest, qan, kvan, wuq, wuk, wuv, qn, kn, ctab, sptab, smtab, l):
    t = rest.shape[0]
    tm = min(512, t)
    lat_blk = R_LAT // 512
    hq = MLA_HEADS * LANES
    vec = lambda n: pl.BlockSpec((None, 1, n), lambda i: (l, 0, 0))
    tab = pl.BlockSpec((tm, LANES), lambda i: (i, 0))
    return pl.pallas_call(
        _mla_prep_kernel,
        grid=(t // tm,),
        in_specs=[
            pl.BlockSpec((tm, 512), lambda i: (i, lat_blk)),
            vec(MLA_Q_RANK), vec(MLA_KV_RANK),
            pl.BlockSpec((None, MLA_Q_RANK, hq), lambda i: (l, 0, 0)),
            pl.BlockSpec((None, MLA_KV_RANK, hq), lambda i: (l, 0, 0)),
            pl.BlockSpec((None, MLA_KV_RANK, MLA_HEADS * MLA_V), lambda i: (l, 0, 0)),
            vec(LANES), vec(LANES), tab, tab, tab,
        ],
        out_specs=[
            pl.BlockSpec((tm, hq), lambda i: (i, 0)),
            pl.BlockSpec((tm, hq), lambda i: (i, 0)),
            pl.BlockSpec((tm, MLA_HEADS * MLA_V), lambda i: (i, 0)),
        ],
        out_shape=[jax.ShapeDtypeStruct((t, hq), bf16), jax.ShapeDtypeStruct((t, hq), bf16),
                   jax.ShapeDtypeStruct((t, MLA_HEADS * MLA_V), bf16)],
        compiler_params=_cparams(("arbitrary",)),
        name="mla_prep",
    )(rest, qan, kvan, wuq, wuk, wuv, qn, kn, ctab, sptab, smtab)


def _flash_t(q_list, k_ref, kcols, vt_ref, vrows, m_ref, l_ref, acc_ref, bias_fn, qi, tq):
    nmap = len(q_list)
    m_ref[...] = jnp.full(m_ref.shape, -jnp.inf, f32)
    l_ref[...] = jnp.zeros(l_ref.shape, f32)
    acc_ref[...] = jnp.zeros(acc_ref.shape, f32)

    def step(kj, which, masked):
        start = pl.multiple_of(kj * tq, tq)
        if masked:
            krow = lax.broadcasted_iota(jnp.int32, (tq, tq), 0)
            qcol = lax.broadcasted_iota(jnp.int32, (tq, tq), 1)
            keep = krow <= qcol
        def qk(a):
            k = k_ref[pl.ds(start, tq), kcols[a]:kcols[a] + LANES]
            s = _dot_nt(k, q_list[a])
            if which is not None and bias_fn is not None:
                s = s + bias_fn(a, which)
            return s

        s_list = [qk(a) for a in range(min(ATT_SKEW, nmap))]
        for a in range(nmap):
            if a + ATT_SKEW < nmap:
                s_list.append(qk(a + ATT_SKEW))
            s = s_list[a]
            if masked:
                s = jnp.where(keep, s, NEG_BIG)
            m_prev = m_ref[a]
            m_new = jnp.maximum(m_prev, jnp.max(s, axis=0, keepdims=True))
            alpha = jnp.exp2(m_prev - m_new)
            e = jnp.exp2(s - m_new)
            l_ref[a] = alpha * l_ref[a] + jnp.sum(e, axis=0, keepdims=True)
            m_ref[a] = m_new
            vt = vt_ref[kj, vrows[a]:vrows[a] + LANES, :]
            acc_ref[a] = alpha * acc_ref[a] + _dot(vt, e.astype(bf16))

    if bias_fn is None:
        def body(kj, carry):
            step(kj, None, False)
            return carry

        lax.fori_loop(0, qi, body, 0)
        step(qi, None, True)
    else:
        def body(kj, carry):
            step(kj, jnp.minimum(qi - kj, 2), False)
            return carry

        lax.fori_loop(0, qi + 1, body, 0)


def _fill_vt(v_ref, vt_ref, tq):
    for c in range(vt_ref.shape[0]):
        vt_ref[c] = v_ref[c * tq:(c + 1) * tq, :].astype(f32).T.astype(bf16)


def _diff_attn_kernel(q_ref, k_ref, v_ref, bias_ref, lp_ref, sub_ref, o_ref, vt_ref, m_ref, l_ref, acc_ref,
                      *, lam_init, tq):
    qi = pl.program_id(1)

    @pl.when(qi == 0)
    def _():
        _fill_vt(v_ref, vt_ref, tq)

    lane = _lane_iota((tq, LANES))
    q_list, kcols, vrows = [], [], []
    for h in range(DA_HEADS):
        q = q_ref[:, h * LANES:(h + 1) * LANES]
        zero = jnp.zeros_like(q)
        q_list += [jnp.where(lane < DA_HEAD_DIM, q, zero), jnp.where(lane >= DA_HEAD_DIM, q, zero)]
        kcols += [h * LANES, h * LANES]
        vrows += [h * LANES, h * LANES]
    _flash_t(q_list, k_ref, kcols, vt_ref, vrows, m_ref, l_ref, acc_ref,
             lambda a, which: bias_ref[a // 2, which], qi, tq)
    lp = lp_ref[...]
    lam = (jnp.exp(jnp.sum(lp[0:1] * lp[1:2], axis=-1, keepdims=True))
           - jnp.exp(jnp.sum(lp[2:3] * lp[3:4], axis=-1, keepdims=True)) + lam_init)
    for h in range(DA_HEADS):
        o_t = acc_ref[2 * h] / l_ref[2 * h] - lam * (acc_ref[2 * h + 1] / l_ref[2 * h + 1])
        o = o_t.T
        o = o * lax.rsqrt(jnp.mean(o * o, axis=-1, keepdims=True) + EPS) * sub_ref[...]
        o_ref[:, h * LANES:(h + 1) * LANES] = (o * (1.0 - lam_init)).astype(bf16)


def _diff_attn(qkv3, bias, da_lambda, da_subln, l, lam_init):
    bsz, s, _ = qkv3.shape
    tq = bias.shape[-1]
    nq = s // tq
    hw = DA_HEADS * LANES
    kern = functools.partial(_diff_attn_kernel, lam_init=lam_init, tq=tq)
    return pl.pallas_call(
        kern,
        grid=(bsz, nq),
        in_specs=[
            pl.BlockSpec((None, tq, hw), lambda b, i: (b, i, 0)),
            pl.BlockSpec((None, s, hw), lambda b, i: (b, 0, 1)),
            pl.BlockSpec((None, s, hw), lambda b, i: (b, 0, 2)),
            pl.BlockSpec((DA_HEADS, 3, tq, tq), lambda b, i: (0, 0, 0, 0)),
            pl.BlockSpec((None, 4, DA_HEAD_DIM), lambda b, i: (l, 0, 0)),
            pl.BlockSpec((None, 1, LANES), lambda b, i: (l, 0, 0)),
        ],
        out_specs=pl.BlockSpec((None, tq, hw), lambda b, i: (b, i, 0)),
        out_shape=jax.ShapeDtypeStruct((bsz, s, hw), bf16),
        scratch_shapes=[pltpu.VMEM((nq, hw, tq), bf16),
                        pltpu.VMEM((2 * DA_HEADS, 1, tq), f32), pltpu.VMEM((2 * DA_HEADS, 1, tq), f32),
                        pltpu.VMEM((2 * DA_HEADS, LANES, tq), f32)],
        compiler_params=_cparams(("arbitrary", "arbitrary")),
        name="diff_attn",
    )(qkv3, qkv3, qkv3, bias, da_lambda, da_subln)


def _mla_attn_kernel(q_ref, k_ref, v_ref, o_ref, vt_ref, m_ref, l_ref, acc_ref, *, tq):
    qi = pl.program_id(1)

    @pl.when(qi == 0)
    def _():
        _fill_vt(v_ref, vt_ref, tq)

    q_list = [q_ref[:, h * LANES:(h + 1) * LANES] for h in range(MLA_HEADS)]
    kcols = [h * LANES for h in range(MLA_HEADS)]
    vrows = [(h // 2) * LANES for h in range(MLA_HEADS)]
    _flash_t(q_list, k_ref, kcols, vt_ref, vrows, m_ref, l_ref, acc_ref, None, qi, tq)
    row = lax.broadcasted_iota(jnp.int32, (LANES, tq), 0)
    for p in range(MLA_HEADS // 2):
        o_t = jnp.where(row < MLA_V, acc_ref[2 * p] / l_ref[2 * p], acc_ref[2 * p + 1] / l_ref[2 * p + 1])
        o_ref[:, p * LANES:(p + 1) * LANES] = o_t.T.astype(bf16)


def _mla_attn(q3, k3, v3, tq):
    bsz, s, hq = q3.shape
    nq = s // tq
    hv = v3.shape[-1]
    kern = functools.partial(_mla_attn_kernel, tq=tq)
    return pl.pallas_call(
        kern,
        grid=(bsz, nq),
        in_specs=[
            pl.BlockSpec((None, tq, hq), lambda b, i: (b, i, 0)),
            pl.BlockSpec((None, s, hq), lambda b, i: (b, 0, 0)),
            pl.BlockSpec((None, s, hv), lambda b, i: (b, 0, 0)),
        ],
        out_specs=pl.BlockSpec((None, tq, hv), lambda b, i: (b, i, 0)),
        out_shape=jax.ShapeDtypeStruct((bsz, s, hv), bf16),
        scratch_shapes=[pltpu.VMEM((nq, hv, tq), bf16),
                        pltpu.VMEM((MLA_HEADS, 1, tq), f32), pltpu.VMEM((MLA_HEADS, 1, tq), f32),
                        pltpu.VMEM((MLA_HEADS, LANES, tq), f32)],
        compiler_params=_cparams(("arbitrary", "arbitrary")),
        name="mla_attn",
    )(q3, k3, v3)


def _split3(x):
    hi = x.astype(bf16)
    r1 = x - hi.astype(f32)
    mid = r1.astype(bf16)
    lo = (r1 - mid.astype(f32)).astype(bf16)
    return hi, mid, lo


def _pair_bcast(x, j, lane):
    return jnp.where(lane < SSM_HEAD_DIM, x[:, 2 * j:2 * j + 1], x[:, 2 * j + 1:2 * j + 2])


def _ssd_kernel(xbc_ref, z_ref, misc_ref, dtb_ref, alog_ref, dvec_ref, nw_ref, o_ref, st_ref, y_ref):
    c = pl.program_id(1)
    L = SSM_CHUNK
    G, N = SSM_GROUPS, SSM_STATE

    @pl.when(c == 0)
    def _():
        st_ref[...] = jnp.zeros(st_ref.shape, f32)

    xs = xbc_ref[:, :SSM_D_INNER]
    bm = xbc_ref[:, SSM_D_INNER:SSM_D_INNER + G * N]
    cm = xbc_ref[:, SSM_D_INNER + G * N:]
    zs, dvec, nw = z_ref[...], dvec_ref[...], nw_ref[...]

    lane = _lane_iota((L, LANES))
    lane1 = _lane_iota((1, LANES))
    dt = jax.nn.softplus(misc_ref[...] + dtb_ref[...])
    a_l = jnp.where(lane1 < SSM_HEADS, -jnp.exp(alog_ref[...]), 0.0)
    da = dt * a_l
    row = lax.broadcasted_iota(jnp.int32, (L, L), 0)
    col = lax.broadcasted_iota(jnp.int32, (L, L), 1)
    causal = col <= row
    tri = jnp.where(causal, 1.0, 0.0).astype(bf16)
    hi, mid, lo = _split3(da)
    cs = _dot(tri, hi) + _dot(tri, mid) + _dot(tri, lo)
    cs_t = cs.T
    cs_last = cs[L - 1:L, :]
    w_end = dt * jnp.exp(cs_last - cs)
    e_cs = jnp.exp(cs)
    cdec = jnp.exp(cs_last)

    for g in range(G):
        bm_g = bm[:, g * N:(g + 1) * N]
        cm_g = cm[:, g * N:(g + 1) * N].astype(bf16)
        cb = _dot_nt(cm_g, bm_g.astype(bf16))
        bm_t = bm_g.T.astype(bf16)
        for jj in range(2):
            j = 2 * g + jj
            sl = slice(j * LANES, (j + 1) * LANES)
            psl = slice(jj * LANES, (jj + 1) * LANES)
            xs_p = xs[:, sl]
            xdt = (xs_p * _pair_bcast(dt, j, lane)).astype(bf16)
            yd = []
            for hh in (2 * j, 2 * j + 1):
                seg = cs[:, hh:hh + 1] - cs_t[hh:hh + 1, :]
                decay = jnp.exp(jnp.where(causal, seg, -jnp.inf))
                yd.append(_dot((cb * decay).astype(bf16), xdt))
            y_diag = jnp.where(lane < SSM_HEAD_DIM, yd[0], yd[1])
            prev = st_ref[g, :, psl]
            y_off = _dot(cm_g, prev.astype(bf16)) * _pair_bcast(e_cs, j, lane)
            y_ref[:, sl] = y_diag + y_off + dvec[:, sl] * xs_p
            xw = (xs_p * _pair_bcast(w_end, j, lane)).astype(bf16)
            cd = jnp.where(lane1 < SSM_HEAD_DIM, cdec[:, 2 * j:2 * j + 1], cdec[:, 2 * j + 1:2 * j + 2])
            st_ref[g, :, psl] = prev * cd + _dot(bm_t, xw)

    gw = SSM_D_INNER // G
    for g in range(G):
        sl = slice(g * gw, (g + 1) * gw)
        yz = y_ref[:, sl] * zs[:, sl]
        yz = yz * lax.rsqrt(jnp.mean(yz * yz, axis=-1, keepdims=True) + EPS) * nw[:, sl]
        o_ref[:, sl] = yz.astype(bf16)


def _ssd(rest3, dt_bias, a_log, dvec, norm_w, l):
    bsz, s, _ = rest3.shape
    L = SSM_CHUNK
    nc = s // L
    misc_blk = (R_LAT + MLA_Q_RANK + MLA_KV_RANK) // LANES
    vec = lambda n: pl.BlockSpec((None, 1, n), lambda b, c: (l, 0, 0))
    return pl.pallas_call(
        _ssd_kernel,
        grid=(bsz, nc),
        in_specs=[
            pl.BlockSpec((None, L, SSM_CONV_DIM), lambda b, c: (b, c, R_XBC // SSM_CONV_DIM)),
            pl.BlockSpec((None, L, SSM_D_INNER), lambda b, c: (b, c, R_Z // SSM_D_INNER)),
            pl.BlockSpec((None, L, LANES), lambda b, c: (b, c, misc_blk)),
            vec(LANES), vec(LANES), vec(SSM_D_INNER), vec(SSM_D_INNER),
        ],
        out_specs=pl.BlockSpec((None, L, SSM_D_INNER), lambda b, c: (b, c, 0)),
        out_shape=jax.ShapeDtypeStruct((bsz, s, SSM_D_INNER), bf16),
        scratch_shapes=[
            pltpu.VMEM((SSM_GROUPS, SSM_STATE, 4 * SSM_HEAD_DIM), f32),
            pltpu.VMEM((L, SSM_D_INNER), f32),
        ],
        compiler_params=_cparams(("arbitrary", "arbitrary")),
        name="ssd",
    )(rest3, rest3, rest3, dt_bias, a_log, dvec, norm_w)


def _merge_kernel(ya_ref, yb_ref, ys_ref, ga_ref, gb_ref, gc_ref, x_ref, mod_ref,
                  wa_ref, wb_ref, wc_ref, wo_ref, o_ref):
    merged = (ga_ref[...] * _dot(ya_ref[...], wa_ref[...])
              + gb_ref[...] * _dot(yb_ref[...], wb_ref[...])
              + gc_ref[...] * _dot(ys_ref[...], wc_ref[...]))
    o_ref[...] = x_ref[...] + mod_ref[2:3, :] * _dot(merged.astype(bf16), wo_ref[...])


def _merge(ya, yb, ys, rest, x2, mod, wa, wb, wc, wo, l, rows_per_batch):
    t, d = x2.shape
    tm = min(512, rows_per_batch)
    tpb = rows_per_batch // tm
    gblk = R_GATES // d
    row = lambda n: pl.BlockSpec((tm, n), lambda i: (i, 0))
    wspec = lambda k: pl.BlockSpec((None, k, d), lambda i: (l, 0, 0))
    return pl.pallas_call(
        _merge_kernel,
        grid=(t // tm,),
        in_specs=[
            row(ya.shape[1]), row(yb.shape[1]), row(ys.shape[1]),
            pl.BlockSpec((tm, d), lambda i: (i, gblk)),
            pl.BlockSpec((tm, d), lambda i: (i, gblk + 1)),
            pl.BlockSpec((tm, d), lambda i: (i, gblk + 2)),
            row(d),
            pl.BlockSpec((None, None, 6, d), lambda i: (l, i // tpb, 0, 0)),
            wspec(ya.shape[1]), wspec(yb.shape[1]), wspec(ys.shape[1]), wspec(d),
        ],
        out_specs=row(d),
        out_shape=jax.ShapeDtypeStruct((t, d), f32),
        compiler_params=_cparams(("arbitrary",)),
        name="merge_out",
    )(ya, yb, ys, rest, rest, rest, x2, mod, wa, wb, wc, wo)


FFN_TF = 256
FFN_NF = FFN_DIM // FFN_TF


FFN_HALO = 16
FFN_RB = 128


def _ffn_kernel(x_ref, halo_ref, nw_ref, mod_ref, wup_in, cw_in, cb_in, wd_in,
                o_ref, wup_ref, cw_ref, cb_ref, wd_ref, h_ref, ua0_ref, ua1_ref, uv0_ref, uv1_ref, a0_ref, a1_ref,
                acc_ref):
    mi = pl.program_id(1)
    tm = x_ref.shape[0]
    nf = FFN_NF
    nrb = tm // FFN_RB
    H = FFN_HALO

    @pl.when((pl.program_id(0) == 0) & (mi == 0))
    def _():
        wup_ref[...] = wup_in[...]
        wd_ref[...] = wd_in[...]
        cw_ref[...] = cw_in[...]
        cb_ref[...] = cb_in[...]

    def norm_mod(x):
        y = x * lax.rsqrt(jnp.mean(x * x, axis=-1, keepdims=True) + EPS) * nw_ref[...]
        return (y * (1.0 + mod_ref[4:5, :]) + mod_ref[3:4, :]).astype(bf16)

    keep = jnp.where(mi > 0, 1.0, 0.0)
    h_ref[0:H, :] = norm_mod(halo_ref[...] * keep) * keep.astype(bf16)
    h_ref[H:, :] = norm_mod(x_ref[...])

    ua_refs, uv_refs, a_refs = (ua0_ref, ua1_ref), (uv0_ref, uv1_ref), (a0_ref, a1_ref)

    def up(f, slot, r):
        lo = 0 if r == 0 else H + r * FFN_RB
        hi = H + (r + 1) * FFN_RB
        h = h_ref[lo:hi, :]
        ua_refs[slot][lo:hi, :] = _dot(h, wup_ref[f])
        uv_refs[slot][lo:hi, :] = _dot(h, wup_ref[nf + f])

    def conv(u_ref, g, r):
        base = H + r * FFN_RB
        out = cb_ref[g] + cw_ref[g, FFN_CONV - 1:FFN_CONV, :] * u_ref[base:base + FFN_RB, :]
        for s in range(1, FFN_CONV):
            out = out + cw_ref[g, FFN_CONV - 1 - s:FFN_CONV - s, :] * u_ref[base - s:base - s + FFN_RB, :]
        return out

    def act(f, slot, r):
        rows = slice(r * FFN_RB, (r + 1) * FFN_RB)
        a_refs[slot][rows, :] = (_silu(conv(ua_refs[slot], f, r)) * conv(uv_refs[slot], nf + f, r)).astype(bf16)

    def down(f, slot, r, first=False):
        rows = slice(r * FFN_RB, (r + 1) * FFN_RB)
        res = _dot(a_refs[slot][rows, :], wd_ref[f])
        if first:
            acc_ref[rows, :] = res
        else:
            acc_ref[rows, :] += res

    def stage(f_down, f_up, f_act, slot_act, first=False):
        for r in range(nrb):
            if f_down is not None:
                down(f_down, 1 - slot_act, r, first)
            if f_up is not None:
                up(f_up, 1 - slot_act, r)
            if f_act is not None:
                act(f_act, slot_act, r)

    for r in range(nrb):
        up(0, 0, r)
    stage(None, 1, 0, 0)
    stage(0, 2, 1, 1, first=True)

    assert (nf - 3) % 2 == 0

    def body(i, carry):
        f = 2 + 2 * i
        stage(f - 1, f + 1, f, 0)
        stage(f, f + 2, f + 1, 1)
        return carry

    lax.fori_loop(0, (nf - 3) // 2, body, 0)
    last = (nf - 1) % 2
    stage(nf - 2, None, nf - 1, last)
    for r in range(nrb):
        down(nf - 1, last, r)
    o_ref[...] = x_ref[...] + mod_ref[5:6, :] * acc_ref[...]


def _ffn(x3, norm_w, mod, wup3, conv_w3, conv_b3, wdown3, l):
    bsz, s, d = x3.shape
    tm = min(512, s)
    tf, nf = FFN_TF, FFN_NF
    hb = tm // FFN_HALO
    resident = dict(pipeline_mode=pl.Buffered(1))
    return pl.pallas_call(
        _ffn_kernel,
        grid=(bsz, s // tm),
        in_specs=[
            pl.BlockSpec((None, tm, d), lambda b, m: (b, m, 0)),
            pl.BlockSpec((None, FFN_HALO, d), lambda b, m: (b, jnp.maximum(m * hb - 1, 0), 0)),
            pl.BlockSpec((None, 1, d), lambda b, m: (l, 0, 0)),
            pl.BlockSpec((None, None, 6, d), lambda b, m: (l, b, 0, 0)),
            pl.BlockSpec((None, 2 * nf, d, tf), lambda b, m: (l, 0, 0, 0), **resident),
            pl.BlockSpec((None, 2 * nf, FFN_CONV, tf), lambda b, m: (l, 0, 0, 0)),
            pl.BlockSpec((None, 2 * nf, 1, tf), lambda b, m: (l, 0, 0, 0)),
            pl.BlockSpec((None, nf, tf, d), lambda b, m: (l, 0, 0, 0), **resident),
        ],
        out_specs=pl.BlockSpec((None, tm, d), lambda b, m: (b, m, 0)),
        out_shape=jax.ShapeDtypeStruct((bsz, s, d), f32),
        scratch_shapes=[
            pltpu.VMEM((2 * nf, d, tf), bf16), pltpu.VMEM((2 * nf, FFN_CONV, tf), f32),
            pltpu.VMEM((2 * nf, 1, tf), f32), pltpu.VMEM((nf, tf, d), bf16),
            pltpu.VMEM((tm + FFN_HALO, d), bf16),
            pltpu.VMEM((tm + FFN_HALO, tf), f32), pltpu.VMEM((tm + FFN_HALO, tf), f32),
            pltpu.VMEM((tm + FFN_HALO, tf), f32), pltpu.VMEM((tm + FFN_HALO, tf), f32),
            pltpu.VMEM((tm, tf), bf16), pltpu.VMEM((tm, tf), bf16),
            pltpu.VMEM((tm, d), f32),
        ],
        compiler_params=_cparams(("arbitrary", "arbitrary")),
        name="ffn",
    )(x3, x3, norm_w, mod, wup3, conv_w3, conv_b3, wdown3)


def _pack_w_in(w_in):
    nl, d, _ = w_in.shape
    sizes = (512, 512, 512, MLA_Q_RANK, MLA_KV_RANK, MLA_ROPE, SSM_D_INNER, SSM_CONV_DIM, SSM_HEADS, 3 * D_MODEL)
    offs = np.concatenate([[0], np.cumsum(sizes)])
    qa, ka, va, cq, ckv, kr, z, xbc, dt, gates = [w_in[..., int(offs[i]):int(offs[i + 1])] for i in range(10)]
    zeros = lambda n: jnp.zeros((nl, d, n), w_in.dtype)
    misc = jnp.concatenate([dt, zeros(MISC_KR_LANE - SSM_HEADS), kr, zeros(LANES - MISC_KR_LANE - MLA_ROPE)], axis=-1)
    return jnp.concatenate([qa, ka, va, xbc, z, gates, cq, ckv, misc], axis=-1).astype(bf16)


def _pad_lanes(v, n):
    return jnp.pad(v, [(0, 0)] * (v.ndim - 1) + [(0, n - v.shape[-1])])


def kernel(x, c, positions, rel_bias, ada_w, ada_b, norm_mix, norm_ffn, w_in, da_q_norm, da_k_norm, da_lambda, da_subln, mla_q_a_norm, mla_kv_a_norm, mla_w_uq, mla_w_ukv, mla_q_norm, mla_k_norm, ssm_conv_w, ssm_conv_b, ssm_dt_bias, ssm_a_log, ssm_d, ssm_norm, w_branch_a, w_branch_b, w_branch_c, w_out, ffn_w_up, ffn_conv_w, ffn_conv_b, ffn_w_down):
    bsz, seq, d = x.shape
    nl = w_in.shape[0]
    t = bsz * seq
    tq = min(ATT_TQ, seq)

    w_in_p = _pack_w_in(w_in)
    qk_w = jnp.stack([jnp.tile(da_q_norm, (1, 2 * DA_HEADS)) * (DA_HEAD_DIM ** -0.5 * LOG2E),
                      jnp.tile(da_k_norm, (1, 2 * DA_HEADS))], axis=1).reshape(nl, 2, 1, IN_TN)
    blk = np.arange(IN_TN) // DA_HEAD_DIM
    bd = jnp.asarray(blk[:, None] == blk[None, :], bf16)
    wuq = _pad_lanes(mla_w_uq.reshape(nl, MLA_Q_RANK, MLA_HEADS, MLA_QK), LANES)
    wuq = wuq.reshape(nl, MLA_Q_RANK, MLA_HEADS * LANES).astype(bf16)
    wukv = mla_w_ukv.reshape(nl, MLA_KV_RANK, MLA_HEADS, MLA_NOPE + MLA_V)
    wuk = _pad_lanes(wukv[..., :MLA_NOPE], LANES).reshape(nl, MLA_KV_RANK, MLA_HEADS * LANES).astype(bf16)
    wuv = wukv[..., MLA_NOPE:].reshape(nl, MLA_KV_RANK, MLA_HEADS * MLA_V).astype(bf16)
    qn = (_pad_lanes(mla_q_norm, LANES) * (MLA_QK ** -0.5 * LOG2E)).reshape(nl, 1, LANES)
    kn = _pad_lanes(mla_k_norm, LANES).reshape(nl, 1, LANES)
    dt_bias = _pad_lanes(ssm_dt_bias, LANES).reshape(nl, 1, LANES)
    a_log = _pad_lanes(ssm_a_log, LANES).reshape(nl, 1, LANES)
    dvec = jnp.repeat(ssm_d, SSM_HEAD_DIM, axis=-1).reshape(nl, 1, SSM_D_INNER)
    v3 = lambda a: a.reshape(nl, 1, a.shape[-1])
    wa, wb, wc, wo = (w.astype(bf16) for w in (w_branch_a, w_branch_b, w_branch_c, w_out))
    nf2 = 2 * FFN_NF
    wup3 = ffn_w_up.astype(bf16).reshape(nl, d, nf2, FFN_TF).transpose(0, 2, 1, 3)
    wdown3 = ffn_w_down.astype(bf16).reshape(nl, FFN_NF, FFN_TF, d)
    fcw3 = ffn_conv_w.reshape(nl, FFN_CONV, nf2, FFN_TF).transpose(0, 2, 1, 3)
    fcb3 = ffn_conv_b.reshape(nl, nf2, 1, FFN_TF)

    mod = _ada_mod(c, ada_w, ada_b).reshape(nl, bsz, 6, d)
    ctab, sptab, smtab = _rope_tables(positions)
    bias = _bias_tiles(rel_bias, tq)

    x2 = x.reshape(t, d)
    for l in range(nl):
        lam_init = 0.8 - 0.6 * math.exp(-0.3 * l)
        qkv, rest = _inproj(x2, v3(norm_mix), mod, w_in_p, qk_w, bd, ssm_conv_w, v3(ssm_conv_b), l, seq)
        q_b, k_b, v_b = _mla_prep(rest, v3(mla_q_a_norm), v3(mla_kv_a_norm), wuq, wuk, wuv, qn, kn,
                                  ctab, sptab, smtab, l)
        ya = _diff_attn(qkv.reshape(bsz, seq, QKV_COLS), bias, da_lambda, v3(da_subln), l, lam_init)
        yb = _mla_attn(q_b.reshape(bsz, seq, -1), k_b.reshape(bsz, seq, -1), v_b.reshape(bsz, seq, -1), tq)
        ys = _ssd(rest.reshape(bsz, seq, REST_COLS), dt_bias, a_log, dvec, v3(ssm_norm), l)
        x2 = _merge(ya.reshape(t, -1), yb.reshape(t, -1), ys.reshape(t, -1), rest, x2, mod, wa, wb, wc, wo, l, seq)
        x2 = _ffn(x2.reshape(bsz, seq, d), v3(norm_ffn), mod, wup3, fcw3, fcb3, wdown3, l).reshape(t, d)
    return x2.reshape(bsz, seq, d)
```

```python
import functools
import math

import jax
import jax.numpy as jnp
import numpy as np
from jax import lax
from jax.experimental import pallas as pl
from jax.experimental.pallas import tpu as pltpu

f32 = jnp.float32
bf16 = jnp.bfloat16

D_MODEL = 1024
DA_HEADS = 4
DA_HEAD_DIM = 64
MLA_HEADS = 8
MLA_Q_RANK = 256
MLA_KV_RANK = 128
MLA_NOPE = 64
MLA_ROPE = 32
MLA_V = 64
MLA_QK = MLA_NOPE + MLA_ROPE
SSM_D_INNER = 1024
SSM_HEAD_DIM = 64
SSM_HEADS = 16
SSM_GROUPS = 4
SSM_STATE = 128
SSM_CONV = 4
SSM_CHUNK = 128
SSM_CONV_DIM = SSM_D_INNER + 2 * SSM_GROUPS * SSM_STATE
FFN_DIM = 2816
FFN_CONV = 3
REL_BUCKETS = 32
REL_MAX_DIST = 128
ROPE_THETA = 10000.0
EPS = 1e-6

LANES = 128
SUBLANES = 8
VMEM_LIMIT = 56 * 1024 * 1024

IN_TN = 512
QKV_COLS = 3 * 512
R_XBC = 0
R_Z = R_XBC + SSM_CONV_DIM
R_GATES = R_Z + SSM_D_INNER
R_LAT = R_GATES + 3 * D_MODEL
REST_COLS = R_LAT + 512
MISC_KR_LANE = 64
NEG_BIG = -1e30
LOG2E = math.log2(math.e)

ATT_TQ = 256


def _cparams(sem):
    return pltpu.CompilerParams(dimension_semantics=sem, vmem_limit_bytes=VMEM_LIMIT)


def _lane_iota(shape):
    return lax.broadcasted_iota(jnp.int32, shape, len(shape) - 1)


def _dot(a, b):
    return jnp.dot(a, b, preferred_element_type=f32)


def _dot_nt(a, b):
    return lax.dot_general(a, b, (((1,), (1,)), ((), ())), preferred_element_type=f32)


def _sigmoid(x):
    return 1.0 / (1.0 + jnp.exp(-x))


def _silu(x):
    return x * _sigmoid(x)


def _ada_kernel(c_ref, w_ref, b_ref, o_ref):
    ca = _silu(c_ref[...]).astype(bf16)
    o_ref[...] = _dot(ca, w_ref[...].astype(bf16)) + b_ref[...]


def _ada_mod(c, ada_w, ada_b):
    nl, d, n = ada_w.shape
    bsz = c.shape[0]
    tn = 1024
    return pl.pallas_call(
        _ada_kernel,
        grid=(nl, n // tn),
        in_specs=[
            pl.BlockSpec((bsz, d), lambda l, j: (0, 0)),
            pl.BlockSpec((None, d, tn), lambda l, j: (l, 0, j)),
            pl.BlockSpec((None, 1, tn), lambda l, j: (l, 0, j)),
        ],
        out_specs=pl.BlockSpec((None, bsz, tn), lambda l, j: (l, 0, j)),
        out_shape=jax.ShapeDtypeStruct((nl, bsz, n), f32),
        compiler_params=_cparams(("arbitrary", "arbitrary")),
        name="ada_mod",
    )(c, ada_w, ada_b.reshape(nl, 1, n))


def _rope_kernel(pos_ref, freq_ref, c_ref, sp_ref, sm_ref):
    ang = pos_ref[...].astype(f32) * freq_ref[...]
    cos, sin = jnp.cos(ang), jnp.sin(ang)
    lane = _lane_iota(ang.shape)
    lo, mid, hi = MISC_KR_LANE, MISC_KR_LANE + MLA_ROPE // 2, MISC_KR_LANE + MLA_ROPE
    c_ref[...] = jnp.where((lane >= lo) & (lane < hi), cos, 1.0)
    sp_ref[...] = jnp.where((lane >= mid) & (lane < hi), sin, 0.0)
    sm_ref[...] = jnp.where((lane >= lo) & (lane < mid), -sin, 0.0)


def _rope_tables(positions):
    t = positions.size
    tm = min(t, 2048)
    inv_freq = ROPE_THETA ** (-jnp.arange(0, MLA_ROPE, 2, dtype=f32) / MLA_ROPE)
    freq = jnp.zeros((LANES,), f32).at[MISC_KR_LANE:MISC_KR_LANE + MLA_ROPE].set(jnp.tile(inv_freq, 2))
    out = jax.ShapeDtypeStruct((t, LANES), f32)
    spec = pl.BlockSpec((tm, LANES), lambda i: (i, 0))
    return pl.pallas_call(
        _rope_kernel,
        grid=(t // tm,),
        in_specs=[pl.BlockSpec((tm, 1), lambda i: (i, 0)), pl.BlockSpec((1, LANES), lambda i: (0, 0))],
        out_specs=[spec, spec, spec],
        out_shape=[out, out, out],
        compiler_params=_cparams(("arbitrary",)),
        name="rope_tables",
    )(positions.reshape(t, 1), freq.reshape(1, LANES))


def _bias_kernel(rb_ref, o_ref):
    h, which = pl.program_id(0), pl.program_id(1)
    tq = o_ref.shape[-1]
    row = lax.broadcasted_iota(jnp.int32, (tq, tq), 0)
    col = lax.broadcasted_iota(jnp.int32, (tq, tq), 1)
    rel = col - row + which * tq
    d = jnp.maximum(rel, 0)
    max_exact = REL_BUCKETS // 2
    large = max_exact + (jnp.log(jnp.maximum(d, 1).astype(f32) / max_exact)
                         / math.log(REL_MAX_DIST / max_exact) * (REL_BUCKETS - max_exact)).astype(jnp.int32)
    large = jnp.minimum(large, REL_BUCKETS - 1)
    bucket = jnp.where(d < max_exact, d, large)
    val = jnp.full((tq, tq), rb_ref[0, h], f32)
    for k in range(1, REL_BUCKETS):
        val = jnp.where(bucket == k, rb_ref[k, h], val)
    val = (val - rb_ref[REL_BUCKETS - 1, h]) * LOG2E
    val = jnp.where(rel < 0, NEG_BIG, val)
    o_ref[...] = jnp.where(which >= 2, 0.0, val)


def _bias_tiles(rel_bias, tq):
    return pl.pallas_call(
        _bias_kernel,
        grid=(DA_HEADS, 3),
        in_specs=[pl.BlockSpec(memory_space=pltpu.SMEM)],
        out_specs=pl.BlockSpec((None, None, tq, tq), lambda h, w: (h, w, 0, 0)),
        out_shape=jax.ShapeDtypeStruct((DA_HEADS, 3, tq, tq), f32),
        compiler_params=_cparams(("arbitrary", "arbitrary")),
        name="bias_tiles",
    )(rel_bias)


def _inproj_kernel(x_ref, nw_ref, mod_ref, w_ref, qkw_ref, bd_ref, qkv_ref, rest_ref, h_ref):
    j = pl.program_id(1)

    @pl.when(j == 0)
    def _():
        x = x_ref[...]
        y = x * lax.rsqrt(jnp.mean(x * x, axis=-1, keepdims=True) + EPS) * nw_ref[...]
        h_ref[...] = (y * (1.0 + mod_ref[1:2, :]) + mod_ref[0:1, :]).astype(bf16)

    def proj():
        return _dot(h_ref[...], w_ref[...])

    @pl.when(j < 2)
    def _():
        acc = proj()
        ss = _dot((acc * acc).astype(bf16), bd_ref[...]) * (1.0 / DA_HEAD_DIM)
        qkv_ref[...] = (acc * lax.rsqrt(ss + EPS) * qkw_ref[...]).astype(bf16)

    @pl.when(j == 2)
    def _():
        qkv_ref[...] = proj().astype(bf16)

    g_lo = 3 + R_GATES // IN_TN
    g_hi = 3 + R_LAT // IN_TN

    @pl.when((j >= 3) & ((j < g_lo) | (j >= g_hi)))
    def _():
        rest_ref[...] = proj()

    @pl.when((j >= g_lo) & (j < g_hi))
    def _():
        rest_ref[...] = 0.5 * jnp.tanh(0.5 * proj()) + 0.5


def _inproj(x2, norm_w, mod, w_p, qk_w, bd, l, rows_per_batch):
    t, d = x2.shape
    tm = rows_per_batch
    nj = (QKV_COLS + REST_COLS) // IN_TN
    return pl.pallas_call(
        _inproj_kernel,
        grid=(t // tm, nj),
        in_specs=[
            pl.BlockSpec((tm, d), lambda i, j: (i, 0)),
            pl.BlockSpec((None, 1, d), lambda i, j: (l, 0, 0)),
            pl.BlockSpec((None, None, 6, d), lambda i, j: (l, i, 0, 0)),
            pl.BlockSpec((None, d, IN_TN), lambda i, j: (l, 0, j)),
            pl.BlockSpec((None, None, 1, IN_TN), lambda i, j: (l, jnp.minimum(j, 1), 0, 0)),
            pl.BlockSpec((IN_TN, IN_TN), lambda i, j: (0, 0)),
        ],
        out_specs=[
            pl.BlockSpec((tm, IN_TN), lambda i, j: (i, jnp.minimum(j, 2))),
            pl.BlockSpec((tm, IN_TN), lambda i, j: (i, jnp.maximum(j - 3, 0))),
        ],
        out_shape=[jax.ShapeDtypeStruct((t, QKV_COLS), bf16), jax.ShapeDtypeStruct((t, REST_COLS), f32)],
        scratch_shapes=[pltpu.VMEM((tm, d), bf16)],
        compiler_params=_cparams(("arbitrary", "arbitrary")),
        name="inproj",
    )(x2, norm_w, mod, w_p, qk_w, bd)


def _rope_apply(x, c, sp, sm):
    return x * c + pltpu.roll(x, 16, 1) * sp + pltpu.roll(x, LANES - 16, 1) * sm


def _mla_prep_kernel(lat_ref, qan_ref, kvan_ref, wuq_ref, wuk_ref, wuv_ref, qn_ref, kn_ref,
                     c_ref, sp_ref, sm_ref, q_ref, k_ref, v_ref):
    lat = lat_ref[...]
    c, sp, sm = c_ref[...], sp_ref[...], sm_ref[...]
    cq = lat[:, :MLA_Q_RANK]
    cqn = (cq * lax.rsqrt(jnp.mean(cq * cq, axis=-1, keepdims=True) + EPS) * qan_ref[...]).astype(bf16)
    ckv = lat[:, MLA_Q_RANK:MLA_Q_RANK + MLA_KV_RANK]
    ckvn = (ckv * lax.rsqrt(jnp.mean(ckv * ckv, axis=-1, keepdims=True) + EPS) * kvan_ref[...]).astype(bf16)
    misc = lat[:, MLA_Q_RANK + MLA_KV_RANK:]
    lane = _lane_iota(misc.shape)
    krb = jnp.where((lane >= MISC_KR_LANE) & (lane < MISC_KR_LANE + MLA_ROPE), misc, 0.0)
    qf = _dot(cqn, wuq_ref[...])
    kf = _dot(ckvn, wuk_ref[...])
    v_ref[...] = _dot(ckvn, wuv_ref[...]).astype(bf16)
    inv = 1.0 / MLA_QK
    qn, kn = qn_ref[...], kn_ref[...]
    for h in range(MLA_HEADS):
        sl = slice(h * LANES, (h + 1) * LANES)
        qh = qf[:, sl]
        qh = qh * lax.rsqrt(jnp.sum(qh * qh, axis=-1, keepdims=True) * inv + EPS) * qn
        q_ref[:, sl] = _rope_apply(qh, c, sp, sm).astype(bf16)
        kh = kf[:, sl] + krb
        kh = kh * lax.rsqrt(jnp.sum(kh * kh, axis=-1, keepdims=True) * inv + EPS) * kn
        k_ref[:, sl] = _rope_apply(kh, c, sp, sm).astype(bf16)


def _mla_prep(rest, qan, kvan, wuq, wuk, wuv, qn, kn, ctab, sptab, smtab, l):
    t = rest.shape[0]
    tm = min(512, t)
    lat_blk = R_LAT // 512
    hq = MLA_HEADS * LANES
    vec = lambda n: pl.BlockSpec((None, 1, n), lambda i: (l, 0, 0))
    tab = pl.BlockSpec((tm, LANES), lambda i: (i, 0))
    return pl.pallas_call(
        _mla_prep_kernel,
        grid=(t // tm,),
        in_specs=[
            pl.BlockSpec((tm, 512), lambda i: (i, lat_blk)),
            vec(MLA_Q_RANK), vec(MLA_KV_RANK),
            pl.BlockSpec((None, MLA_Q_RANK, hq), lambda i: (l, 0, 0)),
            pl.BlockSpec((None, MLA_KV_RANK, hq), lambda i: (l, 0, 0)),
            pl.BlockSpec((None, MLA_KV_RANK, MLA_HEADS * MLA_V), lambda i: (l, 0, 0)),
            vec(LANES), vec(LANES), tab, tab, tab,
        ],
        out_specs=[
            pl.BlockSpec((tm, hq), lambda i: (i, 0)),
            pl.BlockSpec((tm, hq), lambda i: (i, 0)),
            pl.BlockSpec((tm, MLA_HEADS * MLA_V), lambda i: (i, 0)),
        ],
        out_shape=[jax.ShapeDtypeStruct((t, hq), bf16), jax.ShapeDtypeStruct((t, hq), bf16),
                   jax.ShapeDtypeStruct((t, MLA_HEADS * MLA_V), bf16)],
        compiler_params=_cparams(("arbitrary",)),
        name="mla_prep",
    )(rest, qan, kvan, wuq, wuk, wuv, qn, kn, ctab, sptab, smtab)


def _flash_t(q_list, k_ref, kcols, vt_ref, vrows, m_ref, l_ref, acc_ref, bias_fn, qi, tq):
    nmap = len(q_list)
    m_ref[...] = jnp.full(m_ref.shape, -jnp.inf, f32)
    l_ref[...] = jnp.zeros(l_ref.shape, f32)
    acc_ref[...] = jnp.zeros(acc_ref.shape, f32)

    def step(kj, which, masked):
        start = pl.multiple_of(kj * tq, tq)
        if masked:
            krow = lax.broadcasted_iota(jnp.int32, (tq, tq), 0)
            qcol = lax.broadcasted_iota(jnp.int32, (tq, tq), 1)
            keep = krow <= qcol

        def qk(a):
            k = k_ref[pl.ds(start, tq), kcols[a]:kcols[a] + LANES]
            s = _dot_nt(k, q_list[a])
            if which is not None and bias_fn is not None:
                s = s + bias_fn(a, which)
            return s

        s_list = [qk(a) for a in range(nmap)]
        for a in range(nmap):
            s = s_list[a]
            if masked:
                s = jnp.where(keep, s, NEG_BIG)
            m_prev = m_ref[a]
            m_new = jnp.maximum(m_prev, jnp.max(s, axis=0, keepdims=True))
            alpha = jnp.exp2(m_prev - m_new)
            e = jnp.exp2(s - m_new)
            l_ref[a] = alpha * l_ref[a] + jnp.sum(e, axis=0, keepdims=True)
            m_ref[a] = m_new
            vt = vt_ref[kj, vrows[a]:vrows[a] + LANES, :]
            acc_ref[a] = alpha * acc_ref[a] + _dot(vt, e.astype(bf16))

    if bias_fn is None:
        def body(kj, carry):
            step(kj, None, False)
            return carry

        lax.fori_loop(0, qi, body, 0)
        step(qi, None, True)
    else:
        def body(kj, carry):
            step(kj, jnp.minimum(qi - kj, 2), False)
            return carry

        lax.fori_loop(0, qi + 1, body, 0)


def _fill_vt(v_ref, vt_ref, tq):
    for c in range(vt_ref.shape[0]):
        vt_ref[c] = v_ref[c * tq:(c + 1) * tq, :].astype(f32).T.astype(bf16)


def _diff_attn_kernel(q_ref, k_ref, v_ref, bias_ref, lp_ref, sub_ref, o_ref, vt_ref, m_ref, l_ref, acc_ref,
                      *, lam_init, tq):
    qi = pl.program_id(1)

    @pl.when(qi == 0)
    def _():
        _fill_vt(v_ref, vt_ref, tq)

    lane = _lane_iota((tq, LANES))
    q_list, kcols, vrows = [], [], []
    for h in range(DA_HEADS):
        q = q_ref[:, h * LANES:(h + 1) * LANES]
        zero = jnp.zeros_like(q)
        q_list += [jnp.where(lane < DA_HEAD_DIM, q, zero), jnp.where(lane >= DA_HEAD_DIM, q, zero)]
        kcols += [h * LANES, h * LANES]
        vrows += [h * LANES, h * LANES]
    _flash_t(q_list, k_ref, kcols, vt_ref, vrows, m_ref, l_ref, acc_ref,
             lambda a, which: bias_ref[a // 2, which], qi, tq)
    lp = lp_ref[...]
    lam = (jnp.exp(jnp.sum(lp[0:1] * lp[1:2], axis=-1, keepdims=True))
           - jnp.exp(jnp.sum(lp[2:3] * lp[3:4], axis=-1, keepdims=True)) + lam_init)
    for h in range(DA_HEADS):
        o_t = acc_ref[2 * h] / l_ref[2 * h] - lam * (acc_ref[2 * h + 1] / l_ref[2 * h + 1])
        o = o_t.T
        o = o * lax.rsqrt(jnp.mean(o * o, axis=-1, keepdims=True) + EPS) * sub_ref[...]
        o_ref[:, h * LANES:(h + 1) * LANES] = (o * (1.0 - lam_init)).astype(bf16)


def _diff_attn(qkv3, bias, da_lambda, da_subln, l, lam_init):
    bsz, s, _ = qkv3.shape
    tq = bias.shape[-1]
    nq = s // tq
    hw = DA_HEADS * LANES
    kern = functools.partial(_diff_attn_kernel, lam_init=lam_init, tq=tq)
    return pl.pallas_call(
        kern,
        grid=(bsz, nq),
        in_specs=[
            pl.BlockSpec((None, tq, hw), lambda b, i: (b, i, 0)),
            pl.BlockSpec((None, s, hw), lambda b, i: (b, 0, 1)),
            pl.BlockSpec((None, s, hw), lambda b, i: (b, 0, 2)),
            pl.BlockSpec((DA_HEADS, 3, tq, tq), lambda b, i: (0, 0, 0, 0)),
            pl.BlockSpec((None, 4, DA_HEAD_DIM), lambda b, i: (l, 0, 0)),
            pl.BlockSpec((None, 1, LANES), lambda b, i: (l, 0, 0)),
        ],
        out_specs=pl.BlockSpec((None, tq, hw), lambda b, i: (b, i, 0)),
        out_shape=jax.ShapeDtypeStruct((bsz, s, hw), bf16),
        scratch_shapes=[pltpu.VMEM((nq, hw, tq), bf16),
                        pltpu.VMEM((2 * DA_HEADS, 1, tq), f32), pltpu.VMEM((2 * DA_HEADS, 1, tq), f32),
                        pltpu.VMEM((2 * DA_HEADS, LANES, tq), f32)],
        compiler_params=_cparams(("arbitrary", "arbitrary")),
        name="diff_attn",
    )(qkv3, qkv3, qkv3, bias, da_lambda, da_subln)


def _mla_attn_kernel(q_ref, k_ref, v_ref, o_ref, vt_ref, m_ref, l_ref, acc_ref, *, tq):
    qi = pl.program_id(1)

    @pl.when(qi == 0)
    def _():
        _fill_vt(v_ref, vt_ref, tq)

    q_list = [q_ref[:, h * LANES:(h + 1) * LANES] for h in range(MLA_HEADS)]
    kcols = [h * LANES for h in range(MLA_HEADS)]
    vrows = [(h // 2) * LANES for h in range(MLA_HEADS)]
    _flash_t(q_list, k_ref, kcols, vt_ref, vrows, m_ref, l_ref, acc_ref, None, qi, tq)
    row = lax.broadcasted_iota(jnp.int32, (LANES, tq), 0)
    for p in range(MLA_HEADS // 2):
        o_t = jnp.where(row < MLA_V, acc_ref[2 * p] / l_ref[2 * p], acc_ref[2 * p + 1] / l_ref[2 * p + 1])
        o_ref[:, p * LANES:(p + 1) * LANES] = o_t.T.astype(bf16)


def _mla_attn(q3, k3, v3, tq):
    bsz, s, hq = q3.shape
    nq = s // tq
    hv = v3.shape[-1]
    kern = functools.partial(_mla_attn_kernel, tq=tq)
    return pl.pallas_call(
        kern,
        grid=(bsz, nq),
        in_specs=[
            pl.BlockSpec((None, tq, hq), lambda b, i: (b, i, 0)),
            pl.BlockSpec((None, s, hq), lambda b, i: (b, 0, 0)),
            pl.BlockSpec((None, s, hv), lambda b, i: (b, 0, 0)),
        ],
        out_specs=pl.BlockSpec((None, tq, hv), lambda b, i: (b, i, 0)),
        out_shape=jax.ShapeDtypeStruct((bsz, s, hv), bf16),
        scratch_shapes=[pltpu.VMEM((nq, hv, tq), bf16),
                        pltpu.VMEM((MLA_HEADS, 1, tq), f32), pltpu.VMEM((MLA_HEADS, 1, tq), f32),
                        pltpu.VMEM((MLA_HEADS, LANES, tq), f32)],
        compiler_params=_cparams(("arbitrary", "arbitrary")),
        name="mla_attn",
    )(q3, k3, v3)


def _split3(x):
    hi = x.astype(bf16)
    r1 = x - hi.astype(f32)
    mid = r1.astype(bf16)
    lo = (r1 - mid.astype(f32)).astype(bf16)
    return hi, mid, lo


def _pair_bcast(x, j, lane):
    return jnp.where(lane < SSM_HEAD_DIM, x[:, 2 * j:2 * j + 1], x[:, 2 * j + 1:2 * j + 2])


def _ssd_kernel(xbc_ref, z_ref, misc_ref, cw_ref, cb_ref, dtb_ref, alog_ref, dvec_ref, nw_ref,
                o_ref, ext_ref, st_ref, y_ref):
    c = pl.program_id(1)
    L = SSM_CHUNK
    G, N = SSM_GROUPS, SSM_STATE

    @pl.when(c == 0)
    def _():
        ext_ref[0:SUBLANES, :] = jnp.zeros((SUBLANES, SSM_CONV_DIM), f32)
        st_ref[...] = jnp.zeros(st_ref.shape, f32)

    zs, dvec, nw = _silu(z_ref[...]), dvec_ref[...], nw_ref[...]
    cw, cbias = cw_ref[...], cb_ref[...]
    dt_in = misc_ref[...] + dtb_ref[...]
    alog = alog_ref[...]

    ext_ref[SUBLANES:, :] = xbc_ref[...]
    conv = cbias + cw[SSM_CONV - 1:SSM_CONV, :] * ext_ref[SUBLANES:, :]
    for s in range(1, SSM_CONV):
        conv = conv + cw[SSM_CONV - 1 - s:SSM_CONV - s, :] * ext_ref[SUBLANES - s:SUBLANES - s + L, :]
    ext_ref[0:SUBLANES, :] = ext_ref[L:L + SUBLANES, :]
    xa = _silu(conv)
    xs = xa[:, :SSM_D_INNER]
    bm = xa[:, SSM_D_INNER:SSM_D_INNER + G * N]
    cm = xa[:, SSM_D_INNER + G * N:]

    lane = _lane_iota((L, LANES))
    lane1 = _lane_iota((1, LANES))
    dt = jax.nn.softplus(dt_in)
    a_l = jnp.where(lane1 < SSM_HEADS, -jnp.exp(alog), 0.0)
    da = dt * a_l
    row = lax.broadcasted_iota(jnp.int32, (L, L), 0)
    col = lax.broadcasted_iota(jnp.int32, (L, L), 1)
    causal = col <= row
    tri = jnp.where(causal, 1.0, 0.0).astype(bf16)
    hi, mid, lo = _split3(da)
    cs = _dot(tri, hi) + _dot(tri, mid) + _dot(tri, lo)
    cs_t = cs.T
    cs_last = cs[L - 1:L, :]
    w_end = dt * jnp.exp(cs_last - cs)
    e_cs = jnp.exp(cs)
    cdec = jnp.exp(cs_last)

    for g in range(G):
        bm_g = bm[:, g * N:(g + 1) * N]
        cm_g = cm[:, g * N:(g + 1) * N].astype(bf16)
        cb = _dot_nt(cm_g, bm_g.astype(bf16))
        bm_t = bm_g.T.astype(bf16)
        for jj in range(2):
            j = 2 * g + jj
            sl = slice(j * LANES, (j + 1) * LANES)
            psl = slice(jj * LANES, (jj + 1) * LANES)
            xs_p = xs[:, sl]
            xdt = (xs_p * _pair_bcast(dt, j, lane)).astype(bf16)
            yd = []
            for hh in (2 * j, 2 * j + 1):
                seg = cs[:, hh:hh + 1] - cs_t[hh:hh + 1, :]
                decay = jnp.exp(jnp.where(causal, seg, -jnp.inf))
                yd.append(_dot((cb * decay).astype(bf16), xdt))
            y_diag = jnp.where(lane < SSM_HEAD_DIM, yd[0], yd[1])
            prev = st_ref[g, :, psl]
            y_off = _dot(cm_g, prev.astype(bf16)) * _pair_bcast(e_cs, j, lane)
            y_ref[:, sl] = y_diag + y_off + dvec[:, sl] * xs_p
            xw = (xs_p * _pair_bcast(w_end, j, lane)).astype(bf16)
            cd = jnp.where(lane1 < SSM_HEAD_DIM, cdec[:, 2 * j:2 * j + 1], cdec[:, 2 * j + 1:2 * j + 2])
            st_ref[g, :, psl] = prev * cd + _dot(bm_t, xw)

    gw = SSM_D_INNER // G
    for g in range(G):
        sl = slice(g * gw, (g + 1) * gw)
        yz = y_ref[:, sl] * zs[:, sl]
        yz = yz * lax.rsqrt(jnp.mean(yz * yz, axis=-1, keepdims=True) + EPS) * nw[:, sl]
        o_ref[:, sl] = yz.astype(bf16)


def _ssd(rest3, conv_w, conv_b, dt_bias, a_log, dvec, norm_w, l):
    bsz, s, _ = rest3.shape
    L = SSM_CHUNK
    nc = s // L
    misc_blk = (R_LAT + MLA_Q_RANK + MLA_KV_RANK) // LANES
    vec = lambda n: pl.BlockSpec((None, 1, n), lambda b, c: (l, 0, 0))
    return pl.pallas_call(
        _ssd_kernel,
        grid=(bsz, nc),
        in_specs=[
            pl.BlockSpec((None, L, SSM_CONV_DIM), lambda b, c: (b, c, R_XBC // SSM_CONV_DIM)),
            pl.BlockSpec((None, L, SSM_D_INNER), lambda b, c: (b, c, R_Z // SSM_D_INNER)),
            pl.BlockSpec((None, L, LANES), lambda b, c: (b, c, misc_blk)),
            pl.BlockSpec((None, SSM_CONV, SSM_CONV_DIM), lambda b, c: (l, 0, 0)),
            vec(SSM_CONV_DIM), vec(LANES), vec(LANES), vec(SSM_D_INNER), vec(SSM_D_INNER),
        ],
        out_specs=pl.BlockSpec((None, L, SSM_D_INNER), lambda b, c: (b, c, 0)),
        out_shape=jax.ShapeDtypeStruct((bsz, s, SSM_D_INNER), bf16),
        scratch_shapes=[
            pltpu.VMEM((L + SUBLANES, SSM_CONV_DIM), f32),
            pltpu.VMEM((SSM_GROUPS, SSM_STATE, 4 * SSM_HEAD_DIM), f32),
            pltpu.VMEM((L, SSM_D_INNER), f32),
        ],
        compiler_params=_cparams(("arbitrary", "arbitrary")),
        name="ssd",
    )(rest3, rest3, rest3, conv_w, conv_b, dt_bias, a_log, dvec, norm_w)


def _merge_kernel(ya_ref, yb_ref, ys_ref, ga_ref, gb_ref, gc_ref, x_ref, mod_ref,
                  wa_ref, wb_ref, wc_ref, wo_ref, o_ref):
    merged = (ga_ref[...] * _dot(ya_ref[...], wa_ref[...])
              + gb_ref[...] * _dot(yb_ref[...], wb_ref[...])
              + gc_ref[...] * _dot(ys_ref[...], wc_ref[...]))
    o_ref[...] = x_ref[...] + mod_ref[2:3, :] * _dot(merged.astype(bf16), wo_ref[...])


def _merge(ya, yb, ys, rest, x2, mod, wa, wb, wc, wo, l, rows_per_batch):
    t, d = x2.shape
    tm = min(512, rows_per_batch)
    tpb = rows_per_batch // tm
    gblk = R_GATES // d
    row = lambda n: pl.BlockSpec((tm, n), lambda i: (i, 0))
    wspec = lambda k: pl.BlockSpec((None, k, d), lambda i: (l, 0, 0))
    return pl.pallas_call(
        _merge_kernel,
        grid=(t // tm,),
        in_specs=[
            row(ya.shape[1]), row(yb.shape[1]), row(ys.shape[1]),
            pl.BlockSpec((tm, d), lambda i: (i, gblk)),
            pl.BlockSpec((tm, d), lambda i: (i, gblk + 1)),
            pl.BlockSpec((tm, d), lambda i: (i, gblk + 2)),
            row(d),
            pl.BlockSpec((None, None, 6, d), lambda i: (l, i // tpb, 0, 0)),
            wspec(ya.shape[1]), wspec(yb.shape[1]), wspec(ys.shape[1]), wspec(d),
        ],
        out_specs=row(d),
        out_shape=jax.ShapeDtypeStruct((t, d), f32),
        compiler_params=_cparams(("arbitrary",)),
        name="merge_out",
    )(ya, yb, ys, rest, rest, rest, x2, mod, wa, wb, wc, wo)


FFN_TF = 256
FFN_NF = FFN_DIM // FFN_TF
FFN_RB = 512


def _ffn_kernel(x_ref, nw_ref, mod_ref, wa_ref, wv_ref, cwa_ref, cwv_ref, cba_ref, cbv_ref, wd_ref,
                o_ref, h_ref, ua_ref, uv_ref, acc_ref):
    fi = pl.program_id(1)
    tm = x_ref.shape[0]

    @pl.when(fi == 0)
    def _():
        x = x_ref[...]
        y = x * lax.rsqrt(jnp.mean(x * x, axis=-1, keepdims=True) + EPS) * nw_ref[...]
        h_ref[...] = (y * (1.0 + mod_ref[4:5, :]) + mod_ref[3:4, :]).astype(bf16)
        acc_ref[...] = jnp.zeros(acc_ref.shape, f32)
        ua_ref[0:SUBLANES, :] = jnp.zeros((SUBLANES, FFN_TF), f32)
        uv_ref[0:SUBLANES, :] = jnp.zeros((SUBLANES, FFN_TF), f32)

    wa, wv, wd = wa_ref[...], wv_ref[...], wd_ref[...]
    cwa, cwv, cba, cbv = cwa_ref[...], cwv_ref[...], cba_ref[...], cbv_ref[...]

    def conv(u_ref, cw, cb, base):
        out = cb + cw[FFN_CONV - 1:FFN_CONV, :] * u_ref[base:base + FFN_RB, :]
        for s in range(1, FFN_CONV):
            out = out + cw[FFN_CONV - 1 - s:FFN_CONV - s, :] * u_ref[base - s:base - s + FFN_RB, :]
        return out

    def up(rb):
        base = SUBLANES + rb * FFN_RB
        hb = h_ref[rb * FFN_RB:(rb + 1) * FFN_RB, :]
        ua_ref[base:base + FFN_RB, :] = _dot(hb, wa)
        uv_ref[base:base + FFN_RB, :] = _dot(hb, wv)

    nrb = tm // FFN_RB
    up(0)
    for rb in range(nrb):
        if rb + 1 < nrb:
            up(rb + 1)
        base = SUBLANES + rb * FFN_RB
        act = (_silu(conv(ua_ref, cwa, cba, base)) * conv(uv_ref, cwv, cbv, base)).astype(bf16)
        acc_ref[rb * FFN_RB:(rb + 1) * FFN_RB, :] += _dot(act, wd)

    @pl.when(fi == pl.num_programs(1) - 1)
    def _():
        o_ref[...] = x_ref[...] + mod_ref[5:6, :] * acc_ref[...]


def _ffn(x3, norm_w, mod, wup, conv_w, conv_b, wdown, l):
    bsz, s, d = x3.shape
    tf, nf = FFN_TF, FFN_NF
    return pl.pallas_call(
        _ffn_kernel,
        grid=(bsz, nf),
        in_specs=[
            pl.BlockSpec((None, s, d), lambda b, f: (b, 0, 0)),
            pl.BlockSpec((None, 1, d), lambda b, f: (l, 0, 0)),
            pl.BlockSpec((None, None, 6, d), lambda b, f: (l, b, 0, 0)),
            pl.BlockSpec((None, d, tf), lambda b, f: (l, 0, f)),
            pl.BlockSpec((None, d, tf), lambda b, f: (l, 0, nf + f)),
            pl.BlockSpec((None, FFN_CONV, tf), lambda b, f: (l, 0, f)),
            pl.BlockSpec((None, FFN_CONV, tf), lambda b, f: (l, 0, nf + f)),
            pl.BlockSpec((None, 1, tf), lambda b, f: (l, 0, f)),
            pl.BlockSpec((None, 1, tf), lambda b, f: (l, 0, nf + f)),
            pl.BlockSpec((None, tf, d), lambda b, f: (l, f, 0)),
        ],
        out_specs=pl.BlockSpec((None, s, d), lambda b, f: (b, 0, 0)),
        out_shape=jax.ShapeDtypeStruct((bsz, s, d), f32),
        scratch_shapes=[
            pltpu.VMEM((s, d), bf16),
            pltpu.VMEM((s + SUBLANES, tf), f32), pltpu.VMEM((s + SUBLANES, tf), f32),
            pltpu.VMEM((s, d), f32),
        ],
        compiler_params=_cparams(("arbitrary", "arbitrary")),
        name="ffn",
    )(x3, norm_w, mod, wup, wup, conv_w, conv_w, conv_b, conv_b, wdown)


def _pack_w_in(w_in):
    nl, d, _ = w_in.shape
    sizes = (512, 512, 512, MLA_Q_RANK, MLA_KV_RANK, MLA_ROPE, SSM_D_INNER, SSM_CONV_DIM, SSM_HEADS, 3 * D_MODEL)
    offs = np.concatenate([[0], np.cumsum(sizes)])
    qa, ka, va, cq, ckv, kr, z, xbc, dt, gates = [w_in[..., int(offs[i]):int(offs[i + 1])] for i in range(10)]
    zeros = lambda n: jnp.zeros((nl, d, n), w_in.dtype)
    misc = jnp.concatenate([dt, zeros(MISC_KR_LANE - SSM_HEADS), kr, zeros(LANES - MISC_KR_LANE - MLA_ROPE)], axis=-1)
    return jnp.concatenate([qa, ka, va, xbc, z, gates, cq, ckv, misc], axis=-1).astype(bf16)


def _pad_lanes(v, n):
    return jnp.pad(v, [(0, 0)] * (v.ndim - 1) + [(0, n - v.shape[-1])])


def kernel(x, c, positions, rel_bias, ada_w, ada_b, norm_mix, norm_ffn, w_in, da_q_norm, da_k_norm, da_lambda, da_subln, mla_q_a_norm, mla_kv_a_norm, mla_w_uq, mla_w_ukv, mla_q_norm, mla_k_norm, ssm_conv_w, ssm_conv_b, ssm_dt_bias, ssm_a_log, ssm_d, ssm_norm, w_branch_a, w_branch_b, w_branch_c, w_out, ffn_w_up, ffn_conv_w, ffn_conv_b, ffn_w_down):
    bsz, seq, d = x.shape
    nl = w_in.shape[0]
    t = bsz * seq
    tq = min(ATT_TQ, seq)

    w_in_p = _pack_w_in(w_in)
    qk_w = jnp.stack([jnp.tile(da_q_norm, (1, 2 * DA_HEADS)) * (DA_HEAD_DIM ** -0.5 * LOG2E),
                      jnp.tile(da_k_norm, (1, 2 * DA_HEADS))], axis=1).reshape(nl, 2, 1, IN_TN)
    blk = np.arange(IN_TN) // DA_HEAD_DIM
    bd = jnp.asarray(blk[:, None] == blk[None, :], bf16)
    wuq = _pad_lanes(mla_w_uq.reshape(nl, MLA_Q_RANK, MLA_HEADS, MLA_QK), LANES)
    wuq = wuq.reshape(nl, MLA_Q_RANK, MLA_HEADS * LANES).astype(bf16)
    wukv = mla_w_ukv.reshape(nl, MLA_KV_RANK, MLA_HEADS, MLA_NOPE + MLA_V)
    wuk = _pad_lanes(wukv[..., :MLA_NOPE], LANES).reshape(nl, MLA_KV_RANK, MLA_HEADS * LANES).astype(bf16)
    wuv = wukv[..., MLA_NOPE:].reshape(nl, MLA_KV_RANK, MLA_HEADS * MLA_V).astype(bf16)
    qn = (_pad_lanes(mla_q_norm, LANES) * (MLA_QK ** -0.5 * LOG2E)).reshape(nl, 1, LANES)
    kn = _pad_lanes(mla_k_norm, LANES).reshape(nl, 1, LANES)
    dt_bias = _pad_lanes(ssm_dt_bias, LANES).reshape(nl, 1, LANES)
    a_log = _pad_lanes(ssm_a_log, LANES).reshape(nl, 1, LANES)
    dvec = jnp.repeat(ssm_d, SSM_HEAD_DIM, axis=-1).reshape(nl, 1, SSM_D_INNER)
    v3 = lambda a: a.reshape(nl, 1, a.shape[-1])
    wa, wb, wc, wo = (w.astype(bf16) for w in (w_branch_a, w_branch_b, w_branch_c, w_out))
    wup, wdown = ffn_w_up.astype(bf16), ffn_w_down.astype(bf16)

    mod = _ada_mod(c, ada_w, ada_b).reshape(nl, bsz, 6, d)
    ctab, sptab, smtab = _rope_tables(positions)
    bias = _bias_tiles(rel_bias, tq)

    x2 = x.reshape(t, d)
    for l in range(nl):
        lam_init = 0.8 - 0.6 * math.exp(-0.3 * l)
        qkv, rest = _inproj(x2, v3(norm_mix), mod, w_in_p, qk_w, bd, l, seq)
        q_b, k_b, v_b = _mla_prep(rest, v3(mla_q_a_norm), v3(mla_kv_a_norm), wuq, wuk, wuv, qn, kn,
                                  ctab, sptab, smtab, l)
        ya = _diff_attn(qkv.reshape(bsz, seq, QKV_COLS), bias, da_lambda, v3(da_subln), l, lam_init)
        yb = _mla_attn(q_b.reshape(bsz, seq, -1), k_b.reshape(bsz, seq, -1), v_b.reshape(bsz, seq, -1), tq)
        ys = _ssd(rest.reshape(bsz, seq, REST_COLS), ssm_conv_w, v3(ssm_conv_b), dt_bias, a_log, dvec,
                  v3(ssm_norm), l)
        x2 = _merge(ya.reshape(t, -1), yb.reshape(t, -1), ys.reshape(t, -1), rest, x2, mod, wa, wb, wc, wo, l, seq)
        x2 = _ffn(x2.reshape(bsz, seq, d), v3(norm_ffn), mod, wup, ffn_conv_w, v3(ffn_conv_b), wdown, l).reshape(t, d)
    return x2.reshape(bsz, seq, d)
```

```python
import functools
import math

import jax
import jax.numpy as jnp
import numpy as np
from jax import lax
from jax.experimental import pallas as pl
from jax.experimental.pallas import tpu as pltpu

f32 = jnp.float32
bf16 = jnp.bfloat16

D_MODEL = 1024
DA_HEADS = 4
DA_HEAD_DIM = 64
MLA_HEADS = 8
MLA_Q_RANK = 256
MLA_KV_RANK = 128
MLA_NOPE = 64
MLA_ROPE = 32
MLA_V = 64
MLA_QK = MLA_NOPE + MLA_ROPE
SSM_D_INNER = 1024
SSM_HEAD_DIM = 64
SSM_HEADS = 16
SSM_GROUPS = 4
SSM_STATE = 128
SSM_CONV = 4
SSM_CHUNK = 128
SSM_CONV_DIM = SSM_D_INNER + 2 * SSM_GROUPS * SSM_STATE
FFN_DIM = 2816
FFN_CONV = 3
REL_BUCKETS = 32
REL_MAX_DIST = 128
ROPE_THETA = 10000.0
EPS = 1e-6

LANES = 128
SUBLANES = 8
VMEM_LIMIT = 56 * 1024 * 1024

IN_TN = 512
QKV_COLS = 3 * 512
R_XBC = 0
R_Z = R_XBC + SSM_CONV_DIM
R_GATES = R_Z + SSM_D_INNER
R_LAT = R_GATES + 3 * D_MODEL
REST_COLS = R_LAT + 512
ROPE_X1 = 32
ROPE_X2 = ROPE_X1 + LANES // 2
HALF_ROPE = MLA_ROPE // 2
LANE_OF_DIM = np.concatenate([np.arange(0, 32), np.arange(48, 80),
                              np.arange(ROPE_X1, ROPE_X1 + HALF_ROPE), np.arange(ROPE_X2, ROPE_X2 + HALF_ROPE)])
NEG_BIG = -1e30
LOG2E = math.log2(math.e)

ATT_TQ = 256


def _cparams(sem):
    return pltpu.CompilerParams(dimension_semantics=sem, vmem_limit_bytes=VMEM_LIMIT)


def _lane_iota(shape):
    return lax.broadcasted_iota(jnp.int32, shape, len(shape) - 1)


def _dot(a, b):
    return jnp.dot(a, b, preferred_element_type=f32)


def _dot_nt(a, b):
    return lax.dot_general(a, b, (((1,), (1,)), ((), ())), preferred_element_type=f32)


def _sigmoid(x):
    return 1.0 / (1.0 + jnp.exp(-x))


def _silu(x):
    return x * _sigmoid(x)


def _ada_kernel(c_ref, w_ref, b_ref, o_ref):
    ca = _silu(c_ref[...]).astype(bf16)
    o_ref[...] = _dot(ca, w_ref[...].astype(bf16)) + b_ref[...]


def _ada_mod(c, ada_w, ada_b):
    nl, d, n = ada_w.shape
    bsz = c.shape[0]
    tn = 1024
    return pl.pallas_call(
        _ada_kernel,
        grid=(nl, n // tn),
        in_specs=[
            pl.BlockSpec((bsz, d), lambda l, j: (0, 0)),
            pl.BlockSpec((None, d, tn), lambda l, j: (l, 0, j)),
            pl.BlockSpec((None, 1, tn), lambda l, j: (l, 0, j)),
        ],
        out_specs=pl.BlockSpec((None, bsz, tn), lambda l, j: (l, 0, j)),
        out_shape=jax.ShapeDtypeStruct((nl, bsz, n), f32),
        compiler_params=_cparams(("arbitrary", "arbitrary")),
        name="ada_mod",
    )(c, ada_w, ada_b.reshape(nl, 1, n))


def _rope_kernel(pos_ref, freq_ref, c_ref, s_ref):
    ang = pos_ref[...].astype(f32) * freq_ref[...]
    cos, sin = jnp.cos(ang), jnp.sin(ang)
    lane = _lane_iota(ang.shape)
    first = (lane >= ROPE_X1) & (lane < ROPE_X1 + HALF_ROPE)
    second = (lane >= ROPE_X2) & (lane < ROPE_X2 + HALF_ROPE)
    c_ref[...] = jnp.where(first | second, cos, 1.0)
    s_ref[...] = jnp.where(first, -sin, jnp.where(second, sin, 0.0))


def _rope_tables(positions):
    t = positions.size
    tm = min(t, 2048)
    inv_freq = ROPE_THETA ** (-jnp.arange(0, MLA_ROPE, 2, dtype=f32) / MLA_ROPE)
    freq = jnp.zeros((LANES,), f32).at[LANE_OF_DIM[MLA_NOPE:]].set(jnp.tile(inv_freq, 2))
    out = jax.ShapeDtypeStruct((t, LANES), f32)
    spec = pl.BlockSpec((tm, LANES), lambda i: (i, 0))
    return pl.pallas_call(
        _rope_kernel,
        grid=(t // tm,),
        in_specs=[pl.BlockSpec((tm, 1), lambda i: (i, 0)), pl.BlockSpec((1, LANES), lambda i: (0, 0))],
        out_specs=[spec, spec],
        out_shape=[out, out],
        compiler_params=_cparams(("arbitrary",)),
        name="rope_tables",
    )(positions.reshape(t, 1), freq.reshape(1, LANES))


def _bias_kernel(rb_ref, o_ref):
    h, which = pl.program_id(0), pl.program_id(1)
    tq = o_ref.shape[-1]
    row = lax.broadcasted_iota(jnp.int32, (tq, tq), 0)
    col = lax.broadcasted_iota(jnp.int32, (tq, tq), 1)
    rel = col - row + which * tq
    d = jnp.maximum(rel, 0)
    max_exact = REL_BUCKETS // 2
    large = max_exact + (jnp.log(jnp.maximum(d, 1).astype(f32) / max_exact)
                         / math.log(REL_MAX_DIST / max_exact) * (REL_BUCKETS - max_exact)).astype(jnp.int32)
    large = jnp.minimum(large, REL_BUCKETS - 1)
    bucket = jnp.where(d < max_exact, d, large)
    val = jnp.full((tq, tq), rb_ref[0, h], f32)
    for k in range(1, REL_BUCKETS):
        val = jnp.where(bucket == k, rb_ref[k, h], val)
    val = (val - rb_ref[REL_BUCKETS - 1, h]) * LOG2E
    val = jnp.where(rel < 0, NEG_BIG, val)
    o_ref[...] = jnp.where(which >= 2, 0.0, val)


def _bias_tiles(rel_bias, tq):
    return pl.pallas_call(
        _bias_kernel,
        grid=(DA_HEADS, 3),
        in_specs=[pl.BlockSpec(memory_space=pltpu.SMEM)],
        out_specs=pl.BlockSpec((None, None, tq, tq), lambda h, w: (h, w, 0, 0)),
        out_shape=jax.ShapeDtypeStruct((DA_HEADS, 3, tq, tq), f32),
        compiler_params=_cparams(("arbitrary", "arbitrary")),
        name="bias_tiles",
    )(rel_bias)


def _inproj_kernel(x_ref, nw_ref, mod_ref, w_ref, qkw_ref, bd_ref, qkv_ref, rest_ref, h_ref):
    j = pl.program_id(1)

    @pl.when(j == 0)
    def _():
        x = x_ref[...]
        y = x * lax.rsqrt(jnp.mean(x * x, axis=-1, keepdims=True) + EPS) * nw_ref[...]
        h_ref[...] = (y * (1.0 + mod_ref[1:2, :]) + mod_ref[0:1, :]).astype(bf16)

    def proj():
        return _dot(h_ref[...], w_ref[...])

    @pl.when(j < 2)
    def _():
        acc = proj()
        ss = _dot((acc * acc).astype(bf16), bd_ref[...]) * (1.0 / DA_HEAD_DIM)
        qkv_ref[...] = (acc * lax.rsqrt(ss + EPS) * qkw_ref[...]).astype(bf16)

    @pl.when(j == 2)
    def _():
        qkv_ref[...] = proj().astype(bf16)

    g_lo = 3 + R_GATES // IN_TN
    g_hi = 3 + R_LAT // IN_TN

    @pl.when((j >= 3) & ((j < g_lo) | (j >= g_hi)))
    def _():
        rest_ref[...] = proj()

    @pl.when((j >= g_lo) & (j < g_hi))
    def _():
        rest_ref[...] = 0.5 * jnp.tanh(0.5 * proj()) + 0.5


def _inproj(x2, norm_w, mod, w_p, qk_w, bd, l, rows_per_batch):
    t, d = x2.shape
    tm = rows_per_batch
    nj = (QKV_COLS + REST_COLS) // IN_TN
    return pl.pallas_call(
        _inproj_kernel,
        grid=(t // tm, nj),
        in_specs=[
            pl.BlockSpec((tm, d), lambda i, j: (i, 0)),
            pl.BlockSpec((None, 1, d), lambda i, j: (l, 0, 0)),
            pl.BlockSpec((None, None, 6, d), lambda i, j: (l, i, 0, 0)),
            pl.BlockSpec((None, d, IN_TN), lambda i, j: (l, 0, j)),
            pl.BlockSpec((None, None, 1, IN_TN), lambda i, j: (l, jnp.minimum(j, 1), 0, 0)),
            pl.BlockSpec((IN_TN, IN_TN), lambda i, j: (0, 0)),
        ],
        out_specs=[
            pl.BlockSpec((tm, IN_TN), lambda i, j: (i, jnp.minimum(j, 2))),
            pl.BlockSpec((tm, IN_TN), lambda i, j: (i, jnp.maximum(j - 3, 0))),
        ],
        out_shape=[jax.ShapeDtypeStruct((t, QKV_COLS), bf16), jax.ShapeDtypeStruct((t, REST_COLS), f32)],
        scratch_shapes=[pltpu.VMEM((tm, d), bf16)],
        compiler_params=_cparams(("arbitrary", "arbitrary")),
        name="inproj",
    )(x2, norm_w, mod, w_p, qk_w, bd)


def _rope_apply(x, c, s):
    return x * c + pltpu.roll(x, LANES // 2, 1) * s


def _mla_prep_kernel(lat_ref, qan_ref, kvan_ref, wuq_ref, wuk_ref, wuv_ref, qn_ref, kn_ref,
                     c_ref, s_ref, q_ref, k_ref, v_ref):
    lat = lat_ref[...]
    c, s = c_ref[...], s_ref[...]
    cq = lat[:, :MLA_Q_RANK]
    cqn = (cq * lax.rsqrt(jnp.mean(cq * cq, axis=-1, keepdims=True) + EPS) * qan_ref[...]).astype(bf16)
    ckv = lat[:, MLA_Q_RANK:MLA_Q_RANK + MLA_KV_RANK]
    ckvn = (ckv * lax.rsqrt(jnp.mean(ckv * ckv, axis=-1, keepdims=True) + EPS) * kvan_ref[...]).astype(bf16)
    misc = lat[:, MLA_Q_RANK + MLA_KV_RANK:]
    lane = _lane_iota(misc.shape)
    in_rope = (((lane >= ROPE_X1) & (lane < ROPE_X1 + HALF_ROPE))
               | ((lane >= ROPE_X2) & (lane < ROPE_X2 + HALF_ROPE)))
    krb = jnp.where(in_rope, misc, 0.0)
    qf = _dot(cqn, wuq_ref[...])
    kf = _dot(ckvn, wuk_ref[...])
    v_ref[...] = _dot(ckvn, wuv_ref[...]).astype(bf16)
    inv = 1.0 / MLA_QK
    qn, kn = qn_ref[...], kn_ref[...]
    for h in range(MLA_HEADS):
        sl = slice(h * LANES, (h + 1) * LANES)
        qh = qf[:, sl]
        qh = qh * lax.rsqrt(jnp.sum(qh * qh, axis=-1, keepdims=True) * inv + EPS) * qn
        q_ref[:, sl] = _rope_apply(qh, c, s).astype(bf16)
        kh = kf[:, sl] + krb
        kh = kh * lax.rsqrt(jnp.sum(kh * kh, axis=-1, keepdims=True) * inv + EPS) * kn
        k_ref[:, sl] = _rope_apply(kh, c, s).astype(bf16)


def _mla_prep(rest, qan, kvan, wuq, wuk, wuv, qn, kn, ctab, stab, l):
    t = rest.shape[0]
    tm = min(512, t)
    lat_blk = R_LAT // 512
    hq = MLA_HEADS * LANES
    vec = lambda n: pl.BlockSpec((None, 1, n), lambda i: (l, 0, 0))
    tab = pl.BlockSpec((tm, LANES), lambda i: (i, 0))
    return pl.pallas_call(
        _mla_prep_kernel,
        grid=(t // tm,),
        in_specs=[
            pl.BlockSpec((tm, 512), lambda i: (i, lat_blk)),
            vec(MLA_Q_RANK), vec(MLA_KV_RANK),
            pl.BlockSpec((None, MLA_Q_RANK, hq), lambda i: (l, 0, 0)),
            pl.BlockSpec((None, MLA_KV_RANK, hq), lambda i: (l, 0, 0)),
            pl.BlockSpec((None, MLA_KV_RANK, MLA_HEADS * MLA_V), lambda i: (l, 0, 0)),
            vec(LANES), vec(LANES), tab, tab,
        ],
        out_specs=[
            pl.BlockSpec((tm, hq), lambda i: (i, 0)),
            pl.BlockSpec((tm, hq), lambda i: (i, 0)),
            pl.BlockSpec((tm, MLA_HEADS * MLA_V), lambda i: (i, 0)),
        ],
        out_shape=[jax.ShapeDtypeStruct((t, hq), bf16), jax.ShapeDtypeStruct((t, hq), bf16),
                   jax.ShapeDtypeStruct((t, MLA_HEADS * MLA_V), bf16)],
        compiler_params=_cparams(("arbitrary",)),
        name="mla_prep",
    )(rest, qan, kvan, wuq, wuk, wuv, qn, kn, ctab, stab)


def _flash_t(q_list, k_ref, kcols, vt_ref, vrows, tile_fn, ntiles, qi, tq,
             m_ref, l_ref, acc_ref, s_refs, e_refs, al_refs):
    nmap = len(q_list)
    m_ref[...] = jnp.full(m_ref.shape, -jnp.inf, f32)
    l_ref[...] = jnp.zeros(l_ref.shape, f32)
    acc_ref[...] = jnp.zeros(acc_ref.shape, f32)
    last = qi

    def qk(t, slot):
        start = pl.multiple_of(t * tq, tq)
        w = jnp.minimum(qi - t, ntiles - 1)
        vals = []
        for a in range(nmap):
            k = k_ref[pl.ds(start, tq), kcols[a]:kcols[a] + LANES]
            vals.append(_dot_nt(k, q_list[a]) + tile_fn(a, w))
        for a in range(nmap):
            s_refs[slot][a] = vals[a]

    def sm(slot):
        for a in range(nmap):
            s = s_refs[slot][a]
            m_prev = m_ref[a]
            m_new = jnp.maximum(m_prev, jnp.max(s, axis=0, keepdims=True))
            alpha = jnp.exp2(m_prev - m_new)
            e = jnp.exp2(s - m_new)
            l_ref[a] = alpha * l_ref[a] + jnp.sum(e, axis=0, keepdims=True)
            m_ref[a] = m_new
            e_refs[slot][a] = e.astype(bf16)
            al_refs[slot][a] = alpha

    def pv(t, slot):
        for a in range(nmap):
            vt = vt_ref[t, vrows[a]:vrows[a] + LANES, :]
            acc_ref[a] = al_refs[slot][a] * acc_ref[a] + _dot(vt, e_refs[slot][a])

    def body(t, cur):
        qk(jnp.minimum(t + 1, last), 1 - cur)
        pv(t - 1, 1 - cur)
        sm(cur)

    qk(0, 0)
    qk(jnp.minimum(1, last), 1)
    sm(0)

    def pair(i, carry):
        t = 2 * i + 1
        body(t, 1)
        body(t + 1, 0)
        return carry

    lax.fori_loop(0, last // 2, pair, 0)
    odd = lax.rem(last, 2) == 1

    @pl.when(odd)
    def _():
        body(last, 1)
        pv(last, 1)

    @pl.when(jnp.logical_not(odd))
    def _():
        pv(last, 0)


def _fill_vt(v_ref, vt_ref, tq):
    for c in range(vt_ref.shape[0]):
        vt_ref[c] = v_ref[c * tq:(c + 1) * tq, :].astype(f32).T.astype(bf16)


def _attn_scratch(nmap, nq, hv, tq):
    sbuf = pltpu.VMEM((nmap, tq, tq), f32)
    ebuf = pltpu.VMEM((nmap, tq, tq), bf16)
    row = pltpu.VMEM((nmap, 1, tq), f32)
    return [pltpu.VMEM((nq, hv, tq), bf16), row, row, pltpu.VMEM((nmap, LANES, tq), f32),
            sbuf, sbuf, ebuf, ebuf, row, row]


def _diff_attn_kernel(q_ref, k_ref, v_ref, bias_ref, lp_ref, sub_ref, o_ref,
                      vt_ref, m_ref, l_ref, acc_ref, s0_ref, s1_ref, e0_ref, e1_ref, al0_ref, al1_ref,
                      *, lam_init, tq):
    qi = pl.program_id(1)

    @pl.when(qi == 0)
    def _():
        _fill_vt(v_ref, vt_ref, tq)

    lane = _lane_iota((tq, LANES))
    q_list, kcols, vrows = [], [], []
    for h in range(DA_HEADS):
        q = q_ref[:, h * LANES:(h + 1) * LANES]
        zero = jnp.zeros_like(q)
        q_list += [jnp.where(lane < DA_HEAD_DIM, q, zero), jnp.where(lane >= DA_HEAD_DIM, q, zero)]
        kcols += [h * LANES, h * LANES]
        vrows += [h * LANES, h * LANES]
    lp, sub = lp_ref[...], sub_ref[...]
    _flash_t(q_list, k_ref, kcols, vt_ref, vrows, lambda a, w: bias_ref[a // 2, w], 3, qi, tq,
             m_ref, l_ref, acc_ref, (s0_ref, s1_ref), (e0_ref, e1_ref), (al0_ref, al1_ref))
    lam = (jnp.exp(jnp.sum(lp[0:1] * lp[1:2], axis=-1, keepdims=True))
           - jnp.exp(jnp.sum(lp[2:3] * lp[3:4], axis=-1, keepdims=True)) + lam_init)
    for h in range(DA_HEADS):
        o_t = acc_ref[2 * h] / l_ref[2 * h] - lam * (acc_ref[2 * h + 1] / l_ref[2 * h + 1])
        o = o_t.T
        o = o * lax.rsqrt(jnp.mean(o * o, axis=-1, keepdims=True) + EPS) * sub
        o_ref[:, h * LANES:(h + 1) * LANES] = (o * (1.0 - lam_init)).astype(bf16)


def _diff_attn(qkv3, bias, da_lambda, da_subln, l, lam_init):
    bsz, s, _ = qkv3.shape
    tq = bias.shape[-1]
    nq = s // tq
    hw = DA_HEADS * LANES
    kern = functools.partial(_diff_attn_kernel, lam_init=lam_init, tq=tq)
    return pl.pallas_call(
        kern,
        grid=(bsz, nq),
        in_specs=[
            pl.BlockSpec((None, tq, hw), lambda b, i: (b, i, 0)),
            pl.BlockSpec((None, s, hw), lambda b, i: (b, 0, 1)),
            pl.BlockSpec((None, s, hw), lambda b, i: (b, 0, 2)),
            pl.BlockSpec((DA_HEADS, 3, tq, tq), lambda b, i: (0, 0, 0, 0)),
            pl.BlockSpec((None, 4, DA_HEAD_DIM), lambda b, i: (l, 0, 0)),
            pl.BlockSpec((None, 1, LANES), lambda b, i: (l, 0, 0)),
        ],
        out_specs=pl.BlockSpec((None, tq, hw), lambda b, i: (b, i, 0)),
        out_shape=jax.ShapeDtypeStruct((bsz, s, hw), bf16),
        scratch_shapes=_attn_scratch(2 * DA_HEADS, nq, hw, tq),
        compiler_params=_cparams(("arbitrary", "arbitrary")),
        name="diff_attn",
    )(qkv3, qkv3, qkv3, bias, da_lambda, da_subln)


def _mla_attn_kernel(q_ref, k_ref, v_ref, o_ref,
                     vt_ref, m_ref, l_ref, acc_ref, s0_ref, s1_ref, e0_ref, e1_ref, al0_ref, al1_ref, mask_ref,
                     *, tq):
    qi = pl.program_id(1)

    @pl.when(qi == 0)
    def _():
        _fill_vt(v_ref, vt_ref, tq)
        krow = lax.broadcasted_iota(jnp.int32, (tq, tq), 0)
        qcol = lax.broadcasted_iota(jnp.int32, (tq, tq), 1)
        mask_ref[0] = jnp.where(krow <= qcol, 0.0, NEG_BIG)
        mask_ref[1] = jnp.zeros((tq, tq), f32)

    q_list = [q_ref[:, h * LANES:(h + 1) * LANES] for h in range(MLA_HEADS)]
    kcols = [h * LANES for h in range(MLA_HEADS)]
    vrows = [(h // 2) * LANES for h in range(MLA_HEADS)]
    _flash_t(q_list, k_ref, kcols, vt_ref, vrows, lambda a, w: mask_ref[w], 2, qi, tq,
             m_ref, l_ref, acc_ref, (s0_ref, s1_ref), (e0_ref, e1_ref), (al0_ref, al1_ref))
    row = lax.broadcasted_iota(jnp.int32, (LANES, tq), 0)
    for p in range(MLA_HEADS // 2):
        o_t = jnp.where(row < MLA_V, acc_ref[2 * p] / l_ref[2 * p], acc_ref[2 * p + 1] / l_ref[2 * p + 1])
        o_ref[:, p * LANES:(p + 1) * LANES] = o_t.T.astype(bf16)


def _mla_attn(q3, k3, v3, tq):
    bsz, s, hq = q3.shape
    nq = s // tq
    hv = v3.shape[-1]
    kern = functools.partial(_mla_attn_kernel, tq=tq)
    return pl.pallas_call(
        kern,
        grid=(bsz, nq),
        in_specs=[
            pl.BlockSpec((None, tq, hq), lambda b, i: (b, i, 0)),
            pl.BlockSpec((None, s, hq), lambda b, i: (b, 0, 0)),
            pl.BlockSpec((None, s, hv), lambda b, i: (b, 0, 0)),
        ],
        out_specs=pl.BlockSpec((None, tq, hv), lambda b, i: (b, i, 0)),
        out_shape=jax.ShapeDtypeStruct((bsz, s, hv), bf16),
        scratch_shapes=_attn_scratch(MLA_HEADS, nq, hv, tq) + [pltpu.VMEM((2, tq, tq), f32)],
        compiler_params=_cparams(("arbitrary", "arbitrary")),
        name="mla_attn",
    )(q3, k3, v3)


def _split3(x):
    hi = x.astype(bf16)
    r1 = x - hi.astype(f32)
    mid = r1.astype(bf16)
    lo = (r1 - mid.astype(f32)).astype(bf16)
    return hi, mid, lo


def _pair_bcast(x, j, lane):
    return jnp.where(lane < SSM_HEAD_DIM, x[:, 2 * j:2 * j + 1], x[:, 2 * j + 1:2 * j + 2])


def _ssd_kernel(xbc_ref, z_ref, misc_ref, cw_ref, cb_ref, dtb_ref, alog_ref, dvec_ref, nw_ref,
                o_ref, ext_ref, st_ref, y_ref):
    c = pl.program_id(1)
    L = SSM_CHUNK
    G, N = SSM_GROUPS, SSM_STATE

    @pl.when(c == 0)
    def _():
        ext_ref[0:SUBLANES, :] = jnp.zeros((SUBLANES, SSM_CONV_DIM), f32)
        st_ref[...] = jnp.zeros(st_ref.shape, f32)

    zs, dvec, nw = _silu(z_ref[...]), dvec_ref[...], nw_ref[...]
    cw, cbias = cw_ref[...], cb_ref[...]
    dt_in = misc_ref[...] + dtb_ref[...]
    alog = alog_ref[...]

    ext_ref[SUBLANES:, :] = xbc_ref[...]
    conv = cbias + cw[SSM_CONV - 1:SSM_CONV, :] * ext_ref[SUBLANES:, :]
    for s in range(1, SSM_CONV):
        conv = conv + cw[SSM_CONV - 1 - s:SSM_CONV - s, :] * ext_ref[SUBLANES - s:SUBLANES - s + L, :]
    ext_ref[0:SUBLANES, :] = ext_ref[L:L + SUBLANES, :]
    xa = _silu(conv)
    xs = xa[:, :SSM_D_INNER]
    bm = xa[:, SSM_D_INNER:SSM_D_INNER + G * N]
    cm = xa[:, SSM_D_INNER + G * N:]

    lane = _lane_iota((L, LANES))
    lane1 = _lane_iota((1, LANES))
    dt = jax.nn.softplus(dt_in)
    a_l = jnp.where(lane1 < SSM_HEADS, -jnp.exp(alog), 0.0)
    da = dt * a_l
    row = lax.broadcasted_iota(jnp.int32, (L, L), 0)
    col = lax.broadcasted_iota(jnp.int32, (L, L), 1)
    causal = col <= row
    tri = jnp.where(causal, 1.0, 0.0).astype(bf16)
    hi, mid, lo = _split3(da)
    cs = _dot(tri, hi) + _dot(tri, mid) + _dot(tri, lo)
    cs_t = cs.T
    cs_last = cs[L - 1:L, :]
    w_end = dt * jnp.exp(cs_last - cs)
    e_cs = jnp.exp(cs)
    cdec = jnp.exp(cs_last)

    for g in range(G):
        bm_g = bm[:, g * N:(g + 1) * N]
        cm_g = cm[:, g * N:(g + 1) * N].astype(bf16)
        cb = _dot_nt(cm_g, bm_g.astype(bf16))
        bm_t = bm_g.T.astype(bf16)
        for jj in range(2):
            j = 2 * g + jj
            sl = slice(j * LANES, (j + 1) * LANES)
            psl = slice(jj * LANES, (jj + 1) * LANES)
            xs_p = xs[:, sl]
            xdt = (xs_p * _pair_bcast(dt, j, lane)).astype(bf16)
            yd = []
            for hh in (2 * j, 2 * j + 1):
                seg = cs[:, hh:hh + 1] - cs_t[hh:hh + 1, :]
                decay = jnp.exp(jnp.where(causal, seg, -jnp.inf))
                yd.append(_dot((cb * decay).astype(bf16), xdt))
            y_diag = jnp.where(lane < SSM_HEAD_DIM, yd[0], yd[1])
            prev = st_ref[g, :, psl]
            y_off = _dot(cm_g, prev.astype(bf16)) * _pair_bcast(e_cs, j, lane)
            y_ref[:, sl] = y_diag + y_off + dvec[:, sl] * xs_p
            xw = (xs_p * _pair_bcast(w_end, j, lane)).astype(bf16)
            cd = jnp.where(lane1 < SSM_HEAD_DIM, cdec[:, 2 * j:2 * j + 1], cdec[:, 2 * j + 1:2 * j + 2])
            st_ref[g, :, psl] = prev * cd + _dot(bm_t, xw)

    gw = SSM_D_INNER // G
    for g in range(G):
        sl = slice(g * gw, (g + 1) * gw)
        yz = y_ref[:, sl] * zs[:, sl]
        yz = yz * lax.rsqrt(jnp.mean(yz * yz, axis=-1, keepdims=True) + EPS) * nw[:, sl]
        o_ref[:, sl] = yz.astype(bf16)


def _ssd(rest3, conv_w, conv_b, dt_bias, a_log, dvec, norm_w, l):
    bsz, s, _ = rest3.shape
    L = SSM_CHUNK
    nc = s // L
    misc_blk = (R_LAT + MLA_Q_RANK + MLA_KV_RANK) // LANES
    vec = lambda n: pl.BlockSpec((None, 1, n), lambda b, c: (l, 0, 0))
    return pl.pallas_call(
        _ssd_kernel,
        grid=(bsz, nc),
        in_specs=[
            pl.BlockSpec((None, L, SSM_CONV_DIM), lambda b, c: (b, c, R_XBC // SSM_CONV_DIM)),
            pl.BlockSpec((None, L, SSM_D_INNER), lambda b, c: (b, c, R_Z // SSM_D_INNER)),
            pl.BlockSpec((None, L, LANES), lambda b, c: (b, c, misc_blk)),
            pl.BlockSpec((None, SSM_CONV, SSM_CONV_DIM), lambda b, c: (l, 0, 0)),
            vec(SSM_CONV_DIM), vec(LANES), vec(LANES), vec(SSM_D_INNER), vec(SSM_D_INNER),
        ],
        out_specs=pl.BlockSpec((None, L, SSM_D_INNER), lambda b, c: (b, c, 0)),
        out_shape=jax.ShapeDtypeStruct((bsz, s, SSM_D_INNER), bf16),
        scratch_shapes=[
            pltpu.VMEM((L + SUBLANES, SSM_CONV_DIM), f32),
            pltpu.VMEM((SSM_GROUPS, SSM_STATE, 4 * SSM_HEAD_DIM), f32),
            pltpu.VMEM((L, SSM_D_INNER), f32),
        ],
        compiler_params=_cparams(("arbitrary", "arbitrary")),
        name="ssd",
    )(rest3, rest3, rest3, conv_w, conv_b, dt_bias, a_log, dvec, norm_w)


def _merge_kernel(ya_ref, yb_ref, ys_ref, ga_ref, gb_ref, gc_ref, x_ref, mod_ref,
                  wa_ref, wb_ref, wc_ref, wo_ref, o_ref):
    merged = (ga_ref[...] * _dot(ya_ref[...], wa_ref[...])
              + gb_ref[...] * _dot(yb_ref[...], wb_ref[...])
              + gc_ref[...] * _dot(ys_ref[...], wc_ref[...]))
    o_ref[...] = x_ref[...] + mod_ref[2:3, :] * _dot(merged.astype(bf16), wo_ref[...])


def _merge(ya, yb, ys, rest, x2, mod, wa, wb, wc, wo, l, rows_per_batch):
    t, d = x2.shape
    tm = min(512, rows_per_batch)
    tpb = rows_per_batch // tm
    gblk = R_GATES // d
    row = lambda n: pl.BlockSpec((tm, n), lambda i: (i, 0))
    wspec = lambda k: pl.BlockSpec((None, k, d), lambda i: (l, 0, 0))
    return pl.pallas_call(
        _merge_kernel,
        grid=(t // tm,),
        in_specs=[
            row(ya.shape[1]), row(yb.shape[1]), row(ys.shape[1]),
            pl.BlockSpec((tm, d), lambda i: (i, gblk)),
            pl.BlockSpec((tm, d), lambda i: (i, gblk + 1)),
            pl.BlockSpec((tm, d), lambda i: (i, gblk + 2)),
            row(d),
            pl.BlockSpec((None, None, 6, d), lambda i: (l, i // tpb, 0, 0)),
            wspec(ya.shape[1]), wspec(yb.shape[1]), wspec(ys.shape[1]), wspec(d),
        ],
        out_specs=row(d),
        out_shape=jax.ShapeDtypeStruct((t, d), f32),
        compiler_params=_cparams(("arbitrary",)),
        name="merge_out",
    )(ya, yb, ys, rest, rest, rest, x2, mod, wa, wb, wc, wo)


FFN_TF = 256
FFN_NF = FFN_DIM // FFN_TF
FFN_RB = 512


def _ffn_kernel(x_ref, nw_ref, mod_ref, wa_ref, wv_ref, cwa_ref, cwv_ref, cba_ref, cbv_ref, wd_ref,
                o_ref, h_ref, ua_ref, uv_ref, acc_ref):
    fi = pl.program_id(1)
    tm = x_ref.shape[0]

    @pl.when(fi == 0)
    def _():
        x = x_ref[...]
        y = x * lax.rsqrt(jnp.mean(x * x, axis=-1, keepdims=True) + EPS) * nw_ref[...]
        h_ref[...] = (y * (1.0 + mod_ref[4:5, :]) + mod_ref[3:4, :]).astype(bf16)
        acc_ref[...] = jnp.zeros(acc_ref.shape, f32)
        ua_ref[0:SUBLANES, :] = jnp.zeros((SUBLANES, FFN_TF), f32)
        uv_ref[0:SUBLANES, :] = jnp.zeros((SUBLANES, FFN_TF), f32)

    wa, wv, wd = wa_ref[...], wv_ref[...], wd_ref[...]
    cwa, cwv, cba, cbv = cwa_ref[...], cwv_ref[...], cba_ref[...], cbv_ref[...]

    def conv(u_ref, cw, cb, base):
        out = cb + cw[FFN_CONV - 1:FFN_CONV, :] * u_ref[base:base + FFN_RB, :]
        for s in range(1, FFN_CONV):
            out = out + cw[FFN_CONV - 1 - s:FFN_CONV - s, :] * u_ref[base - s:base - s + FFN_RB, :]
        return out

    def up(rb):
        base = SUBLANES + rb * FFN_RB
        hb = h_ref[rb * FFN_RB:(rb + 1) * FFN_RB, :]
        ua_ref[base:base + FFN_RB, :] = _dot(hb, wa)
        uv_ref[base:base + FFN_RB, :] = _dot(hb, wv)

    nrb = tm // FFN_RB
    up(0)
    for rb in range(nrb):
        if rb + 1 < nrb:
            up(rb + 1)
        base = SUBLANES + rb * FFN_RB
        act = (_silu(conv(ua_ref, cwa, cba, base)) * conv(uv_ref, cwv, cbv, base)).astype(bf16)
        acc_ref[rb * FFN_RB:(rb + 1) * FFN_RB, :] += _dot(act, wd)

    @pl.when(fi == pl.num_programs(1) - 1)
    def _():
        o_ref[...] = x_ref[...] + mod_ref[5:6, :] * acc_ref[...]


def _ffn(x3, norm_w, mod, wup, conv_w, conv_b, wdown, l):
    bsz, s, d = x3.shape
    tf, nf = FFN_TF, FFN_NF
    return pl.pallas_call(
        _ffn_kernel,
        grid=(bsz, nf),
        in_specs=[
            pl.BlockSpec((None, s, d), lambda b, f: (b, 0, 0)),
            pl.BlockSpec((None, 1, d), lambda b, f: (l, 0, 0)),
            pl.BlockSpec((None, None, 6, d), lambda b, f: (l, b, 0, 0)),
            pl.BlockSpec((None, d, tf), lambda b, f: (l, 0, f)),
            pl.BlockSpec((None, d, tf), lambda b, f: (l, 0, nf + f)),
            pl.BlockSpec((None, FFN_CONV, tf), lambda b, f: (l, 0, f)),
            pl.BlockSpec((None, FFN_CONV, tf), lambda b, f: (l, 0, nf + f)),
            pl.BlockSpec((None, 1, tf), lambda b, f: (l, 0, f)),
            pl.BlockSpec((None, 1, tf), lambda b, f: (l, 0, nf + f)),
            pl.BlockSpec((None, tf, d), lambda b, f: (l, f, 0)),
        ],
        out_specs=pl.BlockSpec((None, s, d), lambda b, f: (b, 0, 0)),
        out_shape=jax.ShapeDtypeStruct((bsz, s, d), f32),
        scratch_shapes=[
            pltpu.VMEM((s, d), bf16),
            pltpu.VMEM((s + SUBLANES, tf), f32), pltpu.VMEM((s + SUBLANES, tf), f32),
            pltpu.VMEM((s, d), f32),
        ],
        compiler_params=_cparams(("arbitrary", "arbitrary")),
        name="ffn",
    )(x3, norm_w, mod, wup, wup, conv_w, conv_w, conv_b, conv_b, wdown)


def _pack_w_in(w_in):
    nl, d, _ = w_in.shape
    sizes = (512, 512, 512, MLA_Q_RANK, MLA_KV_RANK, MLA_ROPE, SSM_D_INNER, SSM_CONV_DIM, SSM_HEADS, 3 * D_MODEL)
    offs = np.concatenate([[0], np.cumsum(sizes)])
    qa, ka, va, cq, ckv, kr, z, xbc, dt, gates = [w_in[..., int(offs[i]):int(offs[i + 1])] for i in range(10)]
    zeros = lambda n: jnp.zeros((nl, d, n), w_in.dtype)
    misc = jnp.concatenate([dt, zeros(ROPE_X1 - SSM_HEADS), kr[..., :HALF_ROPE], zeros(ROPE_X2 - ROPE_X1 - HALF_ROPE),
                            kr[..., HALF_ROPE:], zeros(LANES - ROPE_X2 - HALF_ROPE)], axis=-1)
    return jnp.concatenate([qa, ka, va, xbc, z, gates, cq, ckv, misc], axis=-1).astype(bf16)


def _pad_lanes(v, n):
    return jnp.pad(v, [(0, 0)] * (v.ndim - 1) + [(0, n - v.shape[-1])])


def _to_head_slot(v):
    return jnp.zeros(v.shape[:-1] + (LANES,), v.dtype).at[..., LANE_OF_DIM[:v.shape[-1]]].set(v)


def kernel(x, c, positions, rel_bias, ada_w, ada_b, norm_mix, norm_ffn, w_in, da_q_norm, da_k_norm, da_lambda, da_subln, mla_q_a_norm, mla_kv_a_norm, mla_w_uq, mla_w_ukv, mla_q_norm, mla_k_norm, ssm_conv_w, ssm_conv_b, ssm_dt_bias, ssm_a_log, ssm_d, ssm_norm, w_branch_a, w_branch_b, w_branch_c, w_out, ffn_w_up, ffn_conv_w, ffn_conv_b, ffn_w_down):
    bsz, seq, d = x.shape
    nl = w_in.shape[0]
    t = bsz * seq
    tq = min(ATT_TQ, seq)

    w_in_p = _pack_w_in(w_in)
    qk_w = jnp.stack([jnp.tile(da_q_norm, (1, 2 * DA_HEADS)) * (DA_HEAD_DIM ** -0.5 * LOG2E),
                      jnp.tile(da_k_norm, (1, 2 * DA_HEADS))], axis=1).reshape(nl, 2, 1, IN_TN)
    blk = np.arange(IN_TN) // DA_HEAD_DIM
    bd = jnp.asarray(blk[:, None] == blk[None, :], bf16)
    wuq = _to_head_slot(mla_w_uq.reshape(nl, MLA_Q_RANK, MLA_HEADS, MLA_QK))
    wuq = wuq.reshape(nl, MLA_Q_RANK, MLA_HEADS * LANES).astype(bf16)
    wukv = mla_w_ukv.reshape(nl, MLA_KV_RANK, MLA_HEADS, MLA_NOPE + MLA_V)
    wuk = _to_head_slot(wukv[..., :MLA_NOPE]).reshape(nl, MLA_KV_RANK, MLA_HEADS * LANES).astype(bf16)
    wuv = wukv[..., MLA_NOPE:].reshape(nl, MLA_KV_RANK, MLA_HEADS * MLA_V).astype(bf16)
    qn = (_to_head_slot(mla_q_norm) * (MLA_QK ** -0.5 * LOG2E)).reshape(nl, 1, LANES)
    kn = _to_head_slot(mla_k_norm).reshape(nl, 1, LANES)
    dt_bias = _pad_lanes(ssm_dt_bias, LANES).reshape(nl, 1, LANES)
    a_log = _pad_lanes(ssm_a_log, LANES).reshape(nl, 1, LANES)
    dvec = jnp.repeat(ssm_d, SSM_HEAD_DIM, axis=-1).reshape(nl, 1, SSM_D_INNER)
    v3 = lambda a: a.reshape(nl, 1, a.shape[-1])
    wa, wb, wc, wo = (w.astype(bf16) for w in (w_branch_a, w_branch_b, w_branch_c, w_out))
    wup, wdown = ffn_w_up.astype(bf16), ffn_w_down.astype(bf16)

    mod = _ada_mod(c, ada_w, ada_b).reshape(nl, bsz, 6, d)
    ctab, stab = _rope_tables(positions)
    bias = _bias_tiles(rel_bias, tq)

    x2 = x.reshape(t, d)
    for l in range(nl):
        lam_init = 0.8 - 0.6 * math.exp(-0.3 * l)
        qkv, rest = _inproj(x2, v3(norm_mix), mod, w_in_p, qk_w, bd, l, seq)
        q_b, k_b, v_b = _mla_prep(rest, v3(mla_q_a_norm), v3(mla_kv_a_norm), wuq, wuk, wuv, qn, kn, ctab, stab, l)
        ya = _diff_attn(qkv.reshape(bsz, seq, QKV_COLS), bias, da_lambda, v3(da_subln), l, lam_init)
        yb = _mla_attn(q_b.reshape(bsz, seq, -1), k_b.reshape(bsz, seq, -1), v_b.reshape(bsz, seq, -1), tq)
        ys = _ssd(rest.reshape(bsz, seq, REST_COLS), ssm_conv_w, v3(ssm_conv_b), dt_bias, a_log, dvec,
                  v3(ssm_norm), l)
        x2 = _merge(ya.reshape(t, -1), yb.reshape(t, -1), ys.reshape(t, -1), rest, x2, mod, wa, wb, wc, wo, l, seq)
        x2 = _ffn(x2.reshape(bsz, seq, d), v3(norm_ffn), mod, wup, ffn_conv_w, v3(ffn_conv_b), wdown, l).reshape(t, d)
    return x2.reshape(bsz, seq, d)
```

```python
import functools
import math

import jax
import jax.numpy as jnp
import numpy as np
from jax import lax
from jax.experimental import pallas as pl
from jax.experimental.pallas import tpu as pltpu

f32 = jnp.float32
bf16 = jnp.bfloat16

D_MODEL = 1024
DA_HEADS = 4
DA_HEAD_DIM = 64
MLA_HEADS = 8
MLA_Q_RANK = 256
MLA_KV_RANK = 128
MLA_NOPE = 64
MLA_ROPE = 32
MLA_V = 64
MLA_QK = MLA_NOPE + MLA_ROPE
SSM_D_INNER = 1024
SSM_HEAD_DIM = 64
SSM_HEADS = 16
SSM_GROUPS = 4
SSM_STATE = 128
SSM_CONV = 4
SSM_CHUNK = 128
SSM_CONV_DIM = SSM_D_INNER + 2 * SSM_GROUPS * SSM_STATE
FFN_DIM = 2816
FFN_CONV = 3
REL_BUCKETS = 32
REL_MAX_DIST = 128
ROPE_THETA = 10000.0
EPS = 1e-6

LANES = 128
SUBLANES = 8
VMEM_LIMIT = 56 * 1024 * 1024

IN_TN = 512
G_COLS = 3 * D_MODEL
GQKV_COLS = G_COLS + 3 * 512
R_XBC = 0
R_Z = R_XBC + SSM_CONV_DIM
R_LAT = R_Z + SSM_D_INNER
REST_COLS = R_LAT + 512
ROPE_X1 = 32
ROPE_X2 = ROPE_X1 + LANES // 2
HALF_ROPE = MLA_ROPE // 2
LANE_OF_DIM = np.concatenate([np.arange(0, 32), np.arange(48, 80),
                              np.arange(ROPE_X1, ROPE_X1 + HALF_ROPE), np.arange(ROPE_X2, ROPE_X2 + HALF_ROPE)])
NEG_BIG = -1e30
LOG2E = math.log2(math.e)

ATT_TQ = 256
MLA_TQ = 256


def _cparams(sem):
    return pltpu.CompilerParams(dimension_semantics=sem, vmem_limit_bytes=VMEM_LIMIT)


def _lane_iota(shape):
    return lax.broadcasted_iota(jnp.int32, shape, len(shape) - 1)


def _dot(a, b):
    return jnp.dot(a, b, preferred_element_type=f32)


def _dot_nt(a, b):
    return lax.dot_general(a, b, (((1,), (1,)), ((), ())), preferred_element_type=f32)


def _sigmoid(x):
    return 1.0 / (1.0 + jnp.exp(-x))


def _silu(x):
    return x * _sigmoid(x)


def _ada_kernel(c_ref, w_ref, b_ref, o_ref):
    ca = _silu(c_ref[...]).astype(bf16)
    o_ref[...] = _dot(ca, w_ref[...].astype(bf16)) + b_ref[...]


def _ada_mod(c, ada_w, ada_b):
    nl, d, n = ada_w.shape
    bsz = c.shape[0]
    tn = 1024
    return pl.pallas_call(
        _ada_kernel,
        grid=(nl, n // tn),
        in_specs=[
            pl.BlockSpec((bsz, d), lambda l, j: (0, 0)),
            pl.BlockSpec((None, d, tn), lambda l, j: (l, 0, j)),
            pl.BlockSpec((None, 1, tn), lambda l, j: (l, 0, j)),
        ],
        out_specs=pl.BlockSpec((None, bsz, tn), lambda l, j: (l, 0, j)),
        out_shape=jax.ShapeDtypeStruct((nl, bsz, n), f32),
        compiler_params=_cparams(("arbitrary", "arbitrary")),
        name="ada_mod",
    )(c, ada_w, ada_b.reshape(nl, 1, n))


def _rope_kernel(pos_ref, freq_ref, c_ref, s_ref):
    ang = pos_ref[...].astype(f32) * freq_ref[...]
    cos, sin = jnp.cos(ang), jnp.sin(ang)
    lane = _lane_iota(ang.shape)
    first = (lane >= ROPE_X1) & (lane < ROPE_X1 + HALF_ROPE)
    second = (lane >= ROPE_X2) & (lane < ROPE_X2 + HALF_ROPE)
    c_ref[...] = jnp.where(first | second, cos, 1.0)
    s_ref[...] = jnp.where(first, -sin, jnp.where(second, sin, 0.0))


def _rope_tables(positions):
    t = positions.size
    tm = min(t, 2048)
    inv_freq = ROPE_THETA ** (-jnp.arange(0, MLA_ROPE, 2, dtype=f32) / MLA_ROPE)
    freq = jnp.zeros((LANES,), f32).at[LANE_OF_DIM[MLA_NOPE:]].set(jnp.tile(inv_freq, 2))
    out = jax.ShapeDtypeStruct((t, LANES), f32)
    spec = pl.BlockSpec((tm, LANES), lambda i: (i, 0))
    return pl.pallas_call(
        _rope_kernel,
        grid=(t // tm,),
        in_specs=[pl.BlockSpec((tm, 1), lambda i: (i, 0)), pl.BlockSpec((1, LANES), lambda i: (0, 0))],
        out_specs=[spec, spec],
        out_shape=[out, out],
        compiler_params=_cparams(("arbitrary",)),
        name="rope_tables",
    )(positions.reshape(t, 1), freq.reshape(1, LANES))


def _bias_kernel(rb_ref, o_ref):
    h, which = pl.program_id(0), pl.program_id(1)
    tq = o_ref.shape[-1]
    row = lax.broadcasted_iota(jnp.int32, (tq, tq), 0)
    col = lax.broadcasted_iota(jnp.int32, (tq, tq), 1)
    rel = col - row + which * tq
    d = jnp.maximum(rel, 0)
    max_exact = REL_BUCKETS // 2
    large = max_exact + (jnp.log(jnp.maximum(d, 1).astype(f32) / max_exact)
                         / math.log(REL_MAX_DIST / max_exact) * (REL_BUCKETS - max_exact)).astype(jnp.int32)
    large = jnp.minimum(large, REL_BUCKETS - 1)
    bucket = jnp.where(d < max_exact, d, large)
    val = jnp.full((tq, tq), rb_ref[0, h], f32)
    for k in range(1, REL_BUCKETS):
        val = jnp.where(bucket == k, rb_ref[k, h], val)
    val = (val - rb_ref[REL_BUCKETS - 1, h]) * LOG2E
    val = jnp.where(rel < 0, NEG_BIG, val)
    o_ref[...] = jnp.where(which >= 2, 0.0, val)


def _bias_tiles(rel_bias, tq):
    return pl.pallas_call(
        _bias_kernel,
        grid=(DA_HEADS, 3),
        in_specs=[pl.BlockSpec(memory_space=pltpu.SMEM)],
        out_specs=pl.BlockSpec((None, None, tq, tq), lambda h, w: (h, w, 0, 0)),
        out_shape=jax.ShapeDtypeStruct((DA_HEADS, 3, tq, tq), f32),
        compiler_params=_cparams(("arbitrary", "arbitrary")),
        name="bias_tiles",
    )(rel_bias)


def _inproj_kernel(x_ref, nw_ref, mod_ref, w_ref, qkw_ref, bd_ref, gqkv_ref, rest_ref, h_ref):
    j = pl.program_id(1)
    jq = G_COLS // IN_TN

    @pl.when(j == 0)
    def _():
        x = x_ref[...]
        y = x * lax.rsqrt(jnp.mean(x * x, axis=-1, keepdims=True) + EPS) * nw_ref[...]
        h_ref[...] = (y * (1.0 + mod_ref[1:2, :]) + mod_ref[0:1, :]).astype(bf16)

    def proj():
        return _dot(h_ref[...], w_ref[...])

    @pl.when(j < jq)
    def _():
        gqkv_ref[...] = (0.5 * jnp.tanh(0.5 * proj()) + 0.5).astype(bf16)

    @pl.when((j >= jq) & (j < jq + 2))
    def _():
        acc = proj()
        ss = _dot((acc * acc).astype(bf16), bd_ref[...]) * (1.0 / DA_HEAD_DIM)
        gqkv_ref[...] = (acc * lax.rsqrt(ss + EPS) * qkw_ref[...]).astype(bf16)

    @pl.when(j == jq + 2)
    def _():
        gqkv_ref[...] = proj().astype(bf16)

    @pl.when(j > jq + 2)
    def _():
        rest_ref[...] = proj()


def _inproj(x2, norm_w, mod, w_p, qk_w, bd, l, rows_per_batch):
    t, d = x2.shape
    tm = rows_per_batch
    nj = (GQKV_COLS + REST_COLS) // IN_TN
    nb = GQKV_COLS // IN_TN
    jq = G_COLS // IN_TN
    return pl.pallas_call(
        _inproj_kernel,
        grid=(t // tm, nj),
        in_specs=[
            pl.BlockSpec((tm, d), lambda i, j: (i, 0)),
            pl.BlockSpec((None, 1, d), lambda i, j: (l, 0, 0)),
            pl.BlockSpec((None, None, 6, d), lambda i, j: (l, i, 0, 0)),
            pl.BlockSpec((None, d, IN_TN), lambda i, j: (l, 0, j)),
            pl.BlockSpec((None, None, 1, IN_TN), lambda i, j: (l, jnp.clip(j - jq, 0, 1), 0, 0)),
            pl.BlockSpec((IN_TN, IN_TN), lambda i, j: (0, 0)),
        ],
        out_specs=[
            pl.BlockSpec((tm, IN_TN), lambda i, j: (i, jnp.minimum(j, nb - 1))),
            pl.BlockSpec((tm, IN_TN), lambda i, j: (i, jnp.maximum(j - nb, 0))),
        ],
        out_shape=[jax.ShapeDtypeStruct((t, GQKV_COLS), bf16), jax.ShapeDtypeStruct((t, REST_COLS), f32)],
        scratch_shapes=[pltpu.VMEM((tm, d), bf16)],
        compiler_params=_cparams(("arbitrary", "arbitrary")),
        name="inproj",
    )(x2, norm_w, mod, w_p, qk_w, bd)


def _rope_apply(x, c, s):
    return x * c + pltpu.roll(x, LANES // 2, 1) * s


def _mla_prep_kernel(lat_ref, qan_ref, kvan_ref, wuq_ref, wuk_ref, wuv_ref, qn_ref, kn_ref,
                     c_ref, s_ref, q_ref, k_ref, v_ref):
    lat = lat_ref[...]
    c, s = c_ref[...], s_ref[...]
    cq = lat[:, :MLA_Q_RANK]
    cqn = (cq * lax.rsqrt(jnp.mean(cq * cq, axis=-1, keepdims=True) + EPS) * qan_ref[...]).astype(bf16)
    ckv = lat[:, MLA_Q_RANK:MLA_Q_RANK + MLA_KV_RANK]
    ckvn = (ckv * lax.rsqrt(jnp.mean(ckv * ckv, axis=-1, keepdims=True) + EPS) * kvan_ref[...]).astype(bf16)
    misc = lat[:, MLA_Q_RANK + MLA_KV_RANK:]
    lane = _lane_iota(misc.shape)
    in_rope = (((lane >= ROPE_X1) & (lane < ROPE_X1 + HALF_ROPE))
               | ((lane >= ROPE_X2) & (lane < ROPE_X2 + HALF_ROPE)))
    krb = jnp.where(in_rope, misc, 0.0)
    qf = _dot(cqn, wuq_ref[...])
    kf = _dot(ckvn, wuk_ref[...])
    v_ref[...] = _dot(ckvn, wuv_ref[...]).astype(bf16)
    inv = 1.0 / MLA_QK
    qn, kn = qn_ref[...], kn_ref[...]
    for h in range(MLA_HEADS):
        sl = slice(h * LANES, (h + 1) * LANES)
        qh = qf[:, sl]
        qh = qh * lax.rsqrt(jnp.sum(qh * qh, axis=-1, keepdims=True) * inv + EPS) * qn
        q_ref[:, sl] = _rope_apply(qh, c, s).astype(bf16)
        kh = kf[:, sl] + krb
        kh = kh * lax.rsqrt(jnp.sum(kh * kh, axis=-1, keepdims=True) * inv + EPS) * kn
        k_ref[:, sl] = _rope_apply(kh, c, s).astype(bf16)


def _mla_prep(rest, qan, kvan, wuq, wuk, wuv, qn, kn, ctab, stab, l):
    t = rest.shape[0]
    tm = min(512, t)
    lat_blk = R_LAT // 512
    hq = MLA_HEADS * LANES
    vec = lambda n: pl.BlockSpec((None, 1, n), lambda i: (l, 0, 0))
    tab = pl.BlockSpec((tm, LANES), lambda i: (i, 0))
    return pl.pallas_call(
        _mla_prep_kernel,
        grid=(t // tm,),
        in_specs=[
            pl.BlockSpec((tm, 512), lambda i: (i, lat_blk)),
            vec(MLA_Q_RANK), vec(MLA_KV_RANK),
            pl.BlockSpec((None, MLA_Q_RANK, hq), lambda i: (l, 0, 0)),
            pl.BlockSpec((None, MLA_KV_RANK, hq), lambda i: (l, 0, 0)),
            pl.BlockSpec((None, MLA_KV_RANK, MLA_HEADS * MLA_V), lambda i: (l, 0, 0)),
            vec(LANES), vec(LANES), tab, tab,
        ],
        out_specs=[
            pl.BlockSpec((tm, hq), lambda i: (i, 0)),
            pl.BlockSpec((tm, hq), lambda i: (i, 0)),
            pl.BlockSpec((tm, MLA_HEADS * MLA_V), lambda i: (i, 0)),
        ],
        out_shape=[jax.ShapeDtypeStruct((t, hq), bf16), jax.ShapeDtypeStruct((t, hq), bf16),
                   jax.ShapeDtypeStruct((t, MLA_HEADS * MLA_V), bf16)],
        compiler_params=_cparams(("arbitrary",)),
        name="mla_prep",
    )(rest, qan, kvan, wuq, wuk, wuv, qn, kn, ctab, stab)


def _flash_t(q_list, k_ref, kcols, vt_ref, vrows, tile_fn, ntiles, qi, tq,
             m_ref, l_ref, acc_ref, s_refs, e_refs, al_refs):
    nmap = len(q_list)
    m_ref[...] = jnp.full(m_ref.shape, -jnp.inf, f32)
    l_ref[...] = jnp.zeros(l_ref.shape, f32)
    acc_ref[...] = jnp.zeros(acc_ref.shape, f32)
    last = qi

    def qk(t, slot):
        start = pl.multiple_of(t * tq, tq)
        w = jnp.minimum(qi - t, ntiles - 1)
        vals = []
        for a in range(nmap):
            k = k_ref[pl.ds(start, tq), kcols[a]:kcols[a] + LANES]
            vals.append(_dot_nt(k, q_list[a]) + tile_fn(a, w))
        for a in range(nmap):
            s_refs[slot][a] = vals[a]

    def sm(slot):
        for a in range(nmap):
            s = s_refs[slot][a]
            m_prev = m_ref[a]
            m_new = jnp.maximum(m_prev, jnp.max(s, axis=0, keepdims=True))
            alpha = jnp.exp2(m_prev - m_new)
            e = jnp.exp2(s - m_new)
            l_ref[a] = alpha * l_ref[a] + jnp.sum(e, axis=0, keepdims=True)
            m_ref[a] = m_new
            e_refs[slot][a] = e.astype(bf16)
            al_refs[slot][a] = alpha

    def pv(t, slot):
        for a in range(nmap):
            vt = vt_ref[t, vrows[a]:vrows[a] + LANES, :]
            acc_ref[a] = al_refs[slot][a] * acc_ref[a] + _dot(vt, e_refs[slot][a])

    def body(t, cur):
        qk(jnp.minimum(t + 1, last), 1 - cur)
        pv(t - 1, 1 - cur)
        sm(cur)

    qk(0, 0)
    qk(jnp.minimum(1, last), 1)
    sm(0)

    def pair(i, carry):
        t = 2 * i + 1
        body(t, 1)
        body(t + 1, 0)
        return carry

    lax.fori_loop(0, last // 2, pair, 0)
    odd = lax.rem(last, 2) == 1

    @pl.when(odd)
    def _():
        body(last, 1)
        pv(last, 1)

    @pl.when(jnp.logical_not(odd))
    def _():
        pv(last, 0)


def _fill_vt(v_ref, vt_ref, tq):
    for c in range(vt_ref.shape[0]):
        vt_ref[c] = v_ref[c * tq:(c + 1) * tq, :].astype(f32).T.astype(bf16)


def _attn_scratch(nmap, nq, hv, tq):
    sbuf = pltpu.VMEM((nmap, tq, tq), f32)
    ebuf = pltpu.VMEM((nmap, tq, tq), bf16)
    row = pltpu.VMEM((nmap, 1, tq), f32)
    return [pltpu.VMEM((nq, hv, tq), bf16), row, row, pltpu.VMEM((nmap, LANES, tq), f32),
            sbuf, sbuf, ebuf, ebuf, row, row]


def _diff_attn_kernel(q_ref, k_ref, v_ref, bias_ref, lp_ref, sub_ref, o_ref,
                      vt_ref, m_ref, l_ref, acc_ref, s0_ref, s1_ref, e0_ref, e1_ref, al0_ref, al1_ref,
                      *, lam_init, tq):
    qi = pl.program_id(1)

    @pl.when(qi == 0)
    def _():
        _fill_vt(v_ref, vt_ref, tq)

    lane = _lane_iota((tq, LANES))
    q_list, kcols, vrows = [], [], []
    for h in range(DA_HEADS):
        q = q_ref[:, h * LANES:(h + 1) * LANES]
        zero = jnp.zeros_like(q)
        q_list += [jnp.where(lane < DA_HEAD_DIM, q, zero), jnp.where(lane >= DA_HEAD_DIM, q, zero)]
        kcols += [h * LANES, h * LANES]
        vrows += [h * LANES, h * LANES]
    lp, sub = lp_ref[...], sub_ref[...]
    _flash_t(q_list, k_ref, kcols, vt_ref, vrows, lambda a, w: bias_ref[a // 2, w], 3, qi, tq,
             m_ref, l_ref, acc_ref, (s0_ref, s1_ref), (e0_ref, e1_ref), (al0_ref, al1_ref))
    lam = (jnp.exp(jnp.sum(lp[0:1] * lp[1:2], axis=-1, keepdims=True))
           - jnp.exp(jnp.sum(lp[2:3] * lp[3:4], axis=-1, keepdims=True)) + lam_init)
    for h in range(DA_HEADS):
        o_t = acc_ref[2 * h] / l_ref[2 * h] - lam * (acc_ref[2 * h + 1] / l_ref[2 * h + 1])
        o = o_t.T
        o = o * lax.rsqrt(jnp.mean(o * o, axis=-1, keepdims=True) + EPS) * sub
        o_ref[:, h * LANES:(h + 1) * LANES] = (o * (1.0 - lam_init)).astype(bf16)


def _diff_attn(qkv3, bias, da_lambda, da_subln, l, lam_init):
    bsz, s, _ = qkv3.shape
    tq = bias.shape[-1]
    nq = s // tq
    hw = DA_HEADS * LANES
    qblk = G_COLS // hw
    kern = functools.partial(_diff_attn_kernel, lam_init=lam_init, tq=tq)
    return pl.pallas_call(
        kern,
        grid=(bsz, nq),
        in_specs=[
            pl.BlockSpec((None, tq, hw), lambda b, i: (b, i, qblk)),
            pl.BlockSpec((None, s, hw), lambda b, i: (b, 0, qblk + 1)),
            pl.BlockSpec((None, s, hw), lambda b, i: (b, 0, qblk + 2)),
            pl.BlockSpec((DA_HEADS, 3, tq, tq), lambda b, i: (0, 0, 0, 0)),
            pl.BlockSpec((None, 4, DA_HEAD_DIM), lambda b, i: (l, 0, 0)),
            pl.BlockSpec((None, 1, LANES), lambda b, i: (l, 0, 0)),
        ],
        out_specs=pl.BlockSpec((None, tq, hw), lambda b, i: (b, i, 0)),
        out_shape=jax.ShapeDtypeStruct((bsz, s, hw), bf16),
        scratch_shapes=_attn_scratch(2 * DA_HEADS, nq, hw, tq),
        compiler_params=_cparams(("arbitrary", "arbitrary")),
        name="diff_attn",
    )(qkv3, qkv3, qkv3, bias, da_lambda, da_subln)


def _mla_attn_kernel(q_ref, k_ref, v_ref, o_ref,
                     vt_ref, m_ref, l_ref, acc_ref, s0_ref, s1_ref, e0_ref, e1_ref, al0_ref, al1_ref, mask_ref,
                     *, tq):
    qi = pl.program_id(1)

    @pl.when(qi == 0)
    def _():
        _fill_vt(v_ref, vt_ref, tq)
        krow = lax.broadcasted_iota(jnp.int32, (tq, tq), 0)
        qcol = lax.broadcasted_iota(jnp.int32, (tq, tq), 1)
        mask_ref[0] = jnp.where(krow <= qcol, 0.0, NEG_BIG)
        mask_ref[1] = jnp.zeros((tq, tq), f32)

    q_list = [q_ref[:, h * LANES:(h + 1) * LANES] for h in range(MLA_HEADS)]
    kcols = [h * LANES for h in range(MLA_HEADS)]
    vrows = [(h // 2) * LANES for h in range(MLA_HEADS)]
    _flash_t(q_list, k_ref, kcols, vt_ref, vrows, lambda a, w: mask_ref[w], 2, qi, tq,
             m_ref, l_ref, acc_ref, (s0_ref, s1_ref), (e0_ref, e1_ref), (al0_ref, al1_ref))
    row = lax.broadcasted_iota(jnp.int32, (LANES, tq), 0)
    for p in range(MLA_HEADS // 2):
        o_t = jnp.where(row < MLA_V, acc_ref[2 * p] / l_ref[2 * p], acc_ref[2 * p + 1] / l_ref[2 * p + 1])
        o_ref[:, p * LANES:(p + 1) * LANES] = o_t.T.astype(bf16)


def _mla_attn(q3, k3, v3, tq):
    bsz, s, hq = q3.shape
    nq = s // tq
    hv = v3.shape[-1]
    kern = functools.partial(_mla_attn_kernel, tq=tq)
    return pl.pallas_call(
        kern,
        grid=(bsz, nq),
        in_specs=[
            pl.BlockSpec((None, tq, hq), lambda b, i: (b, i, 0)),
            pl.BlockSpec((None, s, hq), lambda b, i: (b, 0, 0)),
            pl.BlockSpec((None, s, hv), lambda b, i: (b, 0, 0)),
        ],
        out_specs=pl.BlockSpec((None, tq, hv), lambda b, i: (b, i, 0)),
        out_shape=jax.ShapeDtypeStruct((bsz, s, hv), bf16),
        scratch_shapes=_attn_scratch(MLA_HEADS, nq, hv, tq) + [pltpu.VMEM((2, tq, tq), f32)],
        compiler_params=_cparams(("arbitrary", "arbitrary")),
        name="mla_attn",
    )(q3, k3, v3)


def _split3(x):
    hi = x.astype(bf16)
    r1 = x - hi.astype(f32)
    mid = r1.astype(bf16)
    lo = (r1 - mid.astype(f32)).astype(bf16)
    return hi, mid, lo


def _pair_bcast(x, j, lane):
    return jnp.where(lane < SSM_HEAD_DIM, x[:, 2 * j:2 * j + 1], x[:, 2 * j + 1:2 * j + 2])


def _ssd_kernel(xbc_ref, z_ref, misc_ref, cw_ref, cb_ref, dtb_ref, alog_ref, dvec_ref, nw_ref,
                o_ref, ext_ref, st_ref, y_ref):
    c = pl.program_id(1)
    L = SSM_CHUNK
    G, N = SSM_GROUPS, SSM_STATE

    @pl.when(c == 0)
    def _():
        ext_ref[0:SUBLANES, :] = jnp.zeros((SUBLANES, SSM_CONV_DIM), f32)
        st_ref[...] = jnp.zeros(st_ref.shape, f32)

    zs, dvec, nw = _silu(z_ref[...]), dvec_ref[...], nw_ref[...]
    cw, cbias = cw_ref[...], cb_ref[...]
    dt_in = misc_ref[...] + dtb_ref[...]
    alog = alog_ref[...]

    ext_ref[SUBLANES:, :] = xbc_ref[...]
    conv = cbias + cw[SSM_CONV - 1:SSM_CONV, :] * ext_ref[SUBLANES:, :]
    for s in range(1, SSM_CONV):
        conv = conv + cw[SSM_CONV - 1 - s:SSM_CONV - s, :] * ext_ref[SUBLANES - s:SUBLANES - s + L, :]
    ext_ref[0:SUBLANES, :] = ext_ref[L:L + SUBLANES, :]
    xa = _silu(conv)
    xs = xa[:, :SSM_D_INNER]
    bm = xa[:, SSM_D_INNER:SSM_D_INNER + G * N]
    cm = xa[:, SSM_D_INNER + G * N:]

    lane = _lane_iota((L, LANES))
    lane1 = _lane_iota((1, LANES))
    dt = jax.nn.softplus(dt_in)
    a_l = jnp.where(lane1 < SSM_HEADS, -jnp.exp(alog), 0.0)
    da = dt * a_l
    row = lax.broadcasted_iota(jnp.int32, (L, L), 0)
    col = lax.broadcasted_iota(jnp.int32, (L, L), 1)
    causal = col <= row
    tri = jnp.where(causal, 1.0, 0.0).astype(bf16)
    hi, mid, lo = _split3(da)
    cs = _dot(tri, hi) + _dot(tri, mid) + _dot(tri, lo)
    cs_t = cs.T
    cs_last = cs[L - 1:L, :]
    w_end = dt * jnp.exp(cs_last - cs)
    e_cs = jnp.exp(cs)
    cdec = jnp.exp(cs_last)

    for g in range(G):
        bm_g = bm[:, g * N:(g + 1) * N]
        cm_g = cm[:, g * N:(g + 1) * N].astype(bf16)
        cb = _dot_nt(cm_g, bm_g.astype(bf16))
        bm_t = bm_g.T.astype(bf16)
        for jj in range(2):
            j = 2 * g + jj
            sl = slice(j * LANES, (j + 1) * LANES)
            psl = slice(jj * LANES, (jj + 1) * LANES)
            xs_p = xs[:, sl]
            xdt = (xs_p * _pair_bcast(dt, j, lane)).astype(bf16)
            yd = []
            for hh in (2 * j, 2 * j + 1):
                seg = cs[:, hh:hh + 1] - cs_t[hh:hh + 1, :]
                decay = jnp.exp(jnp.where(causal, seg, -jnp.inf))
                yd.append(_dot((cb * decay).astype(bf16), xdt))
            y_diag = jnp.where(lane < SSM_HEAD_DIM, yd[0], yd[1])
            prev = st_ref[g, :, psl]
            y_off = _dot(cm_g, prev.astype(bf16)) * _pair_bcast(e_cs, j, lane)
            y_ref[:, sl] = y_diag + y_off + dvec[:, sl] * xs_p
            xw = (xs_p * _pair_bcast(w_end, j, lane)).astype(bf16)
            cd = jnp.where(lane1 < SSM_HEAD_DIM, cdec[:, 2 * j:2 * j + 1], cdec[:, 2 * j + 1:2 * j + 2])
            st_ref[g, :, psl] = prev * cd + _dot(bm_t, xw)

    gw = SSM_D_INNER // G
    for g in range(G):
        sl = slice(g * gw, (g + 1) * gw)
        yz = y_ref[:, sl] * zs[:, sl]
        yz = yz * lax.rsqrt(jnp.mean(yz * yz, axis=-1, keepdims=True) + EPS) * nw[:, sl]
        o_ref[:, sl] = yz.astype(bf16)


def _ssd(rest3, conv_w, conv_b, dt_bias, a_log, dvec, norm_w, l):
    bsz, s, _ = rest3.shape
    L = SSM_CHUNK
    nc = s // L
    misc_blk = (R_LAT + MLA_Q_RANK + MLA_KV_RANK) // LANES
    vec = lambda n: pl.BlockSpec((None, 1, n), lambda b, c: (l, 0, 0))
    return pl.pallas_call(
        _ssd_kernel,
        grid=(bsz, nc),
        in_specs=[
            pl.BlockSpec((None, L, SSM_CONV_DIM), lambda b, c: (b, c, R_XBC // SSM_CONV_DIM)),
            pl.BlockSpec((None, L, SSM_D_INNER), lambda b, c: (b, c, R_Z // SSM_D_INNER)),
            pl.BlockSpec((None, L, LANES), lambda b, c: (b, c, misc_blk)),
            pl.BlockSpec((None, SSM_CONV, SSM_CONV_DIM), lambda b, c: (l, 0, 0)),
            vec(SSM_CONV_DIM), vec(LANES), vec(LANES), vec(SSM_D_INNER), vec(SSM_D_INNER),
        ],
        out_specs=pl.BlockSpec((None, L, SSM_D_INNER), lambda b, c: (b, c, 0)),
        out_shape=jax.ShapeDtypeStruct((bsz, s, SSM_D_INNER), bf16),
        scratch_shapes=[
            pltpu.VMEM((L + SUBLANES, SSM_CONV_DIM), f32),
            pltpu.VMEM((SSM_GROUPS, SSM_STATE, 4 * SSM_HEAD_DIM), f32),
            pltpu.VMEM((L, SSM_D_INNER), f32),
        ],
        compiler_params=_cparams(("arbitrary", "arbitrary")),
        name="ssd",
    )(rest3, rest3, rest3, conv_w, conv_b, dt_bias, a_log, dvec, norm_w)


def _merge_kernel(ya_ref, yb_ref, ys_ref, ga_ref, gb_ref, gc_ref, x_ref, mod_ref,
                  wa_ref, wb_ref, wc_ref, wo_ref, o_ref):
    merged = (ga_ref[...] * _dot(ya_ref[...], wa_ref[...])
              + gb_ref[...] * _dot(yb_ref[...], wb_ref[...])
              + gc_ref[...] * _dot(ys_ref[...], wc_ref[...]))
    o_ref[...] = x_ref[...] + mod_ref[2:3, :] * _dot(merged.astype(bf16), wo_ref[...])


def _merge(ya, yb, ys, gqkv, x2, mod, wa, wb, wc, wo, l, rows_per_batch):
    t, d = x2.shape
    tm = min(512, rows_per_batch)
    tpb = rows_per_batch // tm
    row = lambda n: pl.BlockSpec((tm, n), lambda i: (i, 0))
    wspec = lambda k: pl.BlockSpec((None, k, d), lambda i: (l, 0, 0))
    return pl.pallas_call(
        _merge_kernel,
        grid=(t // tm,),
        in_specs=[
            row(ya.shape[1]), row(yb.shape[1]), row(ys.shape[1]),
            pl.BlockSpec((tm, d), lambda i: (i, 0)),
            pl.BlockSpec((tm, d), lambda i: (i, 1)),
            pl.BlockSpec((tm, d), lambda i: (i, 2)),
            row(d),
            pl.BlockSpec((None, None, 6, d), lambda i: (l, i // tpb, 0, 0)),
            wspec(ya.shape[1]), wspec(yb.shape[1]), wspec(ys.shape[1]), wspec(d),
        ],
        out_specs=row(d),
        out_shape=jax.ShapeDtypeStruct((t, d), f32),
        compiler_params=_cparams(("arbitrary",)),
        name="merge_out",
    )(ya, yb, ys, gqkv, gqkv, gqkv, x2, mod, wa, wb, wc, wo)


FFN_TF = 256
FFN_NF = FFN_DIM // FFN_TF
FFN_RB = 512


def _ffn_kernel(x_ref, nw_ref, mod_ref, wa_ref, wv_ref, cwa_ref, cwv_ref, cba_ref, cbv_ref, wd_ref,
                o_ref, h_ref, ua_ref, uv_ref, acc_ref):
    fi = pl.program_id(1)
    tm = x_ref.shape[0]

    @pl.when(fi == 0)
    def _():
        x = x_ref[...]
        y = x * lax.rsqrt(jnp.mean(x * x, axis=-1, keepdims=True) + EPS) * nw_ref[...]
        h_ref[...] = (y * (1.0 + mod_ref[4:5, :]) + mod_ref[3:4, :]).astype(bf16)
        acc_ref[...] = jnp.zeros(acc_ref.shape, f32)
        ua_ref[0:SUBLANES, :] = jnp.zeros((SUBLANES, FFN_TF), f32)
        uv_ref[0:SUBLANES, :] = jnp.zeros((SUBLANES, FFN_TF), f32)

    wa, wv, wd = wa_ref[...], wv_ref[...], wd_ref[...]
    cwa, cwv, cba, cbv = cwa_ref[...], cwv_ref[...], cba_ref[...], cbv_ref[...]

    def conv(u_ref, cw, cb, base):
        out = cb + cw[FFN_CONV - 1:FFN_CONV, :] * u_ref[base:base + FFN_RB, :]
        for s in range(1, FFN_CONV):
            out = out + cw[FFN_CONV - 1 - s:FFN_CONV - s, :] * u_ref[base - s:base - s + FFN_RB, :]
        return out

    def up(rb):
        base = SUBLANES + rb * FFN_RB
        hb = h_ref[rb * FFN_RB:(rb + 1) * FFN_RB, :]
        ua_ref[base:base + FFN_RB, :] = _dot(hb, wa)
        uv_ref[base:base + FFN_RB, :] = _dot(hb, wv)

    nrb = tm // FFN_RB
    up(0)
    for rb in range(nrb):
        if rb + 1 < nrb:
            up(rb + 1)
        base = SUBLANES + rb * FFN_RB
        act = (_silu(conv(ua_ref, cwa, cba, base)) * conv(uv_ref, cwv, cbv, base)).astype(bf16)
        acc_ref[rb * FFN_RB:(rb + 1) * FFN_RB, :] += _dot(act, wd)

    @pl.when(fi == pl.num_programs(1) - 1)
    def _():
        o_ref[...] = x_ref[...] + mod_ref[5:6, :] * acc_ref[...]


def _ffn(x3, norm_w, mod, wup, conv_w, conv_b, wdown, l):
    bsz, s, d = x3.shape
    tf, nf = FFN_TF, FFN_NF
    return pl.pallas_call(
        _ffn_kernel,
        grid=(bsz, nf),
        in_specs=[
            pl.BlockSpec((None, s, d), lambda b, f: (b, 0, 0)),
            pl.BlockSpec((None, 1, d), lambda b, f: (l, 0, 0)),
            pl.BlockSpec((None, None, 6, d), lambda b, f: (l, b, 0, 0)),
            pl.BlockSpec((None, d, tf), lambda b, f: (l, 0, f)),
            pl.BlockSpec((None, d, tf), lambda b, f: (l, 0, nf + f)),
            pl.BlockSpec((None, FFN_CONV, tf), lambda b, f: (l, 0, f)),
            pl.BlockSpec((None, FFN_CONV, tf), lambda b, f: (l, 0, nf + f)),
            pl.BlockSpec((None, 1, tf), lambda b, f: (l, 0, f)),
            pl.BlockSpec((None, 1, tf), lambda b, f: (l, 0, nf + f)),
            pl.BlockSpec((None, tf, d), lambda b, f: (l, f, 0)),
        ],
        out_specs=pl.BlockSpec((None, s, d), lambda b, f: (b, 0, 0)),
        out_shape=jax.ShapeDtypeStruct((bsz, s, d), f32),
        scratch_shapes=[
            pltpu.VMEM((s, d), bf16),
            pltpu.VMEM((s + SUBLANES, tf), f32), pltpu.VMEM((s + SUBLANES, tf), f32),
            pltpu.VMEM((s, d), f32),
        ],
        compiler_params=_cparams(("arbitrary", "arbitrary")),
        name="ffn",
    )(x3, norm_w, mod, wup, wup, conv_w, conv_w, conv_b, conv_b, wdown)


def _pack_w_in(w_in):
    nl, d, _ = w_in.shape
    sizes = (512, 512, 512, MLA_Q_RANK, MLA_KV_RANK, MLA_ROPE, SSM_D_INNER, SSM_CONV_DIM, SSM_HEADS, 3 * D_MODEL)
    offs = np.concatenate([[0], np.cumsum(sizes)])
    qa, ka, va, cq, ckv, kr, z, xbc, dt, gates = [w_in[..., int(offs[i]):int(offs[i + 1])] for i in range(10)]
    zeros = lambda n: jnp.zeros((nl, d, n), w_in.dtype)
    misc = jnp.concatenate([dt, zeros(ROPE_X1 - SSM_HEADS), kr[..., :HALF_ROPE], zeros(ROPE_X2 - ROPE_X1 - HALF_ROPE),
                            kr[..., HALF_ROPE:], zeros(LANES - ROPE_X2 - HALF_ROPE)], axis=-1)
    return jnp.concatenate([gates, qa, ka, va, xbc, z, cq, ckv, misc], axis=-1).astype(bf16)


def _pad_lanes(v, n):
    return jnp.pad(v, [(0, 0)] * (v.ndim - 1) + [(0, n - v.shape[-1])])


def _to_head_slot(v):
    return jnp.zeros(v.shape[:-1] + (LANES,), v.dtype).at[..., LANE_OF_DIM[:v.shape[-1]]].set(v)


def kernel(x, c, positions, rel_bias, ada_w, ada_b, norm_mix, norm_ffn, w_in, da_q_norm, da_k_norm, da_lambda, da_subln, mla_q_a_norm, mla_kv_a_norm, mla_w_uq, mla_w_ukv, mla_q_norm, mla_k_norm, ssm_conv_w, ssm_conv_b, ssm_dt_bias, ssm_a_log, ssm_d, ssm_norm, w_branch_a, w_branch_b, w_branch_c, w_out, ffn_w_up, ffn_conv_w, ffn_conv_b, ffn_w_down):
    bsz, seq, d = x.shape
    nl = w_in.shape[0]
    t = bsz * seq
    tq = min(ATT_TQ, seq)

    w_in_p = _pack_w_in(w_in)
    qk_w = jnp.stack([jnp.tile(da_q_norm, (1, 2 * DA_HEADS)) * (DA_HEAD_DIM ** -0.5 * LOG2E),
                      jnp.tile(da_k_norm, (1, 2 * DA_HEADS))], axis=1).reshape(nl, 2, 1, IN_TN)
    blk = np.arange(IN_TN) // DA_HEAD_DIM
    bd = jnp.asarray(blk[:, None] == blk[None, :], bf16)
    wuq = _to_head_slot(mla_w_uq.reshape(nl, MLA_Q_RANK, MLA_HEADS, MLA_QK))
    wuq = wuq.reshape(nl, MLA_Q_RANK, MLA_HEADS * LANES).astype(bf16)
    wukv = mla_w_ukv.reshape(nl, MLA_KV_RANK, MLA_HEADS, MLA_NOPE + MLA_V)
    wuk = _to_head_slot(wukv[..., :MLA_NOPE]).reshape(nl, MLA_KV_RANK, MLA_HEADS * LANES).astype(bf16)
    wuv = wukv[..., MLA_NOPE:].reshape(nl, MLA_KV_RANK, MLA_HEADS * MLA_V).astype(bf16)
    qn = (_to_head_slot(mla_q_norm) * (MLA_QK ** -0.5 * LOG2E)).reshape(nl, 1, LANES)
    kn = _to_head_slot(mla_k_norm).reshape(nl, 1, LANES)
    dt_bias = _pad_lanes(ssm_dt_bias, LANES).reshape(nl, 1, LANES)
    a_log = _pad_lanes(ssm_a_log, LANES).reshape(nl, 1, LANES)
    dvec = jnp.repeat(ssm_d, SSM_HEAD_DIM, axis=-1).reshape(nl, 1, SSM_D_INNER)
    v3 = lambda a: a.reshape(nl, 1, a.shape[-1])
    wa, wb, wc, wo = (w.astype(bf16) for w in (w_branch_a, w_branch_b, w_branch_c, w_out))
    wup, wdown = ffn_w_up.astype(bf16), ffn_w_down.astype(bf16)

    mod = _ada_mod(c, ada_w, ada_b).reshape(nl, bsz, 6, d)
    ctab, stab = _rope_tables(positions)
    bias = _bias_tiles(rel_bias, tq)

    x2 = x.reshape(t, d)
    for l in range(nl):
        lam_init = 0.8 - 0.6 * math.exp(-0.3 * l)
        gqkv, rest = _inproj(x2, v3(norm_mix), mod, w_in_p, qk_w, bd, l, seq)
        q_b, k_b, v_b = _mla_prep(rest, v3(mla_q_a_norm), v3(mla_kv_a_norm), wuq, wuk, wuv, qn, kn, ctab, stab, l)
        ya = _diff_attn(gqkv.reshape(bsz, seq, GQKV_COLS), bias, da_lambda, v3(da_subln), l, lam_init)
        yb = _mla_attn(q_b.reshape(bsz, seq, -1), k_b.reshape(bsz, seq, -1), v_b.reshape(bsz, seq, -1),
                       min(MLA_TQ, seq))
        ys = _ssd(rest.reshape(bsz, seq, REST_COLS), ssm_conv_w, v3(ssm_conv_b), dt_bias, a_log, dvec,
                  v3(ssm_norm), l)
        x2 = _merge(ya.reshape(t, -1), yb.reshape(t, -1), ys.reshape(t, -1), gqkv, x2, mod, wa, wb, wc, wo, l, seq)
        x2 = _ffn(x2.reshape(bsz, seq, d), v3(norm_ffn), mod, wup, ffn_conv_w, v3(ffn_conv_b), wdown, l).reshape(t, d)
    return x2.reshape(bsz, seq, d)
```

```python
import functools
import math

import jax
import jax.numpy as jnp
import numpy as np
from jax import lax
from jax.experimental import pallas as pl
from jax.experimental.pallas import tpu as pltpu

f32 = jnp.float32
bf16 = jnp.bfloat16

D_MODEL = 1024
DA_HEADS = 4
DA_HEAD_DIM = 64
MLA_HEADS = 8
MLA_Q_RANK = 256
MLA_KV_RANK = 128
MLA_NOPE = 64
MLA_ROPE = 32
MLA_V = 64
MLA_QK = MLA_NOPE + MLA_ROPE
SSM_D_INNER = 1024
SSM_HEAD_DIM = 64
SSM_HEADS = 16
SSM_GROUPS = 4
SSM_STATE = 128
SSM_CONV = 4
SSM_CHUNK = 128
SSM_CONV_DIM = SSM_D_INNER + 2 * SSM_GROUPS * SSM_STATE
FFN_DIM = 2816
FFN_CONV = 3
REL_BUCKETS = 32
REL_MAX_DIST = 128
ROPE_THETA = 10000.0
EPS = 1e-6

LANES = 128
SUBLANES = 8
VMEM_LIMIT = 56 * 1024 * 1024

IN_TN = 512
G_COLS = 3 * D_MODEL
GQKV_COLS = G_COLS + 3 * 512
R_XBC = 0
R_Z = R_XBC + SSM_CONV_DIM
R_LAT = R_Z + SSM_D_INNER
REST_COLS = R_LAT + 512
ROPE_X1 = 32
ROPE_X2 = ROPE_X1 + LANES // 2
HALF_ROPE = MLA_ROPE // 2
LANE_OF_DIM = np.concatenate([np.arange(0, 32), np.arange(48, 80),
                              np.arange(ROPE_X1, ROPE_X1 + HALF_ROPE), np.arange(ROPE_X2, ROPE_X2 + HALF_ROPE)])
NEG_BIG = -1e30
LOG2E = math.log2(math.e)

ATT_TQ = 256
MLA_TQ = 256


def _cparams(sem):
    return pltpu.CompilerParams(dimension_semantics=sem, vmem_limit_bytes=VMEM_LIMIT)


def _lane_iota(shape):
    return lax.broadcasted_iota(jnp.int32, shape, len(shape) - 1)


def _dot(a, b):
    return jnp.dot(a, b, preferred_element_type=f32)


def _dot_nt(a, b):
    return lax.dot_general(a, b, (((1,), (1,)), ((), ())), preferred_element_type=f32)


def _sigmoid(x):
    return 1.0 / (1.0 + jnp.exp(-x))


def _silu(x):
    return x * _sigmoid(x)


def _ada_kernel(c_ref, w_ref, b_ref, o_ref):
    ca = _silu(c_ref[...]).astype(bf16)
    o_ref[...] = _dot(ca, w_ref[...].astype(bf16)) + b_ref[...]


def _ada_mod(c, ada_w, ada_b):
    nl, d, n = ada_w.shape
    bsz = c.shape[0]
    tn = 1024
    return pl.pallas_call(
        _ada_kernel,
        grid=(nl, n // tn),
        in_specs=[
            pl.BlockSpec((bsz, d), lambda l, j: (0, 0)),
            pl.BlockSpec((None, d, tn), lambda l, j: (l, 0, j)),
            pl.BlockSpec((None, 1, tn), lambda l, j: (l, 0, j)),
        ],
        out_specs=pl.BlockSpec((None, bsz, tn), lambda l, j: (l, 0, j)),
        out_shape=jax.ShapeDtypeStruct((nl, bsz, n), f32),
        compiler_params=_cparams(("arbitrary", "arbitrary")),
        name="ada_mod",
    )(c, ada_w, ada_b.reshape(nl, 1, n))


def _rope_kernel(pos_ref, freq_ref, c_ref, s_ref):
    ang = pos_ref[...].astype(f32) * freq_ref[...]
    cos, sin = jnp.cos(ang), jnp.sin(ang)
    lane = _lane_iota(ang.shape)
    first = (lane >= ROPE_X1) & (lane < ROPE_X1 + HALF_ROPE)
    second = (lane >= ROPE_X2) & (lane < ROPE_X2 + HALF_ROPE)
    c_ref[...] = jnp.where(first | second, cos, 1.0)
    s_ref[...] = jnp.where(first, -sin, jnp.where(second, sin, 0.0))


def _rope_tables(positions):
    t = positions.size
    tm = min(t, 2048)
    inv_freq = ROPE_THETA ** (-jnp.arange(0, MLA_ROPE, 2, dtype=f32) / MLA_ROPE)
    freq = jnp.zeros((LANES,), f32).at[LANE_OF_DIM[MLA_NOPE:]].set(jnp.tile(inv_freq, 2))
    out = jax.ShapeDtypeStruct((t, LANES), f32)
    spec = pl.BlockSpec((tm, LANES), lambda i: (i, 0))
    return pl.pallas_call(
        _rope_kernel,
        grid=(t // tm,),
        in_specs=[spec, pl.BlockSpec((1, LANES), lambda i: (0, 0))],
        out_specs=[spec, spec],
        out_shape=[out, out],
        compiler_params=_cparams(("arbitrary",)),
        name="rope_tables",
    )(jnp.broadcast_to(positions.reshape(t, 1), (t, LANES)), freq.reshape(1, LANES))


def _bias_kernel(rb_ref, o_ref):
    h, which = pl.program_id(0), pl.program_id(1)
    tq = o_ref.shape[-1]
    row = lax.broadcasted_iota(jnp.int32, (tq, tq), 0)
    col = lax.broadcasted_iota(jnp.int32, (tq, tq), 1)
    rel = col - row + which * tq
    d = jnp.maximum(rel, 0)
    max_exact = REL_BUCKETS // 2
    large = max_exact + (jnp.log(jnp.maximum(d, 1).astype(f32) / max_exact)
                         / math.log(REL_MAX_DIST / max_exact) * (REL_BUCKETS - max_exact)).astype(jnp.int32)
    large = jnp.minimum(large, REL_BUCKETS - 1)
    bucket = jnp.where(d < max_exact, d, large)
    val = jnp.full((tq, tq), rb_ref[0, h], f32)
    for k in range(1, REL_BUCKETS):
        val = jnp.where(bucket == k, rb_ref[k, h], val)
    val = (val - rb_ref[REL_BUCKETS - 1, h]) * LOG2E
    val = jnp.where(rel < 0, NEG_BIG, val)
    o_ref[...] = jnp.where(which >= 2, 0.0, val)


def _bias_tiles(rel_bias, tq):
    return pl.pallas_call(
        _bias_kernel,
        grid=(DA_HEADS, 3),
        in_specs=[pl.BlockSpec(memory_space=pltpu.SMEM)],
        out_specs=pl.BlockSpec((None, None, tq, tq), lambda h, w: (h, w, 0, 0)),
        out_shape=jax.ShapeDtypeStruct((DA_HEADS, 3, tq, tq), f32),
        compiler_params=_cparams(("arbitrary", "arbitrary")),
        name="bias_tiles",
    )(rel_bias)


def _inproj_kernel(x_ref, nw_ref, mod_ref, w_ref, qkw_ref, bd_ref, gqkv_ref, rest_ref, h_ref):
    j = pl.program_id(1)
    jq = G_COLS // IN_TN

    @pl.when(j == 0)
    def _():
        x = x_ref[...]
        y = x * lax.rsqrt(jnp.mean(x * x, axis=-1, keepdims=True) + EPS) * nw_ref[...]
        h_ref[...] = (y * (1.0 + mod_ref[1:2, :]) + mod_ref[0:1, :]).astype(bf16)

    def proj():
        return _dot(h_ref[...], w_ref[...].astype(bf16))

    @pl.when(j < jq)
    def _():
        gqkv_ref[...] = (0.5 * jnp.tanh(0.5 * proj()) + 0.5).astype(bf16)

    @pl.when((j >= jq) & (j < jq + 2))
    def _():
        acc = proj()
        ss = _dot((acc * acc).astype(bf16), bd_ref[...]) * (1.0 / DA_HEAD_DIM)
        gqkv_ref[...] = (acc * lax.rsqrt(ss + EPS) * qkw_ref[...]).astype(bf16)

    @pl.when(j == jq + 2)
    def _():
        gqkv_ref[...] = proj().astype(bf16)

    @pl.when(j > jq + 2)
    def _():
        rest_ref[...] = proj()


def _inproj(x2, norm_w, mod, w_p, qk_w, bd, l, rows_per_batch):
    t, d = x2.shape
    tm = rows_per_batch
    nj = (GQKV_COLS + REST_COLS) // IN_TN
    nb = GQKV_COLS // IN_TN
    jq = G_COLS // IN_TN
    return pl.pallas_call(
        _inproj_kernel,
        grid=(t // tm, nj),
        in_specs=[
            pl.BlockSpec((tm, d), lambda i, j: (i, 0)),
            pl.BlockSpec((None, 1, d), lambda i, j: (l, 0, 0)),
            pl.BlockSpec((None, None, 6, d), lambda i, j: (l, i, 0, 0)),
            pl.BlockSpec((None, d, IN_TN), lambda i, j: (l, 0, j)),
            pl.BlockSpec((None, None, 1, IN_TN), lambda i, j: (l, jnp.clip(j - jq, 0, 1), 0, 0)),
            pl.BlockSpec((IN_TN, IN_TN), lambda i, j: (0, 0)),
        ],
        out_specs=[
            pl.BlockSpec((tm, IN_TN), lambda i, j: (i, jnp.minimum(j, nb - 1))),
            pl.BlockSpec((tm, IN_TN), lambda i, j: (i, jnp.maximum(j - nb, 0))),
        ],
        out_shape=[jax.ShapeDtypeStruct((t, GQKV_COLS), bf16), jax.ShapeDtypeStruct((t, REST_COLS), f32)],
        scratch_shapes=[pltpu.VMEM((tm, d), bf16)],
        compiler_params=_cparams(("arbitrary", "arbitrary")),
        name="inproj",
    )(x2, norm_w, mod, w_p, qk_w, bd)


def _rope_apply(x, c, s):
    return x * c + pltpu.roll(x, LANES // 2, 1) * s


def _mla_prep_kernel(lat_ref, qan_ref, kvan_ref, wuq_ref, wuk_ref, wuv_ref, qn_ref, kn_ref,
                     c_ref, s_ref, q_ref, k_ref, v_ref):
    lat = lat_ref[...]
    c, s = c_ref[...], s_ref[...]
    cq = lat[:, :MLA_Q_RANK]
    cqn = (cq * lax.rsqrt(jnp.mean(cq * cq, axis=-1, keepdims=True) + EPS) * qan_ref[...]).astype(bf16)
    ckv = lat[:, MLA_Q_RANK:MLA_Q_RANK + MLA_KV_RANK]
    ckvn = (ckv * lax.rsqrt(jnp.mean(ckv * ckv, axis=-1, keepdims=True) + EPS) * kvan_ref[...]).astype(bf16)
    misc = lat[:, MLA_Q_RANK + MLA_KV_RANK:]
    lane = _lane_iota(misc.shape)
    in_rope = (((lane >= ROPE_X1) & (lane < ROPE_X1 + HALF_ROPE))
               | ((lane >= ROPE_X2) & (lane < ROPE_X2 + HALF_ROPE)))
    krb = jnp.where(in_rope, misc, 0.0)
    qf = _dot(cqn, wuq_ref[...])
    kf = _dot(ckvn, wuk_ref[...])
    v_ref[...] = _dot(ckvn, wuv_ref[...]).astype(bf16)
    inv = 1.0 / MLA_QK
    qn, kn = qn_ref[...], kn_ref[...]
    for h in range(MLA_HEADS):
        sl = slice(h * LANES, (h + 1) * LANES)
        qh = qf[:, sl]
        qh = qh * lax.rsqrt(jnp.sum(qh * qh, axis=-1, keepdims=True) * inv + EPS) * qn
        q_ref[:, sl] = _rope_apply(qh, c, s).astype(bf16)
        kh = kf[:, sl] + krb
        kh = kh * lax.rsqrt(jnp.sum(kh * kh, axis=-1, keepdims=True) * inv + EPS) * kn
        k_ref[:, sl] = _rope_apply(kh, c, s).astype(bf16)


def _mla_prep(rest, qan, kvan, wuq, wuk, wuv, qn, kn, ctab, stab, l):
    t = rest.shape[0]
    tm = min(512, t)
    lat_blk = R_LAT // 512
    hq = MLA_HEADS * LANES
    vec = lambda n: pl.BlockSpec((None, 1, n), lambda i: (l, 0, 0))
    tab = pl.BlockSpec((tm, LANES), lambda i: (i, 0))
    return pl.pallas_call(
        _mla_prep_kernel,
        grid=(t // tm,),
        in_specs=[
            pl.BlockSpec((tm, 512), lambda i: (i, lat_blk)),
            vec(MLA_Q_RANK), vec(MLA_KV_RANK),
            pl.BlockSpec((None, MLA_Q_RANK, hq), lambda i: (l, 0, 0)),
            pl.BlockSpec((None, MLA_KV_RANK, hq), lambda i: (l, 0, 0)),
            pl.BlockSpec((None, MLA_KV_RANK, MLA_HEADS * MLA_V), lambda i: (l, 0, 0)),
            vec(LANES), vec(LANES), tab, tab,
        ],
        out_specs=[
            pl.BlockSpec((tm, hq), lambda i: (i, 0)),
            pl.BlockSpec((tm, hq), lambda i: (i, 0)),
            pl.BlockSpec((tm, MLA_HEADS * MLA_V), lambda i: (i, 0)),
        ],
        out_shape=[jax.ShapeDtypeStruct((t, hq), bf16), jax.ShapeDtypeStruct((t, hq), bf16),
                   jax.ShapeDtypeStruct((t, MLA_HEADS * MLA_V), bf16)],
        compiler_params=_cparams(("arbitrary",)),
        name="mla_prep",
    )(rest, qan, kvan, wuq, wuk, wuv, qn, kn, ctab, stab)


def _flash_t(q_list, k_ref, kcols, vt_ref, vrows, tile_fn, ntiles, qi, tq,
             m_ref, l_ref, acc_ref, s_refs, e_refs, al_refs):
    nmap = len(q_list)
    m_ref[...] = jnp.full(m_ref.shape, -jnp.inf, f32)
    l_ref[...] = jnp.zeros(l_ref.shape, f32)
    acc_ref[...] = jnp.zeros(acc_ref.shape, f32)
    last = qi

    def qk(t, slot):
        start = pl.multiple_of(t * tq, tq)
        w = jnp.minimum(qi - t, ntiles - 1)
        vals = []
        for a in range(nmap):
            k = k_ref[pl.ds(start, tq), kcols[a]:kcols[a] + LANES]
            vals.append(_dot_nt(k, q_list[a]) + tile_fn(a, w))
        for a in range(nmap):
            s_refs[slot][a] = vals[a]

    def sm(slot):
        for a in range(nmap):
            s = s_refs[slot][a]
            m_prev = m_ref[a]
            m_new = jnp.maximum(m_prev, jnp.max(s, axis=0, keepdims=True))
            alpha = jnp.exp2(m_prev - m_new)
            e = jnp.exp2(s - m_new)
            l_ref[a] = alpha * l_ref[a] + jnp.sum(e, axis=0, keepdims=True)
            m_ref[a] = m_new
            e_refs[slot][a] = e.astype(bf16)
            al_refs[slot][a] = alpha

    def pv(t, slot):
        for a in range(nmap):
            vt = vt_ref[t, vrows[a]:vrows[a] + LANES, :]
            acc_ref[a] = al_refs[slot][a] * acc_ref[a] + _dot(vt, e_refs[slot][a])

    def body(t, cur):
        qk(jnp.minimum(t + 1, last), 1 - cur)
        pv(t - 1, 1 - cur)
        sm(cur)

    qk(0, 0)
    qk(jnp.minimum(1, last), 1)
    sm(0)

    def pair(i, carry):
        t = 2 * i + 1
        body(t, 1)
        body(t + 1, 0)
        return carry

    lax.fori_loop(0, last // 2, pair, 0)
    odd = lax.rem(last, 2) == 1

    @pl.when(odd)
    def _():
        body(last, 1)
        pv(last, 1)

    @pl.when(jnp.logical_not(odd))
    def _():
        pv(last, 0)


def _fill_vt(v_ref, vt_ref, tq):
    for c in range(vt_ref.shape[0]):
        vt_ref[c] = v_ref[c * tq:(c + 1) * tq, :].astype(f32).T.astype(bf16)


def _attn_scratch(nmap, nq, hv, tq):
    sbuf = pltpu.VMEM((nmap, tq, tq), f32)
    ebuf = pltpu.VMEM((nmap, tq, tq), bf16)
    row = pltpu.VMEM((nmap, 1, tq), f32)
    return [pltpu.VMEM((nq, hv, tq), bf16), row, row, pltpu.VMEM((nmap, LANES, tq), f32),
            sbuf, sbuf, ebuf, ebuf, row, row]


def _diff_attn_kernel(q_ref, k_ref, v_ref, bias_ref, lp_ref, sub_ref, o_ref,
                      vt_ref, m_ref, l_ref, acc_ref, s0_ref, s1_ref, e0_ref, e1_ref, al0_ref, al1_ref,
                      *, lam_init, tq):
    qi = pl.program_id(1)

    @pl.when(qi == 0)
    def _():
        _fill_vt(v_ref, vt_ref, tq)

    lane = _lane_iota((tq, LANES))
    q_list, kcols, vrows = [], [], []
    for h in range(DA_HEADS):
        q = q_ref[:, h * LANES:(h + 1) * LANES]
        zero = jnp.zeros_like(q)
        q_list += [jnp.where(lane < DA_HEAD_DIM, q, zero), jnp.where(lane >= DA_HEAD_DIM, q, zero)]
        kcols += [h * LANES, h * LANES]
        vrows += [h * LANES, h * LANES]
    lp, sub = lp_ref[...], sub_ref[...]
    _flash_t(q_list, k_ref, kcols, vt_ref, vrows, lambda a, w: bias_ref[a // 2, w], 3, qi, tq,
             m_ref, l_ref, acc_ref, (s0_ref, s1_ref), (e0_ref, e1_ref), (al0_ref, al1_ref))
    lam = (jnp.exp(jnp.sum(lp[0:1] * lp[1:2], axis=-1, keepdims=True))
           - jnp.exp(jnp.sum(lp[2:3] * lp[3:4], axis=-1, keepdims=True)) + lam_init)
    for h in range(DA_HEADS):
        o_t = acc_ref[2 * h] / l_ref[2 * h] - lam * (acc_ref[2 * h + 1] / l_ref[2 * h + 1])
        o = o_t.T
        o = o * lax.rsqrt(jnp.mean(o * o, axis=-1, keepdims=True) + EPS) * sub
        o_ref[:, h * LANES:(h + 1) * LANES] = (o * (1.0 - lam_init)).astype(bf16)


def _diff_attn(qkv3, bias, da_lambda, da_subln, l, lam_init):
    bsz, s, _ = qkv3.shape
    tq = bias.shape[-1]
    nq = s // tq
    hw = DA_HEADS * LANES
    qblk = G_COLS // hw
    kern = functools.partial(_diff_attn_kernel, lam_init=lam_init, tq=tq)
    return pl.pallas_call(
        kern,
        grid=(bsz, nq),
        in_specs=[
            pl.BlockSpec((None, tq, hw), lambda b, i: (b, i, qblk)),
            pl.BlockSpec((None, s, hw), lambda b, i: (b, 0, qblk + 1)),
            pl.BlockSpec((None, s, hw), lambda b, i: (b, 0, qblk + 2)),
            pl.BlockSpec((DA_HEADS, 3, tq, tq), lambda b, i: (0, 0, 0, 0)),
            pl.BlockSpec((None, 4, DA_HEAD_DIM), lambda b, i: (l, 0, 0)),
            pl.BlockSpec((None, 1, LANES), lambda b, i: (l, 0, 0)),
        ],
        out_specs=pl.BlockSpec((None, tq, hw), lambda b, i: (b, i, 0)),
        out_shape=jax.ShapeDtypeStruct((bsz, s, hw), bf16),
        scratch_shapes=_attn_scratch(2 * DA_HEADS, nq, hw, tq),
        compiler_params=_cparams(("arbitrary", "arbitrary")),
        name="diff_attn",
    )(qkv3, qkv3, qkv3, bias, da_lambda, da_subln)


def _mla_attn_kernel(q_ref, k_ref, v_ref, o_ref,
                     vt_ref, m_ref, l_ref, acc_ref, s0_ref, s1_ref, e0_ref, e1_ref, al0_ref, al1_ref, mask_ref,
                     *, tq):
    qi = pl.program_id(1)

    @pl.when(qi == 0)
    def _():
        _fill_vt(v_ref, vt_ref, tq)
        krow = lax.broadcasted_iota(jnp.int32, (tq, tq), 0)
        qcol = lax.broadcasted_iota(jnp.int32, (tq, tq), 1)
        mask_ref[0] = jnp.where(krow <= qcol, 0.0, NEG_BIG)
        mask_ref[1] = jnp.zeros((tq, tq), f32)

    q_list = [q_ref[:, h * LANES:(h + 1) * LANES] for h in range(MLA_HEADS)]
    kcols = [h * LANES for h in range(MLA_HEADS)]
    vrows = [(h // 2) * LANES for h in range(MLA_HEADS)]
    _flash_t(q_list, k_ref, kcols, vt_ref, vrows, lambda a, w: mask_ref[w], 2, qi, tq,
             m_ref, l_ref, acc_ref, (s0_ref, s1_ref), (e0_ref, e1_ref), (al0_ref, al1_ref))
    row = lax.broadcasted_iota(jnp.int32, (LANES, tq), 0)
    for p in range(MLA_HEADS // 2):
        o_t = jnp.where(row < MLA_V, acc_ref[2 * p] / l_ref[2 * p], acc_ref[2 * p + 1] / l_ref[2 * p + 1])
        o_ref[:, p * LANES:(p + 1) * LANES] = o_t.T.astype(bf16)


def _mla_attn(q3, k3, v3, tq):
    bsz, s, hq = q3.shape
    nq = s // tq
    hv = v3.shape[-1]
    kern = functools.partial(_mla_attn_kernel, tq=tq)
    return pl.pallas_call(
        kern,
        grid=(bsz, nq),
        in_specs=[
            pl.BlockSpec((None, tq, hq), lambda b, i: (b, i, 0)),
            pl.BlockSpec((None, s, hq), lambda b, i: (b, 0, 0)),
            pl.BlockSpec((None, s, hv), lambda b, i: (b, 0, 0)),
        ],
        out_specs=pl.BlockSpec((None, tq, hv), lambda b, i: (b, i, 0)),
        out_shape=jax.ShapeDtypeStruct((bsz, s, hv), bf16),
        scratch_shapes=_attn_scratch(MLA_HEADS, nq, hv, tq) + [pltpu.VMEM((2, tq, tq), f32)],
        compiler_params=_cparams(("arbitrary", "arbitrary")),
        name="mla_attn",
    )(q3, k3, v3)


def _split3(x):
    hi = x.astype(bf16)
    r1 = x - hi.astype(f32)
    mid = r1.astype(bf16)
    lo = (r1 - mid.astype(f32)).astype(bf16)
    return hi, mid, lo


def _pair_bcast(x, j, lane):
    return jnp.where(lane < SSM_HEAD_DIM, x[:, 2 * j:2 * j + 1], x[:, 2 * j + 1:2 * j + 2])


def _ssd_kernel(xbc_ref, z_ref, misc_ref, cw_ref, cb_ref, dtb_ref, alog_ref, dvec_ref, nw_ref,
                o_ref, ext_ref, st_ref, y_ref):
    c = pl.program_id(1)
    L = SSM_CHUNK
    G, N = SSM_GROUPS, SSM_STATE

    @pl.when(c == 0)
    def _():
        ext_ref[0:SUBLANES, :] = jnp.zeros((SUBLANES, SSM_CONV_DIM), f32)
        st_ref[...] = jnp.zeros(st_ref.shape, f32)

    zs, dvec, nw = _silu(z_ref[...]), dvec_ref[...], nw_ref[...]
    cw, cbias = cw_ref[...], cb_ref[...]
    dt_in = misc_ref[...] + dtb_ref[...]
    alog = alog_ref[...]

    ext_ref[SUBLANES:, :] = xbc_ref[...]
    def rows8(v):
        return v.reshape(L // SUBLANES, SUBLANES, SSM_CONV_DIM)

    conv = cbias + cw[SSM_CONV - 1] * rows8(ext_ref[SUBLANES:, :])
    for s in range(1, SSM_CONV):
        conv = conv + cw[SSM_CONV - 1 - s] * rows8(ext_ref[SUBLANES - s:SUBLANES - s + L, :])
    ext_ref[0:SUBLANES, :] = ext_ref[L:L + SUBLANES, :]
    xa = _silu(conv).reshape(L, SSM_CONV_DIM)
    xs = xa[:, :SSM_D_INNER]
    bm = xa[:, SSM_D_INNER:SSM_D_INNER + G * N]
    cm = xa[:, SSM_D_INNER + G * N:]

    lane = _lane_iota((L, LANES))
    lane1 = _lane_iota((1, LANES))
    dt = jax.nn.softplus(dt_in)
    a_l = jnp.where(lane1 < SSM_HEADS, -jnp.exp(alog), 0.0)
    da = dt * a_l
    row = lax.broadcasted_iota(jnp.int32, (L, L), 0)
    col = lax.broadcasted_iota(jnp.int32, (L, L), 1)
    causal = col <= row
    tri = jnp.where(causal, 1.0, 0.0).astype(bf16)
    hi, mid, lo = _split3(da)
    cs = _dot(tri, hi) + _dot(tri, mid) + _dot(tri, lo)
    cs_t = cs.T
    cs_last = cs[L - 1:L, :]
    w_end = dt * jnp.exp(cs_last - cs)
    e_cs = jnp.exp(cs)
    cdec = jnp.exp(cs_last)

    for g in range(G):
        bm_g = bm[:, g * N:(g + 1) * N]
        cm_g = cm[:, g * N:(g + 1) * N].astype(bf16)
        cb = _dot_nt(cm_g, bm_g.astype(bf16))
        bm_t = bm_g.T.astype(bf16)
        for jj in range(2):
            j = 2 * g + jj
            sl = slice(j * LANES, (j + 1) * LANES)
            psl = slice(jj * LANES, (jj + 1) * LANES)
            xs_p = xs[:, sl]
            xdt = (xs_p * _pair_bcast(dt, j, lane)).astype(bf16)
            yd = []
            for hh in (2 * j, 2 * j + 1):
                seg = cs[:, hh:hh + 1] - cs_t[hh:hh + 1, :]
                decay = jnp.exp(jnp.where(causal, seg, -jnp.inf))
                yd.append(_dot((cb * decay).astype(bf16), xdt))
            y_diag = jnp.where(lane < SSM_HEAD_DIM, yd[0], yd[1])
            prev = st_ref[g, :, psl]
            y_off = _dot(cm_g, prev.astype(bf16)) * _pair_bcast(e_cs, j, lane)
            y_ref[:, sl] = y_diag + y_off + dvec[:, sl] * xs_p
            xw = (xs_p * _pair_bcast(w_end, j, lane)).astype(bf16)
            cd = jnp.where(lane1 < SSM_HEAD_DIM, cdec[:, 2 * j:2 * j + 1], cdec[:, 2 * j + 1:2 * j + 2])
            st_ref[g, :, psl] = prev * cd + _dot(bm_t, xw)

    gw = SSM_D_INNER // G
    for g in range(G):
        sl = slice(g * gw, (g + 1) * gw)
        yz = y_ref[:, sl] * zs[:, sl]
        yz = yz * lax.rsqrt(jnp.mean(yz * yz, axis=-1, keepdims=True) + EPS) * nw[:, sl]
        o_ref[:, sl] = yz.astype(bf16)


def _ssd(rest3, conv_w, conv_b, dt_bias, a_log, dvec, norm_w, l):
    bsz, s, _ = rest3.shape
    L = SSM_CHUNK
    nc = s // L
    misc_blk = (R_LAT + MLA_Q_RANK + MLA_KV_RANK) // LANES
    vec = lambda n: pl.BlockSpec((None, 1, n), lambda b, c: (l, 0, 0))
    return pl.pallas_call(
        _ssd_kernel,
        grid=(bsz, nc),
        in_specs=[
            pl.BlockSpec((None, L, SSM_CONV_DIM), lambda b, c: (b, c, R_XBC // SSM_CONV_DIM)),
            pl.BlockSpec((None, L, SSM_D_INNER), lambda b, c: (b, c, R_Z // SSM_D_INNER)),
            pl.BlockSpec((None, L, LANES), lambda b, c: (b, c, misc_blk)),
            pl.BlockSpec((None, SSM_CONV, SUBLANES, SSM_CONV_DIM), lambda b, c: (l, 0, 0, 0)),
            pl.BlockSpec((None, SUBLANES, SSM_CONV_DIM), lambda b, c: (l, 0, 0)),
            vec(LANES), vec(LANES), vec(SSM_D_INNER), vec(SSM_D_INNER),
        ],
        out_specs=pl.BlockSpec((None, L, SSM_D_INNER), lambda b, c: (b, c, 0)),
        out_shape=jax.ShapeDtypeStruct((bsz, s, SSM_D_INNER), bf16),
        scratch_shapes=[
            pltpu.VMEM((L + SUBLANES, SSM_CONV_DIM), f32),
            pltpu.VMEM((SSM_GROUPS, SSM_STATE, 4 * SSM_HEAD_DIM), f32),
            pltpu.VMEM((L, SSM_D_INNER), f32),
        ],
        compiler_params=_cparams(("arbitrary", "arbitrary")),
        name="ssd",
    )(rest3, rest3, rest3, conv_w, conv_b, dt_bias, a_log, dvec, norm_w)


def _merge_kernel(ya_ref, yb_ref, ys_ref, ga_ref, gb_ref, gc_ref, x_ref, mod_ref,
                  wa_ref, wb_ref, wc_ref, wo_ref, o_ref):
    merged = (ga_ref[...] * _dot(ya_ref[...], wa_ref[...])
              + gb_ref[...] * _dot(yb_ref[...], wb_ref[...])
              + gc_ref[...] * _dot(ys_ref[...], wc_ref[...]))
    o_ref[...] = x_ref[...] + mod_ref[2:3, :] * _dot(merged.astype(bf16), wo_ref[...])


def _merge(ya, yb, ys, gqkv, x2, mod, wa, wb, wc, wo, l, rows_per_batch):
    t, d = x2.shape
    tm = min(512, rows_per_batch)
    tpb = rows_per_batch // tm
    row = lambda n: pl.BlockSpec((tm, n), lambda i: (i, 0))
    wspec = lambda k: pl.BlockSpec((None, k, d), lambda i: (l, 0, 0))
    return pl.pallas_call(
        _merge_kernel,
        grid=(t // tm,),
        in_specs=[
            row(ya.shape[1]), row(yb.shape[1]), row(ys.shape[1]),
            pl.BlockSpec((tm, d), lambda i: (i, 0)),
            pl.BlockSpec((tm, d), lambda i: (i, 1)),
            pl.BlockSpec((tm, d), lambda i: (i, 2)),
            row(d),
            pl.BlockSpec((None, None, 6, d), lambda i: (l, i // tpb, 0, 0)),
            wspec(ya.shape[1]), wspec(yb.shape[1]), wspec(ys.shape[1]), wspec(d),
        ],
        out_specs=row(d),
        out_shape=jax.ShapeDtypeStruct((t, d), f32),
        compiler_params=_cparams(("arbitrary",)),
        name="merge_out",
    )(ya, yb, ys, gqkv, gqkv, gqkv, x2, mod, wa, wb, wc, wo)


FFN_TF = 256
FFN_NF = FFN_DIM // FFN_TF
FFN_RB = 512


def _ffn_kernel(x_ref, nw_ref, mod_ref, wa_ref, wv_ref, cwa_ref, cwv_ref, cba_ref, cbv_ref, wd_ref,
                o_ref, h_ref, ua_ref, uv_ref, acc_ref):
    fi = pl.program_id(1)
    tm = x_ref.shape[0]

    @pl.when(fi == 0)
    def _():
        x = x_ref[...]
        y = x * lax.rsqrt(jnp.mean(x * x, axis=-1, keepdims=True) + EPS) * nw_ref[...]
        h_ref[...] = (y * (1.0 + mod_ref[4:5, :]) + mod_ref[3:4, :]).astype(bf16)
        acc_ref[...] = jnp.zeros(acc_ref.shape, f32)
        ua_ref[0:SUBLANES, :] = jnp.zeros((SUBLANES, FFN_TF), f32)
        uv_ref[0:SUBLANES, :] = jnp.zeros((SUBLANES, FFN_TF), f32)

    wa, wv, wd = wa_ref[...], wv_ref[...], wd_ref[...]
    cwa, cwv, cba, cbv = cwa_ref[...], cwv_ref[...], cba_ref[...], cbv_ref[...]

    def conv(u_ref, cw, cb, base):
        out = cb + cw[FFN_CONV - 1:FFN_CONV, :] * u_ref[base:base + FFN_RB, :]
        for s in range(1, FFN_CONV):
            out = out + cw[FFN_CONV - 1 - s:FFN_CONV - s, :] * u_ref[base - s:base - s + FFN_RB, :]
        return out

    def up(rb):
        base = SUBLANES + rb * FFN_RB
        hb = h_ref[rb * FFN_RB:(rb + 1) * FFN_RB, :]
        ua_ref[base:base + FFN_RB, :] = _dot(hb, wa)
        uv_ref[base:base + FFN_RB, :] = _dot(hb, wv)

    nrb = tm // FFN_RB
    up(0)
    for rb in range(nrb):
        if rb + 1 < nrb:
            up(rb + 1)
        base = SUBLANES + rb * FFN_RB
        act = (_silu(conv(ua_ref, cwa, cba, base)) * conv(uv_ref, cwv, cbv, base)).astype(bf16)
        acc_ref[rb * FFN_RB:(rb + 1) * FFN_RB, :] += _dot(act, wd)

    @pl.when(fi == pl.num_programs(1) - 1)
    def _():
        o_ref[...] = x_ref[...] + mod_ref[5:6, :] * acc_ref[...]


def _ffn(x3, norm_w, mod, wup, conv_w, conv_b, wdown, l):
    bsz, s, d = x3.shape
    tf, nf = FFN_TF, FFN_NF
    return pl.pallas_call(
        _ffn_kernel,
        grid=(bsz, nf),
        in_specs=[
            pl.BlockSpec((None, s, d), lambda b, f: (b, 0, 0)),
            pl.BlockSpec((None, 1, d), lambda b, f: (l, 0, 0)),
            pl.BlockSpec((None, None, 6, d), lambda b, f: (l, b, 0, 0)),
            pl.BlockSpec((None, d, tf), lambda b, f: (l, 0, f)),
            pl.BlockSpec((None, d, tf), lambda b, f: (l, 0, nf + f)),
            pl.BlockSpec((None, FFN_CONV, tf), lambda b, f: (l, 0, f)),
            pl.BlockSpec((None, FFN_CONV, tf), lambda b, f: (l, 0, nf + f)),
            pl.BlockSpec((None, 1, tf), lambda b, f: (l, 0, f)),
            pl.BlockSpec((None, 1, tf), lambda b, f: (l, 0, nf + f)),
            pl.BlockSpec((None, tf, d), lambda b, f: (l, f, 0)),
        ],
        out_specs=pl.BlockSpec((None, s, d), lambda b, f: (b, 0, 0)),
        out_shape=jax.ShapeDtypeStruct((bsz, s, d), f32),
        scratch_shapes=[
            pltpu.VMEM((s, d), bf16),
            pltpu.VMEM((s + SUBLANES, tf), f32), pltpu.VMEM((s + SUBLANES, tf), f32),
            pltpu.VMEM((s, d), f32),
        ],
        compiler_params=_cparams(("arbitrary", "arbitrary")),
        name="ffn",
    )(x3, norm_w, mod, wup, wup, conv_w, conv_w, conv_b, conv_b, wdown)


def _pack_w_in(w_in):
    nl, d, _ = w_in.shape
    sizes = (512, 512, 512, MLA_Q_RANK, MLA_KV_RANK, MLA_ROPE, SSM_D_INNER, SSM_CONV_DIM, SSM_HEADS, 3 * D_MODEL)
    offs = np.concatenate([[0], np.cumsum(sizes)])
    qa, ka, va, cq, ckv, kr, z, xbc, dt, gates = [w_in[..., int(offs[i]):int(offs[i + 1])] for i in range(10)]
    zeros = lambda n: jnp.zeros((nl, d, n), w_in.dtype)
    misc = jnp.concatenate([dt, zeros(ROPE_X1 - SSM_HEADS), kr[..., :HALF_ROPE], zeros(ROPE_X2 - ROPE_X1 - HALF_ROPE),
                            kr[..., HALF_ROPE:], zeros(LANES - ROPE_X2 - HALF_ROPE)], axis=-1)
    return jnp.concatenate([gates, qa, ka, va, xbc, z, cq, ckv, misc], axis=-1)


def _pad_lanes(v, n):
    return jnp.pad(v, [(0, 0)] * (v.ndim - 1) + [(0, n - v.shape[-1])])


def _to_head_slot(v):
    return jnp.zeros(v.shape[:-1] + (LANES,), v.dtype).at[..., LANE_OF_DIM[:v.shape[-1]]].set(v)


def kernel(x, c, positions, rel_bias, ada_w, ada_b, norm_mix, norm_ffn, w_in, da_q_norm, da_k_norm, da_lambda, da_subln, mla_q_a_norm, mla_kv_a_norm, mla_w_uq, mla_w_ukv, mla_q_norm, mla_k_norm, ssm_conv_w, ssm_conv_b, ssm_dt_bias, ssm_a_log, ssm_d, ssm_norm, w_branch_a, w_branch_b, w_branch_c, w_out, ffn_w_up, ffn_conv_w, ffn_conv_b, ffn_w_down):
    bsz, seq, d = x.shape
    nl = w_in.shape[0]
    t = bsz * seq
    tq = min(ATT_TQ, seq)

    w_in_p = _pack_w_in(w_in)
    qk_w = jnp.stack([jnp.tile(da_q_norm, (1, 2 * DA_HEADS)) * (DA_HEAD_DIM ** -0.5 * LOG2E),
                      jnp.tile(da_k_norm, (1, 2 * DA_HEADS))], axis=1).reshape(nl, 2, 1, IN_TN)
    blk = np.arange(IN_TN) // DA_HEAD_DIM
    bd = jnp.asarray(blk[:, None] == blk[None, :], bf16)
    wuq = _to_head_slot(mla_w_uq.reshape(nl, MLA_Q_RANK, MLA_HEADS, MLA_QK))
    wuq = wuq.reshape(nl, MLA_Q_RANK, MLA_HEADS * LANES).astype(bf16)
    wukv = mla_w_ukv.reshape(nl, MLA_KV_RANK, MLA_HEADS, MLA_NOPE + MLA_V)
    wuk = _to_head_slot(wukv[..., :MLA_NOPE]).reshape(nl, MLA_KV_RANK, MLA_HEADS * LANES).astype(bf16)
    wuv = wukv[..., MLA_NOPE:].reshape(nl, MLA_KV_RANK, MLA_HEADS * MLA_V).astype(bf16)
    qn = (_to_head_slot(mla_q_norm) * (MLA_QK ** -0.5 * LOG2E)).reshape(nl, 1, LANES)
    kn = _to_head_slot(mla_k_norm).reshape(nl, 1, LANES)
    dt_bias = _pad_lanes(ssm_dt_bias, LANES).reshape(nl, 1, LANES)
    a_log = _pad_lanes(ssm_a_log, LANES).reshape(nl, 1, LANES)
    dvec = jnp.repeat(ssm_d, SSM_HEAD_DIM, axis=-1).reshape(nl, 1, SSM_D_INNER)
    conv_w8 = jnp.broadcast_to(ssm_conv_w[:, :, None, :], (nl, SSM_CONV, SUBLANES, SSM_CONV_DIM))
    conv_b8 = jnp.broadcast_to(ssm_conv_b[:, None, :], (nl, SUBLANES, SSM_CONV_DIM))
    v3 = lambda a: a.reshape(nl, 1, a.shape[-1])
    wa, wb, wc, wo = (w.astype(bf16) for w in (w_branch_a, w_branch_b, w_branch_c, w_out))
    wup, wdown = ffn_w_up.astype(bf16), ffn_w_down.astype(bf16)

    mod = _ada_mod(c, ada_w, ada_b).reshape(nl, bsz, 6, d)
    ctab, stab = _rope_tables(positions)
    bias = _bias_tiles(rel_bias, tq)

    x2 = x.reshape(t, d)
    for l in range(nl):
        lam_init = 0.8 - 0.6 * math.exp(-0.3 * l)
        gqkv, rest = _inproj(x2, v3(norm_mix), mod, w_in_p, qk_w, bd, l, seq)
        q_b, k_b, v_b = _mla_prep(rest, v3(mla_q_a_norm), v3(mla_kv_a_norm), wuq, wuk, wuv, qn, kn, ctab, stab, l)
        ya = _diff_attn(gqkv.reshape(bsz, seq, GQKV_COLS), bias, da_lambda, v3(da_subln), l, lam_init)
        yb = _mla_attn(q_b.reshape(bsz, seq, -1), k_b.reshape(bsz, seq, -1), v_b.reshape(bsz, seq, -1),
                       min(MLA_TQ, seq))
        ys = _ssd(rest.reshape(bsz, seq, REST_COLS), conv_w8, conv_b8, dt_bias, a_log, dvec, v3(ssm_norm), l)
        x2 = _merge(ya.reshape(t, -1), yb.reshape(t, -1), ys.reshape(t, -1), gqkv, x2, mod, wa, wb, wc, wo, l, seq)
        x2 = _ffn(x2.reshape(bsz, seq, d), v3(norm_ffn), mod, wup, ffn_conv_w, v3(ffn_conv_b), wdown, l).reshape(t, d)
    return x2.reshape(bsz, seq, d)
```

```python
import functools
import math

import jax
import jax.numpy as jnp
import numpy as np
from jax import lax
from jax.experimental import pallas as pl
from jax.experimental.pallas import tpu as pltpu

f32 = jnp.float32
bf16 = jnp.bfloat16

D_MODEL = 1024
DA_HEADS = 4
DA_HEAD_DIM = 64
MLA_HEADS = 8
MLA_Q_RANK = 256
MLA_KV_RANK = 128
MLA_NOPE = 64
MLA_ROPE = 32
MLA_V = 64
MLA_QK = MLA_NOPE + MLA_ROPE
SSM_D_INNER = 1024
SSM_HEAD_DIM = 64
SSM_HEADS = 16
SSM_GROUPS = 4
SSM_STATE = 128
SSM_CONV = 4
SSM_CHUNK = 128
SSM_CONV_DIM = SSM_D_INNER + 2 * SSM_GROUPS * SSM_STATE
FFN_DIM = 2816
FFN_CONV = 3
REL_BUCKETS = 32
REL_MAX_DIST = 128
ROPE_THETA = 10000.0
EPS = 1e-6

LANES = 128
SUBLANES = 8
VMEM_LIMIT = 56 * 1024 * 1024

IN_TN = 512
G_COLS = 3 * D_MODEL
GQKV_COLS = G_COLS + 3 * 512
R_XBC = 0
R_Z = R_XBC + SSM_CONV_DIM
R_LAT = R_Z + SSM_D_INNER
REST_COLS = R_LAT + 512
ROPE_X1 = 32
ROPE_X2 = ROPE_X1 + LANES // 2
HALF_ROPE = MLA_ROPE // 2
LANE_OF_DIM = np.concatenate([np.arange(0, 32), np.arange(48, 80),
                              np.arange(ROPE_X1, ROPE_X1 + HALF_ROPE), np.arange(ROPE_X2, ROPE_X2 + HALF_ROPE)])
NEG_BIG = -1e30
LOG2E = math.log2(math.e)

ATT_TQ = 256
MLA_TQ = 256


def _cparams(sem):
    return pltpu.CompilerParams(dimension_semantics=sem, vmem_limit_bytes=VMEM_LIMIT)


def _lane_iota(shape):
    return lax.broadcasted_iota(jnp.int32, shape, len(shape) - 1)


def _dot(a, b):
    return jnp.dot(a, b, preferred_element_type=f32)


def _dot_nt(a, b):
    return lax.dot_general(a, b, (((1,), (1,)), ((), ())), preferred_element_type=f32)


def _sigmoid(x):
    return 1.0 / (1.0 + jnp.exp(-x))


def _silu(x):
    return x * _sigmoid(x)


def _ada_kernel(c_ref, w_ref, b_ref, o_ref):
    ca = _silu(c_ref[...]).astype(bf16)
    o_ref[...] = _dot(ca, w_ref[...].astype(bf16)) + b_ref[...]


def _ada_mod(c, ada_w, ada_b):
    nl, d, n = ada_w.shape
    bsz = c.shape[0]
    tn = 1024
    return pl.pallas_call(
        _ada_kernel,
        grid=(nl, n // tn),
        in_specs=[
            pl.BlockSpec((bsz, d), lambda l, j: (0, 0)),
            pl.BlockSpec((None, d, tn), lambda l, j: (l, 0, j)),
            pl.BlockSpec((None, 1, tn), lambda l, j: (l, 0, j)),
        ],
        out_specs=pl.BlockSpec((None, bsz, tn), lambda l, j: (l, 0, j)),
        out_shape=jax.ShapeDtypeStruct((nl, bsz, n), f32),
        compiler_params=_cparams(("arbitrary", "arbitrary")),
        name="ada_mod",
    )(c, ada_w, ada_b.reshape(nl, 1, n))


def _rope_kernel(pos_ref, freq_ref, c_ref, s_ref):
    ang = pos_ref[...].astype(f32) * freq_ref[...]
    cos, sin = jnp.cos(ang), jnp.sin(ang)
    lane = _lane_iota(ang.shape)
    first = (lane >= ROPE_X1) & (lane < ROPE_X1 + HALF_ROPE)
    second = (lane >= ROPE_X2) & (lane < ROPE_X2 + HALF_ROPE)
    c_ref[...] = jnp.where(first | second, cos, 1.0)
    s_ref[...] = jnp.where(first, -sin, jnp.where(second, sin, 0.0))


def _rope_tables(positions):
    t = positions.size
    tm = min(t, 2048)
    inv_freq = ROPE_THETA ** (-jnp.arange(0, MLA_ROPE, 2, dtype=f32) / MLA_ROPE)
    freq = jnp.zeros((LANES,), f32).at[LANE_OF_DIM[MLA_NOPE:]].set(jnp.tile(inv_freq, 2))
    out = jax.ShapeDtypeStruct((t, LANES), f32)
    spec = pl.BlockSpec((tm, LANES), lambda i: (i, 0))
    return pl.pallas_call(
        _rope_kernel,
        grid=(t // tm,),
        in_specs=[spec, pl.BlockSpec((1, LANES), lambda i: (0, 0))],
        out_specs=[spec, spec],
        out_shape=[out, out],
        compiler_params=_cparams(("arbitrary",)),
        name="rope_tables",
    )(jnp.broadcast_to(positions.reshape(t, 1), (t, LANES)), freq.reshape(1, LANES))


def _bias_kernel(rb_ref, o_ref):
    h, which = pl.program_id(0), pl.program_id(1)
    tq = o_ref.shape[-1]
    row = lax.broadcasted_iota(jnp.int32, (tq, tq), 0)
    col = lax.broadcasted_iota(jnp.int32, (tq, tq), 1)
    rel = col - row + which * tq
    d = jnp.maximum(rel, 0)
    max_exact = REL_BUCKETS // 2
    large = max_exact + (jnp.log(jnp.maximum(d, 1).astype(f32) / max_exact)
                         / math.log(REL_MAX_DIST / max_exact) * (REL_BUCKETS - max_exact)).astype(jnp.int32)
    large = jnp.minimum(large, REL_BUCKETS - 1)
    bucket = jnp.where(d < max_exact, d, large)
    val = jnp.full((tq, tq), rb_ref[0, h], f32)
    for k in range(1, REL_BUCKETS):
        val = jnp.where(bucket == k, rb_ref[k, h], val)
    val = (val - rb_ref[REL_BUCKETS - 1, h]) * LOG2E
    val = jnp.where(rel < 0, NEG_BIG, val)
    o_ref[...] = jnp.where(which >= 2, 0.0, val)


def _bias_tiles(rel_bias, tq):
    return pl.pallas_call(
        _bias_kernel,
        grid=(DA_HEADS, 3),
        in_specs=[pl.BlockSpec(memory_space=pltpu.SMEM)],
        out_specs=pl.BlockSpec((None, None, tq, tq), lambda h, w: (h, w, 0, 0)),
        out_shape=jax.ShapeDtypeStruct((DA_HEADS, 3, tq, tq), f32),
        compiler_params=_cparams(("arbitrary", "arbitrary")),
        name="bias_tiles",
    )(rel_bias)


def _inproj_kernel(x_ref, nw_ref, mod_ref, w_ref, qkw_ref, bd_ref, gqkv_ref, rest_ref, h_ref):
    j = pl.program_id(1)
    jq = G_COLS // IN_TN

    @pl.when(j == 0)
    def _():
        x = x_ref[...]
        y = x * lax.rsqrt(jnp.mean(x * x, axis=-1, keepdims=True) + EPS) * nw_ref[...]
        h_ref[...] = (y * (1.0 + mod_ref[1:2, :]) + mod_ref[0:1, :]).astype(bf16)

    def proj():
        return _dot(h_ref[...], w_ref[...])

    @pl.when(j < jq)
    def _():
        gqkv_ref[...] = (0.5 * jnp.tanh(0.5 * proj()) + 0.5).astype(bf16)

    @pl.when((j >= jq) & (j < jq + 2))
    def _():
        acc = proj()
        ss = _dot((acc * acc).astype(bf16), bd_ref[...]) * (1.0 / DA_HEAD_DIM)
        gqkv_ref[...] = (acc * lax.rsqrt(ss + EPS) * qkw_ref[...]).astype(bf16)

    @pl.when(j == jq + 2)
    def _():
        gqkv_ref[...] = proj().astype(bf16)

    @pl.when(j > jq + 2)
    def _():
        rest_ref[...] = proj()


def _inproj(x2, norm_w, mod, w_p, qk_w, bd, l, rows_per_batch):
    t, d = x2.shape
    tm = rows_per_batch
    nj = (GQKV_COLS + REST_COLS) // IN_TN
    nb = GQKV_COLS // IN_TN
    jq = G_COLS // IN_TN
    return pl.pallas_call(
        _inproj_kernel,
        grid=(t // tm, nj),
        in_specs=[
            pl.BlockSpec((tm, d), lambda i, j: (i, 0)),
            pl.BlockSpec((None, 1, d), lambda i, j: (l, 0, 0)),
            pl.BlockSpec((None, None, 6, d), lambda i, j: (l, i, 0, 0)),
            pl.BlockSpec((None, d, IN_TN), lambda i, j: (l, 0, j)),
            pl.BlockSpec((None, None, 1, IN_TN), lambda i, j: (l, jnp.clip(j - jq, 0, 1), 0, 0)),
            pl.BlockSpec((IN_TN, IN_TN), lambda i, j: (0, 0)),
        ],
        out_specs=[
            pl.BlockSpec((tm, IN_TN), lambda i, j: (i, jnp.minimum(j, nb - 1))),
            pl.BlockSpec((tm, IN_TN), lambda i, j: (i, jnp.maximum(j - nb, 0))),
        ],
        out_shape=[jax.ShapeDtypeStruct((t, GQKV_COLS), bf16), jax.ShapeDtypeStruct((t, REST_COLS), f32)],
        scratch_shapes=[pltpu.VMEM((tm, d), bf16)],
        compiler_params=_cparams(("arbitrary", "arbitrary")),
        name="inproj",
    )(x2, norm_w, mod, w_p, qk_w, bd)


def _rope_apply(x, c, s):
    return x * c + pltpu.roll(x, LANES // 2, 1) * s


def _mla_prep_kernel(lat_ref, qan_ref, kvan_ref, wuq_ref, wuk_ref, wuv_ref, qn_ref, kn_ref,
                     c_ref, s_ref, q_ref, k_ref, v_ref):
    lat = lat_ref[...]
    c, s = c_ref[...], s_ref[...]
    cq = lat[:, :MLA_Q_RANK]
    cqn = (cq * lax.rsqrt(jnp.mean(cq * cq, axis=-1, keepdims=True) + EPS) * qan_ref[...]).astype(bf16)
    ckv = lat[:, MLA_Q_RANK:MLA_Q_RANK + MLA_KV_RANK]
    ckvn = (ckv * lax.rsqrt(jnp.mean(ckv * ckv, axis=-1, keepdims=True) + EPS) * kvan_ref[...]).astype(bf16)
    misc = lat[:, MLA_Q_RANK + MLA_KV_RANK:]
    lane = _lane_iota(misc.shape)
    in_rope = (((lane >= ROPE_X1) & (lane < ROPE_X1 + HALF_ROPE))
               | ((lane >= ROPE_X2) & (lane < ROPE_X2 + HALF_ROPE)))
    krb = jnp.where(in_rope, misc, 0.0)
    qf = _dot(cqn, wuq_ref[...])
    kf = _dot(ckvn, wuk_ref[...])
    v_ref[...] = _dot(ckvn, wuv_ref[...]).astype(bf16)
    inv = 1.0 / MLA_QK
    qn, kn = qn_ref[...], kn_ref[...]
    for h in range(MLA_HEADS):
        sl = slice(h * LANES, (h + 1) * LANES)
        qh = qf[:, sl]
        qh = qh * lax.rsqrt(jnp.sum(qh * qh, axis=-1, keepdims=True) * inv + EPS) * qn
        q_ref[:, sl] = _rope_apply(qh, c, s).astype(bf16)
        kh = kf[:, sl] + krb
        kh = kh * lax.rsqrt(jnp.sum(kh * kh, axis=-1, keepdims=True) * inv + EPS) * kn
        k_ref[:, sl] = _rope_apply(kh, c, s).astype(bf16)


def _mla_prep(rest, qan, kvan, wuq, wuk, wuv, qn, kn, ctab, stab, l):
    t = rest.shape[0]
    tm = min(512, t)
    lat_blk = R_LAT // 512
    hq = MLA_HEADS * LANES
    vec = lambda n: pl.BlockSpec((None, 1, n), lambda i: (l, 0, 0))
    tab = pl.BlockSpec((tm, LANES), lambda i: (i, 0))
    return pl.pallas_call(
        _mla_prep_kernel,
        grid=(t // tm,),
        in_specs=[
            pl.BlockSpec((tm, 512), lambda i: (i, lat_blk)),
            vec(MLA_Q_RANK), vec(MLA_KV_RANK),
            pl.BlockSpec((None, MLA_Q_RANK, hq), lambda i: (l, 0, 0)),
            pl.BlockSpec((None, MLA_KV_RANK, hq), lambda i: (l, 0, 0)),
            pl.BlockSpec((None, MLA_KV_RANK, MLA_HEADS * MLA_V), lambda i: (l, 0, 0)),
            vec(LANES), vec(LANES), tab, tab,
        ],
        out_specs=[
            pl.BlockSpec((tm, hq), lambda i: (i, 0)),
            pl.BlockSpec((tm, hq), lambda i: (i, 0)),
            pl.BlockSpec((tm, MLA_HEADS * MLA_V), lambda i: (i, 0)),
        ],
        out_shape=[jax.ShapeDtypeStruct((t, hq), bf16), jax.ShapeDtypeStruct((t, hq), bf16),
                   jax.ShapeDtypeStruct((t, MLA_HEADS * MLA_V), bf16)],
        compiler_params=_cparams(("arbitrary",)),
        name="mla_prep",
    )(rest, qan, kvan, wuq, wuk, wuv, qn, kn, ctab, stab)


def _flash_t(q_list, k_ref, kcols, vt_ref, vrows, tile_fn, ntiles, qi, tq,
             m_ref, l_ref, acc_ref, s_refs, e_refs, al_refs):
    nmap = len(q_list)
    m_ref[...] = jnp.full(m_ref.shape, -jnp.inf, f32)
    l_ref[...] = jnp.zeros(l_ref.shape, f32)
    acc_ref[...] = jnp.zeros(acc_ref.shape, f32)
    last = qi

    def qk(t, slot):
        start = pl.multiple_of(t * tq, tq)
        w = jnp.minimum(qi - t, ntiles - 1)
        vals = []
        for a in range(nmap):
            k = k_ref[pl.ds(start, tq), kcols[a]:kcols[a] + LANES]
            vals.append(_dot_nt(k, q_list[a]) + tile_fn(a, w))
        for a in range(nmap):
            s_refs[slot][a] = vals[a]

    def sm(slot):
        for a in range(nmap):
            s = s_refs[slot][a]
            m_prev = m_ref[a]
            m_new = jnp.maximum(m_prev, jnp.max(s, axis=0, keepdims=True))
            alpha = jnp.exp2(m_prev - m_new)
            e = jnp.exp2(s - m_new)
            m_ref[a] = m_new
            e_refs[slot][a] = e.astype(bf16)
            al_refs[slot][a] = alpha

    def pv(t, slot):
        for a in range(nmap):
            vt = vt_ref[t, vrows[a]:vrows[a] + VT_BLOCK, :]
            res = _dot(vt, e_refs[slot][a])
            alpha = al_refs[slot][a]
            acc_ref[a] = alpha * acc_ref[a] + res[:LANES]
            l_ref[a] = alpha * l_ref[a] + res[LANES:LANES + 1]

    def body(t, cur):
        qk(jnp.minimum(t + 1, last), 1 - cur)
        pv(t - 1, 1 - cur)
        sm(cur)

    qk(0, 0)
    qk(jnp.minimum(1, last), 1)
    sm(0)

    def pair(i, carry):
        t = 2 * i + 1
        body(t, 1)
        body(t + 1, 0)
        return carry

    lax.fori_loop(0, last // 2, pair, 0)
    odd = lax.rem(last, 2) == 1

    @pl.when(odd)
    def _():
        body(last, 1)
        pv(last, 1)

    @pl.when(jnp.logical_not(odd))
    def _():
        pv(last, 0)


VT_BLOCK = LANES + 16


def _fill_vt(v_ref, vt_ref, tq):
    ones = jnp.ones((VT_BLOCK - LANES, tq), bf16)
    for c in range(vt_ref.shape[0]):
        vt = v_ref[c * tq:(c + 1) * tq, :].astype(f32).T.astype(bf16)
        for b in range(vt.shape[0] // LANES):
            vt_ref[c, b * VT_BLOCK:b * VT_BLOCK + LANES, :] = vt[b * LANES:(b + 1) * LANES, :]
            vt_ref[c, b * VT_BLOCK + LANES:(b + 1) * VT_BLOCK, :] = ones


def _attn_scratch(nmap, nq, hv, tq):
    sbuf = pltpu.VMEM((nmap, tq, tq), f32)
    ebuf = pltpu.VMEM((nmap, tq, tq), bf16)
    row = pltpu.VMEM((nmap, 1, tq), f32)
    return [pltpu.VMEM((nq, hv // LANES * VT_BLOCK, tq), bf16), row, row, pltpu.VMEM((nmap, LANES, tq), f32),
            sbuf, sbuf, ebuf, ebuf, row, row]


def _diff_attn_kernel(q_ref, k_ref, v_ref, bias_ref, lp_ref, sub_ref, o_ref,
                      vt_ref, m_ref, l_ref, acc_ref, s0_ref, s1_ref, e0_ref, e1_ref, al0_ref, al1_ref,
                      *, lam_init, tq):
    qi = pl.program_id(1)

    @pl.when(qi == 0)
    def _():
        _fill_vt(v_ref, vt_ref, tq)

    lane = _lane_iota((tq, LANES))
    q_list, kcols, vrows = [], [], []
    for h in range(DA_HEADS):
        q = q_ref[:, h * LANES:(h + 1) * LANES]
        zero = jnp.zeros_like(q)
        q_list += [jnp.where(lane < DA_HEAD_DIM, q, zero), jnp.where(lane >= DA_HEAD_DIM, q, zero)]
        kcols += [h * LANES, h * LANES]
        vrows += [h * VT_BLOCK, h * VT_BLOCK]
    lp, sub = lp_ref[...], sub_ref[...]
    _flash_t(q_list, k_ref, kcols, vt_ref, vrows, lambda a, w: bias_ref[a // 2, w], 3, qi, tq,
             m_ref, l_ref, acc_ref, (s0_ref, s1_ref), (e0_ref, e1_ref), (al0_ref, al1_ref))
    lam = (jnp.exp(jnp.sum(lp[0:1] * lp[1:2], axis=-1, keepdims=True))
           - jnp.exp(jnp.sum(lp[2:3] * lp[3:4], axis=-1, keepdims=True)) + lam_init)
    for h in range(DA_HEADS):
        o_t = acc_ref[2 * h] / l_ref[2 * h] - lam * (acc_ref[2 * h + 1] / l_ref[2 * h + 1])
        o = o_t.T
        o = o * lax.rsqrt(jnp.mean(o * o, axis=-1, keepdims=True) + EPS) * sub
        o_ref[:, h * LANES:(h + 1) * LANES] = (o * (1.0 - lam_init)).astype(bf16)


def _diff_attn(qkv3, bias, da_lambda, da_subln, l, lam_init):
    bsz, s, _ = qkv3.shape
    tq = bias.shape[-1]
    nq = s // tq
    hw = DA_HEADS * LANES
    qblk = G_COLS // hw
    kern = functools.partial(_diff_attn_kernel, lam_init=lam_init, tq=tq)
    return pl.pallas_call(
        kern,
        grid=(bsz, nq),
        in_specs=[
            pl.BlockSpec((None, tq, hw), lambda b, i: (b, i, qblk)),
            pl.BlockSpec((None, s, hw), lambda b, i: (b, 0, qblk + 1)),
            pl.BlockSpec((None, s, hw), lambda b, i: (b, 0, qblk + 2)),
            pl.BlockSpec((DA_HEADS, 3, tq, tq), lambda b, i: (0, 0, 0, 0)),
            pl.BlockSpec((None, 4, DA_HEAD_DIM), lambda b, i: (l, 0, 0)),
            pl.BlockSpec((None, 1, LANES), lambda b, i: (l, 0, 0)),
        ],
        out_specs=pl.BlockSpec((None, tq, hw), lambda b, i: (b, i, 0)),
        out_shape=jax.ShapeDtypeStruct((bsz, s, hw), bf16),
        scratch_shapes=_attn_scratch(2 * DA_HEADS, nq, hw, tq),
        compiler_params=_cparams(("arbitrary", "arbitrary")),
        name="diff_attn",
    )(qkv3, qkv3, qkv3, bias, da_lambda, da_subln)


def _mla_attn_kernel(q_ref, k_ref, v_ref, o_ref,
                     vt_ref, m_ref, l_ref, acc_ref, s0_ref, s1_ref, e0_ref, e1_ref, al0_ref, al1_ref, mask_ref,
                     *, tq):
    qi = pl.program_id(1)

    @pl.when(qi == 0)
    def _():
        _fill_vt(v_ref, vt_ref, tq)
        krow = lax.broadcasted_iota(jnp.int32, (tq, tq), 0)
        qcol = lax.broadcasted_iota(jnp.int32, (tq, tq), 1)
        mask_ref[0] = jnp.where(krow <= qcol, 0.0, NEG_BIG)
        mask_ref[1] = jnp.zeros((tq, tq), f32)

    q_list = [q_ref[:, h * LANES:(h + 1) * LANES] for h in range(MLA_HEADS)]
    kcols = [h * LANES for h in range(MLA_HEADS)]
    vrows = [(h // 2) * VT_BLOCK for h in range(MLA_HEADS)]
    _flash_t(q_list, k_ref, kcols, vt_ref, vrows, lambda a, w: mask_ref[w], 2, qi, tq,
             m_ref, l_ref, acc_ref, (s0_ref, s1_ref), (e0_ref, e1_ref), (al0_ref, al1_ref))
    row = lax.broadcasted_iota(jnp.int32, (LANES, tq), 0)
    for p in range(MLA_HEADS // 2):
        o_t = jnp.where(row < MLA_V, acc_ref[2 * p] / l_ref[2 * p], acc_ref[2 * p + 1] / l_ref[2 * p + 1])
        o_ref[:, p * LANES:(p + 1) * LANES] = o_t.T.astype(bf16)


def _mla_attn(q3, k3, v3, tq):
    bsz, s, hq = q3.shape
    nq = s // tq
    hv = v3.shape[-1]
    kern = functools.partial(_mla_attn_kernel, tq=tq)
    return pl.pallas_call(
        kern,
        grid=(bsz, nq),
        in_specs=[
            pl.BlockSpec((None, tq, hq), lambda b, i: (b, i, 0)),
            pl.BlockSpec((None, s, hq), lambda b, i: (b, 0, 0)),
            pl.BlockSpec((None, s, hv), lambda b, i: (b, 0, 0)),
        ],
        out_specs=pl.BlockSpec((None, tq, hv), lambda b, i: (b, i, 0)),
        out_shape=jax.ShapeDtypeStruct((bsz, s, hv), bf16),
        scratch_shapes=_attn_scratch(MLA_HEADS, nq, hv, tq) + [pltpu.VMEM((2, tq, tq), f32)],
        compiler_params=_cparams(("arbitrary", "arbitrary")),
        name="mla_attn",
    )(q3, k3, v3)


def _split3(x):
    hi = x.astype(bf16)
    r1 = x - hi.astype(f32)
    mid = r1.astype(bf16)
    lo = (r1 - mid.astype(f32)).astype(bf16)
    return hi, mid, lo


def _pair_bcast(x, j, lane):
    return jnp.where(lane < SSM_HEAD_DIM, x[:, 2 * j:2 * j + 1], x[:, 2 * j + 1:2 * j + 2])


def _ssd_kernel(xbc_ref, z_ref, misc_ref, cw_ref, cb_ref, dtb_ref, alog_ref, dvec_ref, nw_ref,
                o_ref, ext_ref, st_ref, y_ref):
    c = pl.program_id(1)
    L = SSM_CHUNK
    G, N = SSM_GROUPS, SSM_STATE

    @pl.when(c == 0)
    def _():
        ext_ref[0:SUBLANES, :] = jnp.zeros((SUBLANES, SSM_CONV_DIM), f32)
        st_ref[...] = jnp.zeros(st_ref.shape, f32)

    zs, dvec, nw = _silu(z_ref[...]), dvec_ref[...], nw_ref[...]
    cw, cbias = cw_ref[...], cb_ref[...]
    dt_in = misc_ref[...] + dtb_ref[...]
    alog = alog_ref[...]

    ext_ref[SUBLANES:, :] = xbc_ref[...]
    def rows8(v):
        return v.reshape(L // SUBLANES, SUBLANES, SSM_CONV_DIM)

    conv = cbias + cw[SSM_CONV - 1] * rows8(ext_ref[SUBLANES:, :])
    for s in range(1, SSM_CONV):
        conv = conv + cw[SSM_CONV - 1 - s] * rows8(ext_ref[SUBLANES - s:SUBLANES - s + L, :])
    ext_ref[0:SUBLANES, :] = ext_ref[L:L + SUBLANES, :]
    xa = _silu(conv).reshape(L, SSM_CONV_DIM)
    xs = xa[:, :SSM_D_INNER]
    bm = xa[:, SSM_D_INNER:SSM_D_INNER + G * N]
    cm = xa[:, SSM_D_INNER + G * N:]

    lane = _lane_iota((L, LANES))
    lane1 = _lane_iota((1, LANES))
    dt = jax.nn.softplus(dt_in)
    a_l = jnp.where(lane1 < SSM_HEADS, -jnp.exp(alog), 0.0)
    da = dt * a_l
    row = lax.broadcasted_iota(jnp.int32, (L, L), 0)
    col = lax.broadcasted_iota(jnp.int32, (L, L), 1)
    causal = col <= row
    tri = jnp.where(causal, 1.0, 0.0).astype(bf16)
    hi, mid, lo = _split3(da)
    cs = _dot(tri, hi) + _dot(tri, mid) + _dot(tri, lo)
    cs_t = cs.T
    cs_last = cs[L - 1:L, :]
    w_end = dt * jnp.exp(cs_last - cs)
    e_cs = jnp.exp(cs)
    cdec = jnp.exp(cs_last)

    for g in range(G):
        bm_g = bm[:, g * N:(g + 1) * N]
        cm_g = cm[:, g * N:(g + 1) * N].astype(bf16)
        cb = _dot_nt(cm_g, bm_g.astype(bf16))
        bm_t = bm_g.T.astype(bf16)
        for jj in range(2):
            j = 2 * g + jj
            sl = slice(j * LANES, (j + 1) * LANES)
            psl = slice(jj * LANES, (jj + 1) * LANES)
            xs_p = xs[:, sl]
            xdt = (xs_p * _pair_bcast(dt, j, lane)).astype(bf16)
            yd = []
            for hh in (2 * j, 2 * j + 1):
                seg = cs[:, hh:hh + 1] - cs_t[hh:hh + 1, :]
                decay = jnp.exp(jnp.where(causal, seg, -jnp.inf))
                yd.append(_dot((cb * decay).astype(bf16), xdt))
            y_diag = jnp.where(lane < SSM_HEAD_DIM, yd[0], yd[1])
            prev = st_ref[g, :, psl]
            y_off = _dot(cm_g, prev.astype(bf16)) * _pair_bcast(e_cs, j, lane)
            y_ref[:, sl] = y_diag + y_off + dvec[:, sl] * xs_p
            xw = (xs_p * _pair_bcast(w_end, j, lane)).astype(bf16)
            cd = jnp.where(lane1 < SSM_HEAD_DIM, cdec[:, 2 * j:2 * j + 1], cdec[:, 2 * j + 1:2 * j + 2])
            st_ref[g, :, psl] = prev * cd + _dot(bm_t, xw)

    gw = SSM_D_INNER // G
    for g in range(G):
        sl = slice(g * gw, (g + 1) * gw)
        yz = y_ref[:, sl] * zs[:, sl]
        yz = yz * lax.rsqrt(jnp.mean(yz * yz, axis=-1, keepdims=True) + EPS) * nw[:, sl]
        o_ref[:, sl] = yz.astype(bf16)


def _ssd(rest3, conv_w, conv_b, dt_bias, a_log, dvec, norm_w, l):
    bsz, s, _ = rest3.shape
    L = SSM_CHUNK
    nc = s // L
    misc_blk = (R_LAT + MLA_Q_RANK + MLA_KV_RANK) // LANES
    vec = lambda n: pl.BlockSpec((None, 1, n), lambda b, c: (l, 0, 0))
    return pl.pallas_call(
        _ssd_kernel,
        grid=(bsz, nc),
        in_specs=[
            pl.BlockSpec((None, L, SSM_CONV_DIM), lambda b, c: (b, c, R_XBC // SSM_CONV_DIM)),
            pl.BlockSpec((None, L, SSM_D_INNER), lambda b, c: (b, c, R_Z // SSM_D_INNER)),
            pl.BlockSpec((None, L, LANES), lambda b, c: (b, c, misc_blk)),
            pl.BlockSpec((None, SSM_CONV, SUBLANES, SSM_CONV_DIM), lambda b, c: (l, 0, 0, 0)),
            pl.BlockSpec((None, SUBLANES, SSM_CONV_DIM), lambda b, c: (l, 0, 0)),
            vec(LANES), vec(LANES), vec(SSM_D_INNER), vec(SSM_D_INNER),
        ],
        out_specs=pl.BlockSpec((None, L, SSM_D_INNER), lambda b, c: (b, c, 0)),
        out_shape=jax.ShapeDtypeStruct((bsz, s, SSM_D_INNER), bf16),
        scratch_shapes=[
            pltpu.VMEM((L + SUBLANES, SSM_CONV_DIM), f32),
            pltpu.VMEM((SSM_GROUPS, SSM_STATE, 4 * SSM_HEAD_DIM), f32),
            pltpu.VMEM((L, SSM_D_INNER), f32),
        ],
        compiler_params=_cparams(("arbitrary", "arbitrary")),
        name="ssd",
    )(rest3, rest3, rest3, conv_w, conv_b, dt_bias, a_log, dvec, norm_w)


def _merge_kernel(ya_ref, yb_ref, ys_ref, ga_ref, gb_ref, gc_ref, x_ref, mod_ref,
                  wa_ref, wb_ref, wc_ref, wo_ref, o_ref):
    merged = (ga_ref[...] * _dot(ya_ref[...], wa_ref[...])
              + gb_ref[...] * _dot(yb_ref[...], wb_ref[...])
              + gc_ref[...] * _dot(ys_ref[...], wc_ref[...]))
    o_ref[...] = x_ref[...] + mod_ref[2:3, :] * _dot(merged.astype(bf16), wo_ref[...])


def _merge(ya, yb, ys, gqkv, x2, mod, wa, wb, wc, wo, l, rows_per_batch):
    t, d = x2.shape
    tm = min(512, rows_per_batch)
    tpb = rows_per_batch // tm
    row = lambda n: pl.BlockSpec((tm, n), lambda i: (i, 0))
    wspec = lambda k: pl.BlockSpec((None, k, d), lambda i: (l, 0, 0))
    return pl.pallas_call(
        _merge_kernel,
        grid=(t // tm,),
        in_specs=[
            row(ya.shape[1]), row(yb.shape[1]), row(ys.shape[1]),
            pl.BlockSpec((tm, d), lambda i: (i, 0)),
            pl.BlockSpec((tm, d), lambda i: (i, 1)),
            pl.BlockSpec((tm, d), lambda i: (i, 2)),
            row(d),
            pl.BlockSpec((None, None, 6, d), lambda i: (l, i // tpb, 0, 0)),
            wspec(ya.shape[1]), wspec(yb.shape[1]), wspec(ys.shape[1]), wspec(d),
        ],
        out_specs=row(d),
        out_shape=jax.ShapeDtypeStruct((t, d), f32),
        compiler_params=_cparams(("arbitrary",)),
        name="merge_out",
    )(ya, yb, ys, gqkv, gqkv, gqkv, x2, mod, wa, wb, wc, wo)


FFN_TF = 256
FFN_NF = FFN_DIM // FFN_TF
FFN_BLOCK_WEIGHTS = (1, 1, 1, 1)


def _ffn_kernel(x_ref, nw_ref, mod_ref, wa_ref, wv_ref, cwa_ref, cwv_ref, cba_ref, cbv_ref, wd_ref,
                o_ref, h_ref, ua_ref, uv_ref, acc_ref):
    fi = pl.program_id(1)
    tm = x_ref.shape[0]

    @pl.when(fi == 0)
    def _():
        x = x_ref[...]
        y = x * lax.rsqrt(jnp.mean(x * x, axis=-1, keepdims=True) + EPS) * nw_ref[...]
        h_ref[...] = (y * (1.0 + mod_ref[4:5, :]) + mod_ref[3:4, :]).astype(bf16)
        acc_ref[...] = jnp.zeros(acc_ref.shape, f32)
        ua_ref[0:SUBLANES, :] = jnp.zeros((SUBLANES, FFN_TF), f32)
        uv_ref[0:SUBLANES, :] = jnp.zeros((SUBLANES, FFN_TF), f32)

    wa, wv, wd = wa_ref[...], wv_ref[...], wd_ref[...]
    cwa, cwv, cba, cbv = cwa_ref[...], cwv_ref[...], cba_ref[...], cbv_ref[...]

    sizes =[tm * w // sum(FFN_BLOCK_WEIGHTS) for w in FFN_BLOCK_WEIGHTS]
    assert sum(sizes) == tm and all(n % 16 == 0 for n in sizes)
    starts = [sum(sizes[:i]) for i in range(len(sizes))]

    def conv(u_ref, cw, cb, base, n):
        out = cb + cw[FFN_CONV - 1:FFN_CONV, :] * u_ref[base:base + n, :]
        for s in range(1, FFN_CONV):
            out = out + cw[FFN_CONV - 1 - s:FFN_CONV - s, :] * u_ref[base - s:base - s + n, :]
        return out

    def up(rb):
        r0, n = starts[rb], sizes[rb]
        hb = h_ref[r0:r0 + n, :]
        ua_ref[SUBLANES + r0:SUBLANES + r0 + n, :] = _dot(hb, wa)
        uv_ref[SUBLANES + r0:SUBLANES + r0 + n, :] = _dot(hb, wv)

    up(0)
    for rb in range(len(sizes)):
        if rb + 1 < len(sizes):
            up(rb + 1)
        r0, n = starts[rb], sizes[rb]
        base = SUBLANES + r0
        act = (_silu(conv(ua_ref, cwa, cba, base, n)) * conv(uv_ref, cwv, cbv, base, n)).astype(bf16)
        acc_ref[r0:r0 + n, :] += _dot(act, wd)

    @pl.when(fi == pl.num_programs(1) - 1)
    def _():
        o_ref[...] = x_ref[...] + mod_ref[5:6, :] * acc_ref[...]


def _ffn(x3, norm_w, mod, wup, conv_w, conv_b, wdown, l):
    bsz, s, d = x3.shape
    tf, nf = FFN_TF, FFN_NF
    return pl.pallas_call(
        _ffn_kernel,
        grid=(bsz, nf),
        in_specs=[
            pl.BlockSpec((None, s, d), lambda b, f: (b, 0, 0)),
            pl.BlockSpec((None, 1, d), lambda b, f: (l, 0, 0)),
            pl.BlockSpec((None, None, 6, d), lambda b, f: (l, b, 0, 0)),
            pl.BlockSpec((None, d, tf), lambda b, f: (l, 0, f)),
            pl.BlockSpec((None, d, tf), lambda b, f: (l, 0, nf + f)),
            pl.BlockSpec((None, FFN_CONV, tf), lambda b, f: (l, 0, f)),
            pl.BlockSpec((None, FFN_CONV, tf), lambda b, f: (l, 0, nf + f)),
            pl.BlockSpec((None, 1, tf), lambda b, f: (l, 0, f)),
            pl.BlockSpec((None, 1, tf), lambda b, f: (l, 0, nf + f)),
            pl.BlockSpec((None, tf, d), lambda b, f: (l, f, 0)),
        ],
        out_specs=pl.BlockSpec((None, s, d), lambda b, f: (b, 0, 0)),
        out_shape=jax.ShapeDtypeStruct((bsz, s, d), f32),
        scratch_shapes=[
            pltpu.VMEM((s, d), bf16),
            pltpu.VMEM((s + SUBLANES, tf), f32), pltpu.VMEM((s + SUBLANES, tf), f32),
            pltpu.VMEM((s, d), f32),
        ],
        compiler_params=_cparams(("arbitrary", "arbitrary")),
        name="ffn",
    )(x3, norm_w, mod, wup, wup, conv_w, conv_w, conv_b, conv_b, wdown)


def _pack_w_in(w_in):
    nl, d, _ = w_in.shape
    sizes = (512, 512, 512, MLA_Q_RANK, MLA_KV_RANK, MLA_ROPE, SSM_D_INNER, SSM_CONV_DIM, SSM_HEADS, 3 * D_MODEL)
    offs = np.concatenate([[0], np.cumsum(sizes)])
    qa, ka, va, cq, ckv, kr, z, xbc, dt, gates = [w_in[..., int(offs[i]):int(offs[i + 1])] for i in range(10)]
    zeros = lambda n: jnp.zeros((nl, d, n), w_in.dtype)
    misc = jnp.concatenate([dt, zeros(ROPE_X1 - SSM_HEADS), kr[..., :HALF_ROPE], zeros(ROPE_X2 - ROPE_X1 - HALF_ROPE),
                            kr[..., HALF_ROPE:], zeros(LANES - ROPE_X2 - HALF_ROPE)], axis=-1)
    return jnp.concatenate([gates, qa, ka, va, xbc, z, cq, ckv, misc], axis=-1).astype(bf16)


def _pad_lanes(v, n):
    return jnp.pad(v, [(0, 0)] * (v.ndim - 1) + [(0, n - v.shape[-1])])


def _to_head_slot(v):
    return jnp.zeros(v.shape[:-1] + (LANES,), v.dtype).at[..., LANE_OF_DIM[:v.shape[-1]]].set(v)


def kernel(x, c, positions, rel_bias, ada_w, ada_b, norm_mix, norm_ffn, w_in, da_q_norm, da_k_norm, da_lambda, da_subln, mla_q_a_norm, mla_kv_a_norm, mla_w_uq, mla_w_ukv, mla_q_norm, mla_k_norm, ssm_conv_w, ssm_conv_b, ssm_dt_bias, ssm_a_log, ssm_d, ssm_norm, w_branch_a, w_branch_b, w_branch_c, w_out, ffn_w_up, ffn_conv_w, ffn_conv_b, ffn_w_down):
    bsz, seq, d = x.shape
    nl = w_in.shape[0]
    t = bsz * seq
    tq = min(ATT_TQ, seq)

    w_in_p = _pack_w_in(w_in)
    qk_w = jnp.stack([jnp.tile(da_q_norm, (1, 2 * DA_HEADS)) * (DA_HEAD_DIM ** -0.5 * LOG2E),
                      jnp.tile(da_k_norm, (1, 2 * DA_HEADS))], axis=1).reshape(nl, 2, 1, IN_TN)
    blk = np.arange(IN_TN) // DA_HEAD_DIM
    bd = jnp.asarray(blk[:, None] == blk[None, :], bf16)
    wuq = _to_head_slot(mla_w_uq.reshape(nl, MLA_Q_RANK, MLA_HEADS, MLA_QK))
    wuq = wuq.reshape(nl, MLA_Q_RANK, MLA_HEADS * LANES).astype(bf16)
    wukv = mla_w_ukv.reshape(nl, MLA_KV_RANK, MLA_HEADS, MLA_NOPE + MLA_V)
    wuk = _to_head_slot(wukv[..., :MLA_NOPE]).reshape(nl, MLA_KV_RANK, MLA_HEADS * LANES).astype(bf16)
    wuv = wukv[..., MLA_NOPE:].reshape(nl, MLA_KV_RANK, MLA_HEADS * MLA_V).astype(bf16)
    qn = (_to_head_slot(mla_q_norm) * (MLA_QK ** -0.5 * LOG2E)).reshape(nl, 1, LANES)
    kn = _to_head_slot(mla_k_norm).reshape(nl, 1, LANES)
    dt_bias = _pad_lanes(ssm_dt_bias, LANES).reshape(nl, 1, LANES)
    a_log = _pad_lanes(ssm_a_log, LANES).reshape(nl, 1, LANES)
    dvec = jnp.repeat(ssm_d, SSM_HEAD_DIM, axis=-1).reshape(nl, 1, SSM_D_INNER)
    conv_w8 = jnp.broadcast_to(ssm_conv_w[:, :, None, :], (nl, SSM_CONV, SUBLANES, SSM_CONV_DIM))
    conv_b8 = jnp.broadcast_to(ssm_conv_b[:, None, :], (nl, SUBLANES, SSM_CONV_DIM))
    v3 = lambda a: a.reshape(nl, 1, a.shape[-1])
    wa, wb, wc, wo = (w.astype(bf16) for w in (w_branch_a, w_branch_b, w_branch_c, w_out))
    wup, wdown = ffn_w_up.astype(bf16), ffn_w_down.astype(bf16)

    mod = _ada_mod(c, ada_w, ada_b).reshape(nl, bsz, 6, d)
    ctab, stab = _rope_tables(positions)
    bias = _bias_tiles(rel_bias, tq)

    x2 = x.reshape(t, d)
    for l in range(nl):
        lam_init = 0.8 - 0.6 * math.exp(-0.3 * l)
        gqkv, rest = _inproj(x2, v3(norm_mix), mod, w_in_p, qk_w, bd, l, seq)
        q_b, k_b, v_b = _mla_prep(rest, v3(mla_q_a_norm), v3(mla_kv_a_norm), wuq, wuk, wuv, qn, kn, ctab, stab, l)
        ya = _diff_attn(gqkv.reshape(bsz, seq, GQKV_COLS), bias, da_lambda, v3(da_subln), l, lam_init)
        yb = _mla_attn(q_b.reshape(bsz, seq, -1), k_b.reshape(bsz, seq, -1), v_b.reshape(bsz, seq, -1),
                       min(MLA_TQ, seq))
        ys = _ssd(rest.reshape(bsz, seq, REST_COLS), conv_w8, conv_b8, dt_bias, a_log, dvec, v3(ssm_norm), l)
        x2 = _merge(ya.reshape(t, -1), yb.reshape(t, -1), ys.reshape(t, -1), gqkv, x2, mod, wa, wb, wc, wo, l, seq)
        x2 = _ffn(x2.reshape(bsz, seq, d), v3(norm_ffn), mod, wup, ffn_conv_w, v3(ffn_conv_b), wdown, l).reshape(t, d)
    return x2.reshape(bsz, seq, d)
```

```python
import functools
import math

import jax
import jax.numpy as jnp
import numpy as np
from jax import lax
from jax.experimental import pallas as pl
from jax.experimental.pallas import tpu as pltpu

f32 = jnp.float32
bf16 = jnp.bfloat16

D_MODEL = 1024
DA_HEADS = 4
DA_HEAD_DIM = 64
MLA_HEADS = 8
MLA_Q_RANK = 256
MLA_KV_RANK = 128
MLA_NOPE = 64
MLA_ROPE = 32
MLA_V = 64
MLA_QK = MLA_NOPE + MLA_ROPE
SSM_D_INNER = 1024
SSM_HEAD_DIM = 64
SSM_HEADS = 16
SSM_GROUPS = 4
SSM_STATE = 128
SSM_CONV = 4
SSM_CHUNK = 128
SSM_CONV_DIM = SSM_D_INNER + 2 * SSM_GROUPS * SSM_STATE
FFN_DIM = 2816
FFN_CONV = 3
REL_BUCKETS = 32
REL_MAX_DIST = 128
ROPE_THETA = 10000.0
EPS = 1e-6

LANES = 128
SUBLANES = 8
VMEM_LIMIT = 56 * 1024 * 1024

IN_TN = 512
G_COLS = 3 * D_MODEL
GQKV_COLS = G_COLS + 3 * 512
R_XBC = 0
R_Z = R_XBC + SSM_CONV_DIM
R_LAT = R_Z + SSM_D_INNER
REST_COLS = R_LAT + 512
ROPE_X1 = 32
ROPE_X2 = ROPE_X1 + LANES // 2
HALF_ROPE = MLA_ROPE // 2
LANE_OF_DIM = np.concatenate([np.arange(0, 32), np.arange(48, 80),
                              np.arange(ROPE_X1, ROPE_X1 + HALF_ROPE), np.arange(ROPE_X2, ROPE_X2 + HALF_ROPE)])
NEG_BIG = -1e30
LOG2E = math.log2(math.e)

ATT_TQ = 256
MLA_TQ = 256


def _cparams(sem):
    return pltpu.CompilerParams(dimension_semantics=sem, vmem_limit_bytes=VMEM_LIMIT)


def _lane_iota(shape):
    return lax.broadcasted_iota(jnp.int32, shape, len(shape) - 1)


def _dot(a, b):
    return jnp.dot(a, b, preferred_element_type=f32)


def _dot_nt(a, b):
    return lax.dot_general(a, b, (((1,), (1,)), ((), ())), preferred_element_type=f32)


def _sigmoid(x):
    return 1.0 / (1.0 + jnp.exp(-x))


def _silu(x):
    return x * _sigmoid(x)


def _ada_kernel(c_ref, w_ref, b_ref, o_ref):
    ca = _silu(c_ref[...]).astype(bf16)
    o_ref[...] = _dot(ca, w_ref[...].astype(bf16)) + b_ref[...]


def _ada_mod(c, ada_w, ada_b):
    nl, d, n = ada_w.shape
    bsz = c.shape[0]
    tn = 1024
    return pl.pallas_call(
        _ada_kernel,
        grid=(nl, n // tn),
        in_specs=[
            pl.BlockSpec((bsz, d), lambda l, j: (0, 0)),
            pl.BlockSpec((None, d, tn), lambda l, j: (l, 0, j)),
            pl.BlockSpec((None, 1, tn), lambda l, j: (l, 0, j)),
        ],
        out_specs=pl.BlockSpec((None, bsz, tn), lambda l, j: (l, 0, j)),
        out_shape=jax.ShapeDtypeStruct((nl, bsz, n), f32),
        compiler_params=_cparams(("arbitrary", "arbitrary")),
        name="ada_mod",
    )(c, ada_w, ada_b.reshape(nl, 1, n))


def _rope_kernel(pos_ref, freq_ref, c_ref, s_ref):
    ang = pos_ref[...].astype(f32) * freq_ref[...]
    cos, sin = jnp.cos(ang), jnp.sin(ang)
    lane = _lane_iota(ang.shape)
    first = (lane >= ROPE_X1) & (lane < ROPE_X1 + HALF_ROPE)
    second = (lane >= ROPE_X2) & (lane < ROPE_X2 + HALF_ROPE)
    c_ref[...] = jnp.where(first | second, cos, 1.0)
    s_ref[...] = jnp.where(first, -sin, jnp.where(second, sin, 0.0))


def _rope_tables(positions):
    t = positions.size
    tm = min(t, 2048)
    inv_freq = ROPE_THETA ** (-jnp.arange(0, MLA_ROPE, 2, dtype=f32) / MLA_ROPE)
    freq = jnp.zeros((LANES,), f32).at[LANE_OF_DIM[MLA_NOPE:]].set(jnp.tile(inv_freq, 2))
    out = jax.ShapeDtypeStruct((t, LANES), f32)
    spec = pl.BlockSpec((tm, LANES), lambda i: (i, 0))
    return pl.pallas_call(
        _rope_kernel,
        grid=(t // tm,),
        in_specs=[spec, pl.BlockSpec((1, LANES), lambda i: (0, 0))],
        out_specs=[spec, spec],
        out_shape=[out, out],
        compiler_params=_cparams(("arbitrary",)),
        name="rope_tables",
    )(jnp.broadcast_to(positions.reshape(t, 1), (t, LANES)), freq.reshape(1, LANES))


def _bias_kernel(rb_ref, o_ref):
    h, which = pl.program_id(0), pl.program_id(1)
    tq = o_ref.shape[-1]
    row = lax.broadcasted_iota(jnp.int32, (tq, tq), 0)
    col = lax.broadcasted_iota(jnp.int32, (tq, tq), 1)
    rel = col - row + which * tq
    d = jnp.maximum(rel, 0)
    max_exact = REL_BUCKETS // 2
    large = max_exact + (jnp.log(jnp.maximum(d, 1).astype(f32) / max_exact)
                         / math.log(REL_MAX_DIST / max_exact) * (REL_BUCKETS - max_exact)).astype(jnp.int32)
    large = jnp.minimum(large, REL_BUCKETS - 1)
    bucket = jnp.where(d < max_exact, d, large)
    val = jnp.full((tq, tq), rb_ref[0, h], f32)
    for k in range(1, REL_BUCKETS):
        val = jnp.where(bucket == k, rb_ref[k, h], val)
    val = (val - rb_ref[REL_BUCKETS - 1, h]) * LOG2E
    val = jnp.where(rel < 0, NEG_BIG, val)
    o_ref[...] = jnp.where(which >= 2, 0.0, val)


def _bias_tiles(rel_bias, tq):
    return pl.pallas_call(
        _bias_kernel,
        grid=(DA_HEADS, 3),
        in_specs=[pl.BlockSpec(memory_space=pltpu.SMEM)],
        out_specs=pl.BlockSpec((None, None, tq, tq), lambda h, w: (h, w, 0, 0)),
        out_shape=jax.ShapeDtypeStruct((DA_HEADS, 3, tq, tq), f32),
        compiler_params=_cparams(("arbitrary", "arbitrary")),
        name="bias_tiles",
    )(rel_bias)


def _inproj_kernel(x_ref, nw_ref, mod_ref, wg_ref, wqkv_ref, wxbc_ref, wz_ref, wlat_ref, qkw_ref, bd_ref,
                   gqkv_ref, rest_ref, h_ref):
    j = pl.program_id(1)
    jq = G_COLS // IN_TN
    jx = GQKV_COLS // IN_TN
    jz = jx + SSM_CONV_DIM // IN_TN
    jl = jz + SSM_D_INNER // IN_TN

    @pl.when(j == 0)
    def _():
        x = x_ref[...]
        y = x * lax.rsqrt(jnp.mean(x * x, axis=-1, keepdims=True) + EPS) * nw_ref[...]
        h_ref[...] = (y * (1.0 + mod_ref[1:2, :]) + mod_ref[0:1, :]).astype(bf16)

    def proj(w_ref):
        return _dot(h_ref[...], w_ref[...])

    @pl.when(j < jq)
    def _():
        gqkv_ref[...] = (0.5 * jnp.tanh(0.5 * proj(wg_ref)) + 0.5).astype(bf16)

    @pl.when((j >= jq) & (j < jq + 2))
    def _():
        acc = proj(wqkv_ref)
        ss = _dot((acc * acc).astype(bf16), bd_ref[...]) * (1.0 / DA_HEAD_DIM)
        gqkv_ref[...] = (acc * lax.rsqrt(ss + EPS) * qkw_ref[...]).astype(bf16)

    @pl.when(j == jq + 2)
    def _():
        gqkv_ref[...] = proj(wqkv_ref).astype(bf16)

    @pl.when((j >= jx) & (j < jz))
    def _():
        rest_ref[...] = proj(wxbc_ref)

    @pl.when((j >= jz) & (j < jl))
    def _():
        rest_ref[...] = proj(wz_ref)

    @pl.when(j >= jl)
    def _():
        rest_ref[...] = proj(wlat_ref)


def _inproj(x2, norm_w, mod, w_parts, qk_w, bd, l, rows_per_batch):
    t, d = x2.shape
    tm = rows_per_batch
    nj = (GQKV_COLS + REST_COLS) // IN_TN
    nb = GQKV_COLS // IN_TN
    jq = G_COLS // IN_TN
    jz = nb + SSM_CONV_DIM // IN_TN

    def wspec(w, first):
        last = w.shape[-1] // IN_TN - 1
        return pl.BlockSpec((None, d, IN_TN), lambda i, j: (l, 0, jnp.clip(j - first, 0, last)))
    return pl.pallas_call(
        _inproj_kernel,
        grid=(t // tm, nj),
        in_specs=[
            pl.BlockSpec((tm, d), lambda i, j: (i, 0)),
            pl.BlockSpec((None, 1, d), lambda i, j: (l, 0, 0)),
            pl.BlockSpec((None, None, 6, d), lambda i, j: (l, i, 0, 0)),
            wspec(w_parts[0], 0), wspec(w_parts[1], jq), wspec(w_parts[2], nb), wspec(w_parts[3], jz),
            wspec(w_parts[4], nj - 1),
            pl.BlockSpec((None, None, 1, IN_TN), lambda i, j: (l, jnp.clip(j - jq, 0, 1), 0, 0)),
            pl.BlockSpec((IN_TN, IN_TN), lambda i, j: (0, 0)),
        ],
        out_specs=[
            pl.BlockSpec((tm, IN_TN), lambda i, j: (i, jnp.minimum(j, nb - 1))),
            pl.BlockSpec((tm, IN_TN), lambda i, j: (i, jnp.maximum(j - nb, 0))),
        ],
        out_shape=[jax.ShapeDtypeStruct((t, GQKV_COLS), bf16), jax.ShapeDtypeStruct((t, REST_COLS), f32)],
        scratch_shapes=[pltpu.VMEM((tm, d), bf16)],
        compiler_params=_cparams(("arbitrary", "arbitrary")),
        name="inproj",
    )(x2, norm_w, mod, *w_parts, qk_w, bd)


def _rope_apply(x, c, s):
    return x * c + pltpu.roll(x, LANES // 2, 1) * s


def _mla_prep_kernel(lat_ref, qan_ref, kvan_ref, wuq_ref, wuk_ref, wuv_ref, qn_ref, kn_ref,
                     c_ref, s_ref, q_ref, k_ref, v_ref):
    lat = lat_ref[...]
    c, s = c_ref[...], s_ref[...]
    cq = lat[:, :MLA_Q_RANK]
    cqn = (cq * lax.rsqrt(jnp.mean(cq * cq, axis=-1, keepdims=True) + EPS) * qan_ref[...]).astype(bf16)
    ckv = lat[:, MLA_Q_RANK:MLA_Q_RANK + MLA_KV_RANK]
    ckvn = (ckv * lax.rsqrt(jnp.mean(ckv * ckv, axis=-1, keepdims=True) + EPS) * kvan_ref[...]).astype(bf16)
    misc = lat[:, MLA_Q_RANK + MLA_KV_RANK:]
    lane = _lane_iota(misc.shape)
    in_rope = (((lane >= ROPE_X1) & (lane < ROPE_X1 + HALF_ROPE))
               | ((lane >= ROPE_X2) & (lane < ROPE_X2 + HALF_ROPE)))
    krb = jnp.where(in_rope, misc, 0.0)
    qf = _dot(cqn, wuq_ref[...])
    kf = _dot(ckvn, wuk_ref[...])
    v_ref[...] = _dot(ckvn, wuv_ref[...]).astype(bf16)
    inv = 1.0 / MLA_QK
    qn, kn = qn_ref[...], kn_ref[...]
    for h in range(MLA_HEADS):
        sl = slice(h * LANES, (h + 1) * LANES)
        qh = qf[:, sl]
        qh = qh * lax.rsqrt(jnp.sum(qh * qh, axis=-1, keepdims=True) * inv + EPS) * qn
        q_ref[:, sl] = _rope_apply(qh, c, s).astype(bf16)
        kh = kf[:, sl] + krb
        kh = kh * lax.rsqrt(jnp.sum(kh * kh, axis=-1, keepdims=True) * inv + EPS) * kn
        k_ref[:, sl] = _rope_apply(kh, c, s).astype(bf16)


def _mla_prep(rest, qan, kvan, wuq, wuk, wuv, qn, kn, ctab, stab, l):
    t = rest.shape[0]
    tm = min(512, t)
    lat_blk = R_LAT // 512
    hq = MLA_HEADS * LANES
    vec = lambda n: pl.BlockSpec((None, 1, n), lambda i: (l, 0, 0))
    tab = pl.BlockSpec((tm, LANES), lambda i: (i, 0))
    return pl.pallas_call(
        _mla_prep_kernel,
        grid=(t // tm,),
        in_specs=[
            pl.BlockSpec((tm, 512), lambda i: (i, lat_blk)),
            vec(MLA_Q_RANK), vec(MLA_KV_RANK),
            pl.BlockSpec((None, MLA_Q_RANK, hq), lambda i: (l, 0, 0)),
            pl.BlockSpec((None, MLA_KV_RANK, hq), lambda i: (l, 0, 0)),
            pl.BlockSpec((None, MLA_KV_RANK, MLA_HEADS * MLA_V), lambda i: (l, 0, 0)),
            vec(LANES), vec(LANES), tab, tab,
        ],
        out_specs=[
            pl.BlockSpec((tm, hq), lambda i: (i, 0)),
            pl.BlockSpec((tm, hq), lambda i: (i, 0)),
            pl.BlockSpec((tm, MLA_HEADS * MLA_V), lambda i: (i, 0)),
        ],
        out_shape=[jax.ShapeDtypeStruct((t, hq), bf16), jax.ShapeDtypeStruct((t, hq), bf16),
                   jax.ShapeDtypeStruct((t, MLA_HEADS * MLA_V), bf16)],
        compiler_params=_cparams(("arbitrary",)),
        name="mla_prep",
    )(rest, qan, kvan, wuq, wuk, wuv, qn, kn, ctab, stab)


def _flash_t(q_list, k_ref, kcols, vt_ref, vrows, tile_fn, ntiles, qi, tq,
             m_ref, l_ref, acc_ref, s_refs, e_refs, al_refs):
    nmap = len(q_list)
    m_ref[...] = jnp.full(m_ref.shape, -jnp.inf, f32)
    l_ref[...] = jnp.zeros(l_ref.shape, f32)
    acc_ref[...] = jnp.zeros(acc_ref.shape, f32)
    last = qi

    def qk(t, slot):
        start = pl.multiple_of(t * tq, tq)
        w = jnp.minimum(qi - t, ntiles - 1)
        vals = []
        for a in range(nmap):
            k = k_ref[pl.ds(start, tq), kcols[a]:kcols[a] + LANES]
            vals.append(_dot_nt(k, q_list[a]) + tile_fn(a, w))
        for a in range(nmap):
            s_refs[slot][a] = vals[a]

    def sm(slot):
        for a in range(nmap):
            s = s_refs[slot][a]
            m_prev = m_ref[a]
            m_new = jnp.maximum(m_prev, jnp.max(s, axis=0, keepdims=True))
            alpha = jnp.exp2(m_prev - m_new)
            e = jnp.exp2(s - m_new)
            m_ref[a] = m_new
            e_refs[slot][a] = e.astype(bf16)
            al_refs[slot][a] = alpha

    def pv(t, slot):
        for a in range(nmap):
            vt = vt_ref[t, vrows[a]:vrows[a] + VT_BLOCK, :]
            res = _dot(vt, e_refs[slot][a])
            alpha = al_refs[slot][a]
            acc_ref[a] = alpha * acc_ref[a] + res[:LANES]
            l_ref[a] = alpha * l_ref[a] + res[LANES:LANES + 1]

    def body(t, cur):
        qk(jnp.minimum(t + 1, last), 1 - cur)
        pv(t - 1, 1 - cur)
        sm(cur)

    qk(0, 0)
    qk(jnp.minimum(1, last), 1)
    sm(0)

    def pair(i, carry):
        t = 2 * i + 1
        body(t, 1)
        body(t + 1, 0)
        return carry

    lax.fori_loop(0, last // 2, pair, 0)
    odd = lax.rem(last, 2) == 1

    @pl.when(odd)
    def _():
        body(last, 1)
        pv(last, 1)

    @pl.when(jnp.logical_not(odd))
    def _():
        pv(last, 0)


VT_BLOCK = LANES + 16


def _fill_vt(v_ref, vt_ref, tq):
    ones = jnp.ones((VT_BLOCK - LANES, tq), bf16)
    for c in range(vt_ref.shape[0]):
        vt = v_ref[c * tq:(c + 1) * tq, :].astype(f32).T.astype(bf16)
        for b in range(vt.shape[0] // LANES):
            vt_ref[c, b * VT_BLOCK:b * VT_BLOCK + LANES, :] = vt[b * LANES:(b + 1) * LANES, :]
            vt_ref[c, b * VT_BLOCK + LANES:(b + 1) * VT_BLOCK, :] = ones


def _attn_scratch(nmap, nq, hv, tq):
    sbuf = pltpu.VMEM((nmap, tq, tq), f32)
    ebuf = pltpu.VMEM((nmap, tq, tq), bf16)
    row = pltpu.VMEM((nmap, 1, tq), f32)
    return [pltpu.VMEM((nq, hv // LANES * VT_BLOCK, tq), bf16), row, row, pltpu.VMEM((nmap, LANES, tq), f32),
            sbuf, sbuf, ebuf, ebuf, row, row]


def _diff_attn_kernel(q_ref, k_ref, v_ref, bias_ref, lp_ref, sub_ref, o_ref,
                      vt_ref, m_ref, l_ref, acc_ref, s0_ref, s1_ref, e0_ref, e1_ref, al0_ref, al1_ref,
                      *, lam_init, tq):
    qi = pl.program_id(1)

    @pl.when(qi == 0)
    def _():
        _fill_vt(v_ref, vt_ref, tq)

    lane = _lane_iota((tq, LANES))
    q_list, kcols, vrows = [], [], []
    for h in range(DA_HEADS):
        q = q_ref[:, h * LANES:(h + 1) * LANES]
        zero = jnp.zeros_like(q)
        q_list += [jnp.where(lane < DA_HEAD_DIM, q, zero), jnp.where(lane >= DA_HEAD_DIM, q, zero)]
        kcols += [h * LANES, h * LANES]
        vrows += [h * VT_BLOCK, h * VT_BLOCK]
    lp, sub = lp_ref[...], sub_ref[...]
    _flash_t(q_list, k_ref, kcols, vt_ref, vrows, lambda a, w: bias_ref[a // 2, w], 3, qi, tq,
             m_ref, l_ref, acc_ref, (s0_ref, s1_ref), (e0_ref, e1_ref), (al0_ref, al1_ref))
    lam = (jnp.exp(jnp.sum(lp[0:1] * lp[1:2], axis=-1, keepdims=True))
           - jnp.exp(jnp.sum(lp[2:3] * lp[3:4], axis=-1, keepdims=True)) + lam_init)
    for h in range(DA_HEADS):
        o_t = acc_ref[2 * h] / l_ref[2 * h] - lam * (acc_ref[2 * h + 1] / l_ref[2 * h + 1])
        o = o_t.T
        o = o * lax.rsqrt(jnp.mean(o * o, axis=-1, keepdims=True) + EPS) * sub
        o_ref[:, h * LANES:(h + 1) * LANES] = (o * (1.0 - lam_init)).astype(bf16)


def _diff_attn(qkv3, bias, da_lambda, da_subln, l, lam_init):
    bsz, s, _ = qkv3.shape
    tq = bias.shape[-1]
    nq = s // tq
    hw = DA_HEADS * LANES
    qblk = G_COLS // hw
    kern = functools.partial(_diff_attn_kernel, lam_init=lam_init, tq=tq)
    return pl.pallas_call(
        kern,
        grid=(bsz, nq),
        in_specs=[
            pl.BlockSpec((None, tq, hw), lambda b, i: (b, i, qblk)),
            pl.BlockSpec((None, s, hw), lambda b, i: (b, 0, qblk + 1)),
            pl.BlockSpec((None, s, hw), lambda b, i: (b, 0, qblk + 2)),
            pl.BlockSpec((DA_HEADS, 3, tq, tq), lambda b, i: (0, 0, 0, 0)),
            pl.BlockSpec((None, 4, DA_HEAD_DIM), lambda b, i: (l, 0, 0)),
            pl.BlockSpec((None, 1, LANES), lambda b, i: (l, 0, 0)),
        ],
        out_specs=pl.BlockSpec((None, tq, hw), lambda b, i: (b, i, 0)),
        out_shape=jax.ShapeDtypeStruct((bsz, s, hw), bf16),
        scratch_shapes=_attn_scratch(2 * DA_HEADS, nq, hw, tq),
        compiler_params=_cparams(("arbitrary", "arbitrary")),
        name="diff_attn",
    )(qkv3, qkv3, qkv3, bias, da_lambda, da_subln)


def _mla_attn_kernel(q_ref, k_ref, v_ref, o_ref,
                     vt_ref, m_ref, l_ref, acc_ref, s0_ref, s1_ref, e0_ref, e1_ref, al0_ref, al1_ref, mask_ref,
                     *, tq):
    qi = pl.program_id(1)

    @pl.when(qi == 0)
    def _():
        _fill_vt(v_ref, vt_ref, tq)
        krow = lax.broadcasted_iota(jnp.int32, (tq, tq), 0)
        qcol = lax.broadcasted_iota(jnp.int32, (tq, tq), 1)
        mask_ref[0] = jnp.where(krow <= qcol, 0.0, NEG_BIG)
        mask_ref[1] = jnp.zeros((tq, tq), f32)

    q_list = [q_ref[:, h * LANES:(h + 1) * LANES] for h in range(MLA_HEADS)]
    kcols = [h * LANES for h in range(MLA_HEADS)]
    vrows = [(h // 2) * VT_BLOCK for h in range(MLA_HEADS)]
    _flash_t(q_list, k_ref, kcols, vt_ref, vrows, lambda a, w: mask_ref[w], 2, qi, tq,
             m_ref, l_ref, acc_ref, (s0_ref, s1_ref), (e0_ref, e1_ref), (al0_ref, al1_ref))
    row = lax.broadcasted_iota(jnp.int32, (LANES, tq), 0)
    for p in range(MLA_HEADS // 2):
        o_t = jnp.where(row < MLA_V, acc_ref[2 * p] / l_ref[2 * p], acc_ref[2 * p + 1] / l_ref[2 * p + 1])
        o_ref[:, p * LANES:(p + 1) * LANES] = o_t.T.astype(bf16)


def _mla_attn(q3, k3, v3, tq):
    bsz, s, hq = q3.shape
    nq = s // tq
    hv = v3.shape[-1]
    kern = functools.partial(_mla_attn_kernel, tq=tq)
    return pl.pallas_call(
        kern,
        grid=(bsz, nq),
        in_specs=[
            pl.BlockSpec((None, tq, hq), lambda b, i: (b, i, 0)),
            pl.BlockSpec((None, s, hq), lambda b, i: (b, 0, 0)),
            pl.BlockSpec((None, s, hv), lambda b, i: (b, 0, 0)),
        ],
        out_specs=pl.BlockSpec((None, tq, hv), lambda b, i: (b, i, 0)),
        out_shape=jax.ShapeDtypeStruct((bsz, s, hv), bf16),
        scratch_shapes=_attn_scratch(MLA_HEADS, nq, hv, tq) + [pltpu.VMEM((2, tq, tq), f32)],
        compiler_params=_cparams(("arbitrary", "arbitrary")),
        name="mla_attn",
    )(q3, k3, v3)


def _split3(x):
    hi = x.astype(bf16)
    r1 = x - hi.astype(f32)
    mid = r1.astype(bf16)
    lo = (r1 - mid.astype(f32)).astype(bf16)
    return hi, mid, lo


def _pair_bcast(x, j, lane):
    return jnp.where(lane < SSM_HEAD_DIM, x[:, 2 * j:2 * j + 1], x[:, 2 * j + 1:2 * j + 2])


def _ssd_kernel(xbc_ref, z_ref, misc_ref, cw_ref, cb_ref, dtb_ref, alog_ref, dvec_ref, nw_ref,
                o_ref, ext_ref, st_ref, y_ref):
    c = pl.program_id(1)
    L = SSM_CHUNK
    G, N = SSM_GROUPS, SSM_STATE

    @pl.when(c == 0)
    def _():
        ext_ref[0:SUBLANES, :] = jnp.zeros((SUBLANES, SSM_CONV_DIM), f32)
        st_ref[...] = jnp.zeros(st_ref.shape, f32)

    zs, dvec, nw = _silu(z_ref[...]), dvec_ref[...], nw_ref[...]
    cw, cbias = cw_ref[...], cb_ref[...]
    dt_in = misc_ref[...] + dtb_ref[...]
    alog = alog_ref[...]

    ext_ref[SUBLANES:, :] = xbc_ref[...]
    def rows8(v):
        return v.reshape(L // SUBLANES, SUBLANES, SSM_CONV_DIM)

    conv = cbias + cw[SSM_CONV - 1] * rows8(ext_ref[SUBLANES:, :])
    for s in range(1, SSM_CONV):
        conv = conv + cw[SSM_CONV - 1 - s] * rows8(ext_ref[SUBLANES - s:SUBLANES - s + L, :])
    ext_ref[0:SUBLANES, :] = ext_ref[L:L + SUBLANES, :]
    xa = _silu(conv).reshape(L, SSM_CONV_DIM)
    xs = xa[:, :SSM_D_INNER]
    bm = xa[:, SSM_D_INNER:SSM_D_INNER + G * N]
    cm = xa[:, SSM_D_INNER + G * N:]

    lane = _lane_iota((L, LANES))
    lane1 = _lane_iota((1, LANES))
    dt = jax.nn.softplus(dt_in)
    a_l = jnp.where(lane1 < SSM_HEADS, -jnp.exp(alog), 0.0)
    da = dt * a_l
    row = lax.broadcasted_iota(jnp.int32, (L, L), 0)
    col = lax.broadcasted_iota(jnp.int32, (L, L), 1)
    causal = col <= row
    tri = jnp.where(causal, 1.0, 0.0).astype(bf16)
    hi, mid, lo = _split3(da)
    cs = _dot(tri, hi) + _dot(tri, mid) + _dot(tri, lo)
    cs_t = cs.T
    cs_last = cs[L - 1:L, :]
    w_end = dt * jnp.exp(cs_last - cs)
    e_cs = jnp.exp(cs)
    cdec = jnp.exp(cs_last)

    for g in range(G):
        bm_g = bm[:, g * N:(g + 1) * N]
        cm_g = cm[:, g * N:(g + 1) * N].astype(bf16)
        cb = _dot_nt(cm_g, bm_g.astype(bf16))
        bm_t = bm_g.T.astype(bf16)
        for jj in range(2):
            j = 2 * g + jj
            sl = slice(j * LANES, (j + 1) * LANES)
            psl = slice(jj * LANES, (jj + 1) * LANES)
            xs_p = xs[:, sl]
            xdt = (xs_p * _pair_bcast(dt, j, lane)).astype(bf16)
            yd = []
            for hh in (2 * j, 2 * j + 1):
                seg = cs[:, hh:hh + 1] - cs_t[hh:hh + 1, :]
                decay = jnp.exp(jnp.where(causal, seg, -jnp.inf))
                yd.append(_dot((cb * decay).astype(bf16), xdt))
            y_diag = jnp.where(lane < SSM_HEAD_DIM, yd[0], yd[1])
            prev = st_ref[g, :, psl]
            y_off = _dot(cm_g, prev.astype(bf16)) * _pair_bcast(e_cs, j, lane)
            y_ref[:, sl] = y_diag + y_off + dvec[:, sl] * xs_p
            xw = (xs_p * _pair_bcast(w_end, j, lane)).astype(bf16)
            cd = jnp.where(lane1 < SSM_HEAD_DIM, cdec[:, 2 * j:2 * j + 1], cdec[:, 2 * j + 1:2 * j + 2])
            st_ref[g, :, psl] = prev * cd + _dot(bm_t, xw)

    gw = SSM_D_INNER // G
    for g in range(G):
        sl = slice(g * gw, (g + 1) * gw)
        yz = y_ref[:, sl] * zs[:, sl]
        yz = yz * lax.rsqrt(jnp.mean(yz * yz, axis=-1, keepdims=True) + EPS) * nw[:, sl]
        o_ref[:, sl] = yz.astype(bf16)


def _ssd(rest3, conv_w, conv_b, dt_bias, a_log, dvec, norm_w, l):
    bsz, s, _ = rest3.shape
    L = SSM_CHUNK
    nc = s // L
    misc_blk = (R_LAT + MLA_Q_RANK + MLA_KV_RANK) // LANES
    vec = lambda n: pl.BlockSpec((None, 1, n), lambda b, c: (l, 0, 0))
    return pl.pallas_call(
        _ssd_kernel,
        grid=(bsz, nc),
        in_specs=[
            pl.BlockSpec((None, L, SSM_CONV_DIM), lambda b, c: (b, c, R_XBC // SSM_CONV_DIM)),
            pl.BlockSpec((None, L, SSM_D_INNER), lambda b, c: (b, c, R_Z // SSM_D_INNER)),
            pl.BlockSpec((None, L, LANES), lambda b, c: (b, c, misc_blk)),
            pl.BlockSpec((None, SSM_CONV, SUBLANES, SSM_CONV_DIM), lambda b, c: (l, 0, 0, 0)),
            pl.BlockSpec((None, SUBLANES, SSM_CONV_DIM), lambda b, c: (l, 0, 0)),
            vec(LANES), vec(LANES), vec(SSM_D_INNER), vec(SSM_D_INNER),
        ],
        out_specs=pl.BlockSpec((None, L, SSM_D_INNER), lambda b, c: (b, c, 0)),
        out_shape=jax.ShapeDtypeStruct((bsz, s, SSM_D_INNER), bf16),
        scratch_shapes=[
            pltpu.VMEM((L + SUBLANES, SSM_CONV_DIM), f32),
            pltpu.VMEM((SSM_GROUPS, SSM_STATE, 4 * SSM_HEAD_DIM), f32),
            pltpu.VMEM((L, SSM_D_INNER), f32),
        ],
        compiler_params=_cparams(("arbitrary", "arbitrary")),
        name="ssd",
    )(rest3, rest3, rest3, conv_w, conv_b, dt_bias, a_log, dvec, norm_w)


def _merge_kernel(ya_ref, yb_ref, ys_ref, ga_ref, gb_ref, gc_ref, x_ref, mod_ref,
                  wa_ref, wb_ref, wc_ref, wo_ref, o_ref):
    merged = (ga_ref[...] * _dot(ya_ref[...], wa_ref[...])
              + gb_ref[...] * _dot(yb_ref[...], wb_ref[...])
              + gc_ref[...] * _dot(ys_ref[...], wc_ref[...]))
    o_ref[...] = x_ref[...] + mod_ref[2:3, :] * _dot(merged.astype(bf16), wo_ref[...])


def _merge(ya, yb, ys, gqkv, x2, mod, wa, wb, wc, wo, l, rows_per_batch):
    t, d = x2.shape
    tm = min(512, rows_per_batch)
    tpb = rows_per_batch // tm
    row = lambda n: pl.BlockSpec((tm, n), lambda i: (i, 0))
    wspec = lambda k: pl.BlockSpec((None, k, d), lambda i: (l, 0, 0))
    return pl.pallas_call(
        _merge_kernel,
        grid=(t // tm,),
        in_specs=[
            row(ya.shape[1]), row(yb.shape[1]), row(ys.shape[1]),
            pl.BlockSpec((tm, d), lambda i: (i, 0)),
            pl.BlockSpec((tm, d), lambda i: (i, 1)),
            pl.BlockSpec((tm, d), lambda i: (i, 2)),
            row(d),
            pl.BlockSpec((None, None, 6, d), lambda i: (l, i // tpb, 0, 0)),
            wspec(ya.shape[1]), wspec(yb.shape[1]), wspec(ys.shape[1]), wspec(d),
        ],
        out_specs=row(d),
        out_shape=jax.ShapeDtypeStruct((t, d), f32),
        compiler_params=_cparams(("arbitrary",)),
        name="merge_out",
    )(ya, yb, ys, gqkv, gqkv, gqkv, x2, mod, wa, wb, wc, wo)


FFN_TF = 256
FFN_NF = FFN_DIM // FFN_TF
FFN_BLOCK_WEIGHTS = (1, 1, 1, 1)


def _ffn_kernel(x_ref, nw_ref, mod_ref, wa_ref, wv_ref, cwa_ref, cwv_ref, cba_ref, cbv_ref, wd_ref,
                o_ref, h_ref, ua_ref, uv_ref, acc_ref):
    fi = pl.program_id(1)
    tm = x_ref.shape[0]

    @pl.when(fi == 0)
    def _():
        x = x_ref[...]
        y = x * lax.rsqrt(jnp.mean(x * x, axis=-1, keepdims=True) + EPS) * nw_ref[...]
        h_ref[...] = (y * (1.0 + mod_ref[4:5, :]) + mod_ref[3:4, :]).astype(bf16)
        acc_ref[...] = jnp.zeros(acc_ref.shape, f32)
        ua_ref[0:SUBLANES, :] = jnp.zeros((SUBLANES, FFN_TF), f32)
        uv_ref[0:SUBLANES, :] = jnp.zeros((SUBLANES, FFN_TF), f32)

    wa, wv, wd = wa_ref[...], wv_ref[...], wd_ref[...]
    cwa, cwv, cba, cbv = cwa_ref[...], cwv_ref[...], cba_ref[...], cbv_ref[...]

    sizes =[tm * w // sum(FFN_BLOCK_WEIGHTS) for w in FFN_BLOCK_WEIGHTS]
    assert sum(sizes) == tm and all(n % 16 == 0 for n in sizes)
    starts = [sum(sizes[:i]) for i in range(len(sizes))]

    def conv(u_ref, cw, cb, base, n):
        out = cb + cw[FFN_CONV - 1:FFN_CONV, :] * u_ref[base:base + n, :]
        for s in range(1, FFN_CONV):
            out = out + cw[FFN_CONV - 1 - s:FFN_CONV - s, :] * u_ref[base - s:base - s + n, :]
        return out

    def up(rb):
        r0, n = starts[rb], sizes[rb]
        hb = h_ref[r0:r0 + n, :]
        ua_ref[SUBLANES + r0:SUBLANES + r0 + n, :] = _dot(hb, wa)
        uv_ref[SUBLANES + r0:SUBLANES + r0 + n, :] = _dot(hb, wv)

    up(0)
    for rb in range(len(sizes)):
        if rb + 1 < len(sizes):
            up(rb + 1)
        r0, n = starts[rb], sizes[rb]
        base = SUBLANES + r0
        act = (_silu(conv(ua_ref, cwa, cba, base, n)) * conv(uv_ref, cwv, cbv, base, n)).astype(bf16)
        acc_ref[r0:r0 + n, :] += _dot(act, wd)

    @pl.when(fi == pl.num_programs(1) - 1)
    def _():
        o_ref[...] = x_ref[...] + mod_ref[5:6, :] * acc_ref[...]


def _ffn(x3, norm_w, mod, wup, conv_w, conv_b, wdown, l):
    bsz, s, d = x3.shape
    tf, nf = FFN_TF, FFN_NF
    return pl.pallas_call(
        _ffn_kernel,
        grid=(bsz, nf),
        in_specs=[
            pl.BlockSpec((None, s, d), lambda b, f: (b, 0, 0)),
            pl.BlockSpec((None, 1, d), lambda b, f: (l, 0, 0)),
            pl.BlockSpec((None, None, 6, d), lambda b, f: (l, b, 0, 0)),
            pl.BlockSpec((None, d, tf), lambda b, f: (l, 0, f)),
            pl.BlockSpec((None, d, tf), lambda b, f: (l, 0, nf + f)),
            pl.BlockSpec((None, FFN_CONV, tf), lambda b, f: (l, 0, f)),
            pl.BlockSpec((None, FFN_CONV, tf), lambda b, f: (l, 0, nf + f)),
            pl.BlockSpec((None, 1, tf), lambda b, f: (l, 0, f)),
            pl.BlockSpec((None, 1, tf), lambda b, f: (l, 0, nf + f)),
            pl.BlockSpec((None, tf, d), lambda b, f: (l, f, 0)),
        ],
        out_specs=pl.BlockSpec((None, s, d), lambda b, f: (b, 0, 0)),
        out_shape=jax.ShapeDtypeStruct((bsz, s, d), f32),
        scratch_shapes=[
            pltpu.VMEM((s, d), bf16),
            pltpu.VMEM((s + SUBLANES, tf), f32), pltpu.VMEM((s + SUBLANES, tf), f32),
            pltpu.VMEM((s, d), f32),
        ],
        compiler_params=_cparams(("arbitrary", "arbitrary")),
        name="ffn",
    )(x3, norm_w, mod, wup, wup, conv_w, conv_w, conv_b, conv_b, wdown)


def _split_w_in(w_in):
    nl, d, _ = w_in.shape
    sizes = (512, 512, 512, MLA_Q_RANK, MLA_KV_RANK, MLA_ROPE, SSM_D_INNER, SSM_CONV_DIM, SSM_HEADS, 3 * D_MODEL)
    offs = np.concatenate([[0], np.cumsum(sizes)])
    qa, ka, va, cq, ckv, kr, z, xbc, dt, gates = [w_in[..., int(offs[i]):int(offs[i + 1])] for i in range(10)]
    zeros = lambda n: jnp.zeros((nl, d, n), w_in.dtype)
    misc = jnp.concatenate([dt, zeros(ROPE_X1 - SSM_HEADS), kr[..., :HALF_ROPE], zeros(ROPE_X2 - ROPE_X1 - HALF_ROPE),
                            kr[..., HALF_ROPE:], zeros(LANES - ROPE_X2 - HALF_ROPE)], axis=-1)
    qkv = w_in[..., :int(offs[3])]
    lat = jnp.concatenate([cq, ckv, misc], axis=-1)
    return tuple(w.astype(bf16) for w in (gates, qkv, xbc, z, lat))


def _pad_lanes(v, n):
    return jnp.pad(v, [(0, 0)] * (v.ndim - 1) + [(0, n - v.shape[-1])])


def _to_head_slot(v):
    return jnp.zeros(v.shape[:-1] + (LANES,), v.dtype).at[..., LANE_OF_DIM[:v.shape[-1]]].set(v)


def kernel(x, c, positions, rel_bias, ada_w, ada_b, norm_mix, norm_ffn, w_in, da_q_norm, da_k_norm, da_lambda, da_subln, mla_q_a_norm, mla_kv_a_norm, mla_w_uq, mla_w_ukv, mla_q_norm, mla_k_norm, ssm_conv_w, ssm_conv_b, ssm_dt_bias, ssm_a_log, ssm_d, ssm_norm, w_branch_a, w_branch_b, w_branch_c, w_out, ffn_w_up, ffn_conv_w, ffn_conv_b, ffn_w_down):
    bsz, seq, d = x.shape
    nl = w_in.shape[0]
    t = bsz * seq
    tq = min(ATT_TQ, seq)

    w_parts = _split_w_in(w_in)
    qk_w = jnp.stack([jnp.tile(da_q_norm, (1, 2 * DA_HEADS)) * (DA_HEAD_DIM ** -0.5 * LOG2E),
                      jnp.tile(da_k_norm, (1, 2 * DA_HEADS))], axis=1).reshape(nl, 2, 1, IN_TN)
    blk = np.arange(IN_TN) // DA_HEAD_DIM
    bd = jnp.asarray(blk[:, None] == blk[None, :], bf16)
    wuq = _to_head_slot(mla_w_uq.reshape(nl, MLA_Q_RANK, MLA_HEADS, MLA_QK))
    wuq = wuq.reshape(nl, MLA_Q_RANK, MLA_HEADS * LANES).astype(bf16)
    wukv = mla_w_ukv.reshape(nl, MLA_KV_RANK, MLA_HEADS, MLA_NOPE + MLA_V)
    wuk = _to_head_slot(wukv[..., :MLA_NOPE]).reshape(nl, MLA_KV_RANK, MLA_HEADS * LANES).astype(bf16)
    wuv = wukv[..., MLA_NOPE:].reshape(nl, MLA_KV_RANK, MLA_HEADS * MLA_V).astype(bf16)
    qn = (_to_head_slot(mla_q_norm) * (MLA_QK ** -0.5 * LOG2E)).reshape(nl, 1, LANES)
    kn = _to_head_slot(mla_k_norm).reshape(nl, 1, LANES)
    dt_bias = _pad_lanes(ssm_dt_bias, LANES).reshape(nl, 1, LANES)
    a_log = _pad_lanes(ssm_a_log, LANES).reshape(nl, 1, LANES)
    dvec = jnp.repeat(ssm_d, SSM_HEAD_DIM, axis=-1).reshape(nl, 1, SSM_D_INNER)
    conv_w8 = jnp.broadcast_to(ssm_conv_w[:, :, None, :], (nl, SSM_CONV, SUBLANES, SSM_CONV_DIM))
    conv_b8 = jnp.broadcast_to(ssm_conv_b[:, None, :], (nl, SUBLANES, SSM_CONV_DIM))
    v3 = lambda a: a.reshape(nl, 1, a.shape[-1])
    wa, wb, wc, wo = (w.astype(bf16) for w in (w_branch_a, w_branch_b, w_branch_c, w_out))
    wup, wdown = ffn_w_up.astype(bf16), ffn_w_down.astype(bf16)

    mod = _ada_mod(c, ada_w, ada_b).reshape(nl, bsz, 6, d)
    ctab, stab = _rope_tables(positions)
    bias = _bias_tiles(rel_bias, tq)

    x2 = x.reshape(t, d)
    for l in range(nl):
        lam_init = 0.8 - 0.6 * math.exp(-0.3 * l)
        gqkv, rest = _inproj(x2, v3(norm_mix), mod, w_parts, qk_w, bd, l, seq)
        q_b, k_b, v_b = _mla_prep(rest, v3(mla_q_a_norm), v3(mla_kv_a_norm), wuq, wuk, wuv, qn, kn, ctab, stab, l)
        ya = _diff_attn(gqkv.reshape(bsz, seq, GQKV_COLS), bias, da_lambda, v3(da_subln), l, lam_init)
        yb = _mla_attn(q_b.reshape(bsz, seq, -1), k_b.reshape(bsz, seq, -1), v_b.reshape(bsz, seq, -1),
                       min(MLA_TQ, seq))
        ys = _ssd(rest.reshape(bsz, seq, REST_COLS), conv_w8, conv_b8, dt_bias, a_log, dvec, v3(ssm_norm), l)
        x2 = _merge(ya.reshape(t, -1), yb.reshape(t, -1), ys.reshape(t, -1), gqkv, x2, mod, wa, wb, wc, wo, l, seq)
        x2 = _ffn(x2.reshape(bsz, seq, d), v3(norm_ffn), mod, wup, ffn_conv_w, v3(ffn_conv_b), wdown, l).reshape(t, d)
    return x2.reshape(bsz, seq, d)
```

```python
import functools
import math

import jax
import jax.numpy as jnp
import numpy as np
from jax import lax
from jax.experimental import pallas as pl
from jax.experimental.pallas import tpu as pltpu

f32 = jnp.float32
bf16 = jnp.bfloat16

D_MODEL = 1024
DA_HEADS = 4
DA_HEAD_DIM = 64
MLA_HEADS = 8
MLA_Q_RANK = 256
MLA_KV_RANK = 128
MLA_NOPE = 64
MLA_ROPE = 32
MLA_V = 64
MLA_QK = MLA_NOPE + MLA_ROPE
SSM_D_INNER = 1024
SSM_HEAD_DIM = 64
SSM_HEADS = 16
SSM_GROUPS = 4
SSM_STATE = 128
SSM_CONV = 4
SSM_CHUNK = 128
SSM_CONV_DIM = SSM_D_INNER + 2 * SSM_GROUPS * SSM_STATE
FFN_DIM = 2816
FFN_CONV = 3
REL_BUCKETS = 32
REL_MAX_DIST = 128
ROPE_THETA = 10000.0
EPS = 1e-6

LANES = 128
SUBLANES = 8
VMEM_LIMIT = 56 * 1024 * 1024

IN_TN = 512
G_COLS = 3 * D_MODEL
GQKV_COLS = G_COLS + 3 * 512
R_XBC = 0
R_Z = R_XBC + SSM_CONV_DIM
R_LAT = R_Z + SSM_D_INNER
REST_COLS = R_LAT + 512
ROPE_X1 = 32
ROPE_X2 = ROPE_X1 + LANES // 2
HALF_ROPE = MLA_ROPE // 2
LANE_OF_DIM = np.concatenate([np.arange(0, 32), np.arange(48, 80),
                              np.arange(ROPE_X1, ROPE_X1 + HALF_ROPE), np.arange(ROPE_X2, ROPE_X2 + HALF_ROPE)])
NEG_BIG = -1e30
LOG2E = math.log2(math.e)

ATT_TQ = 256
MLA_TQ = 256


def _cparams(sem):
    return pltpu.CompilerParams(dimension_semantics=sem, vmem_limit_bytes=VMEM_LIMIT)


def _lane_iota(shape):
    return lax.broadcasted_iota(jnp.int32, shape, len(shape) - 1)


def _dot(a, b):
    return jnp.dot(a, b, preferred_element_type=f32)


def _dot_nt(a, b):
    return lax.dot_general(a, b, (((1,), (1,)), ((), ())), preferred_element_type=f32)


def _sigmoid(x):
    return 1.0 / (1.0 + jnp.exp(-x))


def _silu(x):
    return x * _sigmoid(x)


def _ada_kernel(c_ref, w_ref, b_ref, o_ref):
    ca = _silu(c_ref[...]).astype(bf16)
    o_ref[...] = _dot(ca, w_ref[...].astype(bf16)) + b_ref[...]


def _ada_mod(c, ada_w, ada_b):
    nl, d, n = ada_w.shape
    bsz = c.shape[0]
    tn = 1024
    return pl.pallas_call(
        _ada_kernel,
        grid=(nl, n // tn),
        in_specs=[
            pl.BlockSpec((bsz, d), lambda l, j: (0, 0)),
            pl.BlockSpec((None, d, tn), lambda l, j: (l, 0, j)),
            pl.BlockSpec((None, 1, tn), lambda l, j: (l, 0, j)),
        ],
        out_specs=pl.BlockSpec((None, bsz, tn), lambda l, j: (l, 0, j)),
        out_shape=jax.ShapeDtypeStruct((nl, bsz, n), f32),
        compiler_params=_cparams(("arbitrary", "arbitrary")),
        name="ada_mod",
    )(c, ada_w, ada_b.reshape(nl, 1, n))


def _rope_kernel(pos_ref, freq_ref, c_ref, s_ref):
    ang = pos_ref[...].astype(f32) * freq_ref[...]
    cos, sin = jnp.cos(ang), jnp.sin(ang)
    lane = _lane_iota(ang.shape)
    first = (lane >= ROPE_X1) & (lane < ROPE_X1 + HALF_ROPE)
    second = (lane >= ROPE_X2) & (lane < ROPE_X2 + HALF_ROPE)
    c_ref[...] = jnp.where(first | second, cos, 1.0)
    s_ref[...] = jnp.where(first, -sin, jnp.where(second, sin, 0.0))


def _rope_tables(positions):
    t = positions.size
    tm = min(t, 2048)
    inv_freq = ROPE_THETA ** (-jnp.arange(0, MLA_ROPE, 2, dtype=f32) / MLA_ROPE)
    freq = jnp.zeros((LANES,), f32).at[LANE_OF_DIM[MLA_NOPE:]].set(jnp.tile(inv_freq, 2))
    out = jax.ShapeDtypeStruct((t, LANES), f32)
    spec = pl.BlockSpec((tm, LANES), lambda i: (i, 0))
    return pl.pallas_call(
        _rope_kernel,
        grid=(t // tm,),
        in_specs=[spec, pl.BlockSpec((1, LANES), lambda i: (0, 0))],
        out_specs=[spec, spec],
        out_shape=[out, out],
        compiler_params=_cparams(("arbitrary",)),
        name="rope_tables",
    )(jnp.broadcast_to(positions.reshape(t, 1), (t, LANES)), freq.reshape(1, LANES))


def _bias_kernel(rb_ref, o_ref):
    h, which = pl.program_id(0), pl.program_id(1)
    tq = o_ref.shape[-1]
    row = lax.broadcasted_iota(jnp.int32, (tq, tq), 0)
    col = lax.broadcasted_iota(jnp.int32, (tq, tq), 1)
    rel = col - row + which * tq
    d = jnp.maximum(rel, 0)
    max_exact = REL_BUCKETS // 2
    large = max_exact + (jnp.log(jnp.maximum(d, 1).astype(f32) / max_exact)
                         / math.log(REL_MAX_DIST / max_exact) * (REL_BUCKETS - max_exact)).astype(jnp.int32)
    large = jnp.minimum(large, REL_BUCKETS - 1)
    bucket = jnp.where(d < max_exact, d, large)
    val = jnp.full((tq, tq), rb_ref[0, h], f32)
    for k in range(1, REL_BUCKETS):
        val = jnp.where(bucket == k, rb_ref[k, h], val)
    val = (val - rb_ref[REL_BUCKETS - 1, h]) * LOG2E
    val = jnp.where(rel < 0, NEG_BIG, val)
    o_ref[...] = jnp.where(which >= 2, 0.0, val)


def _bias_tiles(rel_bias, tq):
    return pl.pallas_call(
        _bias_kernel,
        grid=(DA_HEADS, 3),
        in_specs=[pl.BlockSpec(memory_space=pltpu.SMEM)],
        out_specs=pl.BlockSpec((None, None, tq, tq), lambda h, w: (h, w, 0, 0)),
        out_shape=jax.ShapeDtypeStruct((DA_HEADS, 3, tq, tq), f32),
        compiler_params=_cparams(("arbitrary", "arbitrary")),
        name="bias_tiles",
    )(rel_bias)


def _inproj_kernel(x_ref, nw_ref, mod_ref, wg_ref, wqkv_ref, wxbc_ref, wz_ref, wlat_ref, qkw_ref, bd_ref,
                   gqkv_ref, rest_ref, h_ref):
    j = pl.program_id(1)
    jq = G_COLS // IN_TN
    jx = GQKV_COLS // IN_TN
    jz = jx + SSM_CONV_DIM // IN_TN
    jl = jz + SSM_D_INNER // IN_TN

    @pl.when(j == 0)
    def _():
        x = x_ref[...]
        y = x * lax.rsqrt(jnp.mean(x * x, axis=-1, keepdims=True) + EPS) * nw_ref[...]
        h_ref[...] = (y * (1.0 + mod_ref[1:2, :]) + mod_ref[0:1, :]).astype(bf16)

    def proj(w_ref):
        return _dot(h_ref[...], w_ref[...])

    @pl.when(j < jq)
    def _():
        gqkv_ref[...] = (0.5 * jnp.tanh(0.5 * proj(wg_ref)) + 0.5).astype(bf16)

    @pl.when((j >= jq) & (j < jq + 2))
    def _():
        acc = proj(wqkv_ref)
        ss = _dot((acc * acc).astype(bf16), bd_ref[...]) * (1.0 / DA_HEAD_DIM)
        gqkv_ref[...] = (acc * lax.rsqrt(ss + EPS) * qkw_ref[...]).astype(bf16)

    @pl.when(j == jq + 2)
    def _():
        gqkv_ref[...] = proj(wqkv_ref).astype(bf16)

    @pl.when((j >= jx) & (j < jz))
    def _():
        rest_ref[...] = proj(wxbc_ref)

    @pl.when((j >= jz) & (j < jl))
    def _():
        rest_ref[...] = proj(wz_ref)

    @pl.when(j >= jl)
    def _():
        rest_ref[...] = proj(wlat_ref)


def _inproj(x2, norm_w, mod, w_parts, qk_w, bd, l, rows_per_batch):
    t, d = x2.shape
    tm = rows_per_batch
    nj = (GQKV_COLS + REST_COLS) // IN_TN
    nb = GQKV_COLS // IN_TN
    jq = G_COLS // IN_TN
    jz = nb + SSM_CONV_DIM // IN_TN

    def wspec(w, first):
        last = w.shape[-1] // IN_TN - 1
        return pl.BlockSpec((None, d, IN_TN), lambda i, j: (l, 0, jnp.clip(j - first, 0, last)))
    return pl.pallas_call(
        _inproj_kernel,
        grid=(t // tm, nj),
        in_specs=[
            pl.BlockSpec((tm, d), lambda i, j: (i, 0)),
            pl.BlockSpec((None, 1, d), lambda i, j: (l, 0, 0)),
            pl.BlockSpec((None, None, 6, d), lambda i, j: (l, i, 0, 0)),
            wspec(w_parts[0], 0), wspec(w_parts[1], jq), wspec(w_parts[2], nb), wspec(w_parts[3], jz),
            wspec(w_parts[4], nj - 1),
            pl.BlockSpec((None, None, 1, IN_TN), lambda i, j: (l, jnp.clip(j - jq, 0, 1), 0, 0)),
            pl.BlockSpec((IN_TN, IN_TN), lambda i, j: (0, 0)),
        ],
        out_specs=[
            pl.BlockSpec((tm, IN_TN), lambda i, j: (i, jnp.minimum(j, nb - 1))),
            pl.BlockSpec((tm, IN_TN), lambda i, j: (i, jnp.maximum(j - nb, 0))),
        ],
        out_shape=[jax.ShapeDtypeStruct((t, GQKV_COLS), bf16), jax.ShapeDtypeStruct((t, REST_COLS), f32)],
        scratch_shapes=[pltpu.VMEM((tm, d), bf16)],
        compiler_params=_cparams(("arbitrary", "arbitrary")),
        name="inproj",
    )(x2, norm_w, mod, *w_parts, qk_w, bd)


def _rope_apply(x, c, s):
    return x * c + pltpu.roll(x, LANES // 2, 1) * s


def _mla_prep_kernel(lat_ref, qan_ref, kvan_ref, wuq_ref, wuk_ref, wuv_ref, qn_ref, kn_ref,
                     c_ref, s_ref, q_ref, k_ref, v_ref):
    lat = lat_ref[...]
    c, s = c_ref[...], s_ref[...]
    cq = lat[:, :MLA_Q_RANK]
    cqn = (cq * lax.rsqrt(jnp.mean(cq * cq, axis=-1, keepdims=True) + EPS) * qan_ref[...]).astype(bf16)
    ckv = lat[:, MLA_Q_RANK:MLA_Q_RANK + MLA_KV_RANK]
    ckvn = (ckv * lax.rsqrt(jnp.mean(ckv * ckv, axis=-1, keepdims=True) + EPS) * kvan_ref[...]).astype(bf16)
    misc = lat[:, MLA_Q_RANK + MLA_KV_RANK:]
    lane = _lane_iota(misc.shape)
    in_rope = (((lane >= ROPE_X1) & (lane < ROPE_X1 + HALF_ROPE))
               | ((lane >= ROPE_X2) & (lane < ROPE_X2 + HALF_ROPE)))
    krb = jnp.where(in_rope, misc, 0.0)
    qf = _dot(cqn, wuq_ref[...])
    kf = _dot(ckvn, wuk_ref[...])
    v_ref[...] = _dot(ckvn, wuv_ref[...]).astype(bf16)
    inv = 1.0 / MLA_QK
    qn, kn = qn_ref[...], kn_ref[...]
    for h in range(MLA_HEADS):
        sl = slice(h * LANES, (h + 1) * LANES)
        qh = qf[:, sl]
        qh = qh * lax.rsqrt(jnp.sum(qh * qh, axis=-1, keepdims=True) * inv + EPS) * qn
        q_ref[:, sl] = _rope_apply(qh, c, s).astype(bf16)
        kh = kf[:, sl] + krb
        kh = kh * lax.rsqrt(jnp.sum(kh * kh, axis=-1, keepdims=True) * inv + EPS) * kn
        k_ref[:, sl] = _rope_apply(kh, c, s).astype(bf16)


def _mla_prep(rest, qan, kvan, wuq, wuk, wuv, qn, kn, ctab, stab, l):
    t = rest.shape[0]
    tm = min(1024, t)
    lat_blk = R_LAT // 512
    hq = MLA_HEADS * LANES
    vec = lambda n: pl.BlockSpec((None, 1, n), lambda i: (l, 0, 0))
    tab = pl.BlockSpec((tm, LANES), lambda i: (i, 0))
    return pl.pallas_call(
        _mla_prep_kernel,
        grid=(t // tm,),
        in_specs=[
            pl.BlockSpec((tm, 512), lambda i: (i, lat_blk)),
            vec(MLA_Q_RANK), vec(MLA_KV_RANK),
            pl.BlockSpec((None, MLA_Q_RANK, hq), lambda i: (l, 0, 0)),
            pl.BlockSpec((None, MLA_KV_RANK, hq), lambda i: (l, 0, 0)),
            pl.BlockSpec((None, MLA_KV_RANK, MLA_HEADS * MLA_V), lambda i: (l, 0, 0)),
            vec(LANES), vec(LANES), tab, tab,
        ],
        out_specs=[
            pl.BlockSpec((tm, hq), lambda i: (i, 0)),
            pl.BlockSpec((tm, hq), lambda i: (i, 0)),
            pl.BlockSpec((tm, MLA_HEADS * MLA_V), lambda i: (i, 0)),
        ],
        out_shape=[jax.ShapeDtypeStruct((t, hq), bf16), jax.ShapeDtypeStruct((t, hq), bf16),
                   jax.ShapeDtypeStruct((t, MLA_HEADS * MLA_V), bf16)],
        compiler_params=_cparams(("arbitrary",)),
        name="mla_prep",
    )(rest, qan, kvan, wuq, wuk, wuv, qn, kn, ctab, stab)


def _flash_t(q_list, k_ref, kcols, vt_ref, vrows, tile_fn, ntiles, qi, tq,
             m_ref, l_ref, acc_ref, s_refs, e_refs, al_refs):
    nmap = len(q_list)
    m_ref[...] = jnp.full(m_ref.shape, -jnp.inf, f32)
    l_ref[...] = jnp.zeros(l_ref.shape, f32)
    acc_ref[...] = jnp.zeros(acc_ref.shape, f32)
    last = qi

    def qk(t, slot):
        start = pl.multiple_of(t * tq, tq)
        w = jnp.minimum(qi - t, ntiles - 1)
        vals = []
        for a in range(nmap):
            k = k_ref[pl.ds(start, tq), kcols[a]:kcols[a] + LANES]
            vals.append(_dot_nt(k, q_list[a]) + tile_fn(a, w))
        for a in range(nmap):
            s_refs[slot][a] = vals[a]

    def sm(slot):
        for a in range(nmap):
            s = s_refs[slot][a]
            m_prev = m_ref[a]
            m_new = jnp.maximum(m_prev, jnp.max(s, axis=0, keepdims=True))
            alpha = jnp.exp2(m_prev - m_new)
            e = jnp.exp2(s - m_new)
            m_ref[a] = m_new
            e_refs[slot][a] = e.astype(bf16)
            al_refs[slot][a] = alpha

    def pv(t, slot):
        for a in range(nmap):
            vt = vt_ref[t, vrows[a]:vrows[a] + VT_BLOCK, :]
            res = _dot(vt, e_refs[slot][a])
            alpha = al_refs[slot][a]
            acc_ref[a] = alpha * acc_ref[a] + res[:LANES]
            l_ref[a] = alpha * l_ref[a] + res[LANES:LANES + 1]

    def body(t, cur):
        qk(jnp.minimum(t + 1, last), 1 - cur)
        pv(t - 1, 1 - cur)
        sm(cur)

    qk(0, 0)
    qk(jnp.minimum(1, last), 1)
    sm(0)

    def pair(i, carry):
        t = 2 * i + 1
        body(t, 1)
        body(t + 1, 0)
        return carry

    lax.fori_loop(0, last // 2, pair, 0)
    odd = lax.rem(last, 2) == 1

    @pl.when(odd)
    def _():
        body(last, 1)
        pv(last, 1)

    @pl.when(jnp.logical_not(odd))
    def _():
        pv(last, 0)


VT_BLOCK = LANES + 16


def _fill_vt(v_ref, vt_ref, tq):
    ones = jnp.ones((VT_BLOCK - LANES, tq), bf16)
    for c in range(vt_ref.shape[0]):
        vt = v_ref[c * tq:(c + 1) * tq, :].astype(f32).T.astype(bf16)
        for b in range(vt.shape[0] // LANES):
            vt_ref[c, b * VT_BLOCK:b * VT_BLOCK + LANES, :] = vt[b * LANES:(b + 1) * LANES, :]
            vt_ref[c, b * VT_BLOCK + LANES:(b + 1) * VT_BLOCK, :] = ones


def _attn_scratch(nmap, nq, hv, tq):
    sbuf = pltpu.VMEM((nmap, tq, tq), f32)
    ebuf = pltpu.VMEM((nmap, tq, tq), bf16)
    row = pltpu.VMEM((nmap, 1, tq), f32)
    return [pltpu.VMEM((nq, hv // LANES * VT_BLOCK, tq), bf16), row, row, pltpu.VMEM((nmap, LANES, tq), f32),
            sbuf, sbuf, ebuf, ebuf, row, row]


def _diff_attn_kernel(q_ref, k_ref, v_ref, bias_ref, lp_ref, sub_ref, o_ref,
                      vt_ref, m_ref, l_ref, acc_ref, s0_ref, s1_ref, e0_ref, e1_ref, al0_ref, al1_ref,
                      *, lam_init, tq):
    qi = pl.program_id(1)

    @pl.when(qi == 0)
    def _():
        _fill_vt(v_ref, vt_ref, tq)

    lane = _lane_iota((tq, LANES))
    q_list, kcols, vrows = [], [], []
    for h in range(DA_HEADS):
        q = q_ref[:, h * LANES:(h + 1) * LANES]
        zero = jnp.zeros_like(q)
        q_list += [jnp.where(lane < DA_HEAD_DIM, q, zero), jnp.where(lane >= DA_HEAD_DIM, q, zero)]
        kcols += [h * LANES, h * LANES]
        vrows += [h * VT_BLOCK, h * VT_BLOCK]
    lp, sub = lp_ref[...], sub_ref[...]
    _flash_t(q_list, k_ref, kcols, vt_ref, vrows, lambda a, w: bias_ref[a // 2, w], 3, qi, tq,
             m_ref, l_ref, acc_ref, (s0_ref, s1_ref), (e0_ref, e1_ref), (al0_ref, al1_ref))
    lam = (jnp.exp(jnp.sum(lp[0:1] * lp[1:2], axis=-1, keepdims=True))
           - jnp.exp(jnp.sum(lp[2:3] * lp[3:4], axis=-1, keepdims=True)) + lam_init)
    for h in range(DA_HEADS):
        o_t = acc_ref[2 * h] / l_ref[2 * h] - lam * (acc_ref[2 * h + 1] / l_ref[2 * h + 1])
        o = o_t.T
        o = o * lax.rsqrt(jnp.mean(o * o, axis=-1, keepdims=True) + EPS) * sub
        o_ref[:, h * LANES:(h + 1) * LANES] = (o * (1.0 - lam_init)).astype(bf16)


def _diff_attn(qkv3, bias, da_lambda, da_subln, l, lam_init):
    bsz, s, _ = qkv3.shape
    tq = bias.shape[-1]
    nq = s // tq
    hw = DA_HEADS * LANES
    qblk = G_COLS // hw
    kern = functools.partial(_diff_attn_kernel, lam_init=lam_init, tq=tq)
    return pl.pallas_call(
        kern,
        grid=(bsz, nq),
        in_specs=[
            pl.BlockSpec((None, tq, hw), lambda b, i: (b, i, qblk)),
            pl.BlockSpec((None, s, hw), lambda b, i: (b, 0, qblk + 1)),
            pl.BlockSpec((None, s, hw), lambda b, i: (b, 0, qblk + 2)),
            pl.BlockSpec((DA_HEADS, 3, tq, tq), lambda b, i: (0, 0, 0, 0)),
            pl.BlockSpec((None, 4, DA_HEAD_DIM), lambda b, i: (l, 0, 0)),
            pl.BlockSpec((None, 1, LANES), lambda b, i: (l, 0, 0)),
        ],
        out_specs=pl.BlockSpec((None, tq, hw), lambda b, i: (b, i, 0)),
        out_shape=jax.ShapeDtypeStruct((bsz, s, hw), bf16),
        scratch_shapes=_attn_scratch(2 * DA_HEADS, nq, hw, tq),
        compiler_params=_cparams(("arbitrary", "arbitrary")),
        name="diff_attn",
    )(qkv3, qkv3, qkv3, bias, da_lambda, da_subln)


def _mla_attn_kernel(q_ref, k_ref, v_ref, o_ref,
                     vt_ref, m_ref, l_ref, acc_ref, s0_ref, s1_ref, e0_ref, e1_ref, al0_ref, al1_ref, mask_ref,
                     *, tq):
    qi = pl.program_id(1)

    @pl.when(qi == 0)
    def _():
        _fill_vt(v_ref, vt_ref, tq)
        krow = lax.broadcasted_iota(jnp.int32, (tq, tq), 0)
        qcol = lax.broadcasted_iota(jnp.int32, (tq, tq), 1)
        mask_ref[0] = jnp.where(krow <= qcol, 0.0, NEG_BIG)
        mask_ref[1] = jnp.zeros((tq, tq), f32)

    q_list = [q_ref[:, h * LANES:(h + 1) * LANES] for h in range(MLA_HEADS)]
    kcols = [h * LANES for h in range(MLA_HEADS)]
    vrows = [(h // 2) * VT_BLOCK for h in range(MLA_HEADS)]
    _flash_t(q_list, k_ref, kcols, vt_ref, vrows, lambda a, w: mask_ref[w], 2, qi, tq,
             m_ref, l_ref, acc_ref, (s0_ref, s1_ref), (e0_ref, e1_ref), (al0_ref, al1_ref))
    row = lax.broadcasted_iota(jnp.int32, (LANES, tq), 0)
    for p in range(MLA_HEADS // 2):
        o_t = jnp.where(row < MLA_V, acc_ref[2 * p] / l_ref[2 * p], acc_ref[2 * p + 1] / l_ref[2 * p + 1])
        o_ref[:, p * LANES:(p + 1) * LANES] = o_t.T.astype(bf16)


def _mla_attn(q3, k3, v3, tq):
    bsz, s, hq = q3.shape
    nq = s // tq
    hv = v3.shape[-1]
    kern = functools.partial(_mla_attn_kernel, tq=tq)
    return pl.pallas_call(
        kern,
        grid=(bsz, nq),
        in_specs=[
            pl.BlockSpec((None, tq, hq), lambda b, i: (b, i, 0)),
            pl.BlockSpec((None, s, hq), lambda b, i: (b, 0, 0)),
            pl.BlockSpec((None, s, hv), lambda b, i: (b, 0, 0)),
        ],
        out_specs=pl.BlockSpec((None, tq, hv), lambda b, i: (b, i, 0)),
        out_shape=jax.ShapeDtypeStruct((bsz, s, hv), bf16),
        scratch_shapes=_attn_scratch(MLA_HEADS, nq, hv, tq) + [pltpu.VMEM((2, tq, tq), f32)],
        compiler_params=_cparams(("arbitrary", "arbitrary")),
        name="mla_attn",
    )(q3, k3, v3)


def _split3(x):
    hi = x.astype(bf16)
    r1 = x - hi.astype(f32)
    mid = r1.astype(bf16)
    lo = (r1 - mid.astype(f32)).astype(bf16)
    return hi, mid, lo


def _pair_bcast(x, j, lane):
    return jnp.where(lane < SSM_HEAD_DIM, x[:, 2 * j:2 * j + 1], x[:, 2 * j + 1:2 * j + 2])


def _ssd_kernel(xbc_ref, z_ref, misc_ref, cw_ref, cb_ref, dtb_ref, alog_ref, dvec_ref, nw_ref,
                o_ref, ext_ref, st_ref, y_ref):
    c = pl.program_id(1)
    L = SSM_CHUNK
    G, N = SSM_GROUPS, SSM_STATE

    @pl.when(c == 0)
    def _():
        ext_ref[0:SUBLANES, :] = jnp.zeros((SUBLANES, SSM_CONV_DIM), f32)
        st_ref[...] = jnp.zeros(st_ref.shape, f32)

    zs, dvec, nw = _silu(z_ref[...]), dvec_ref[...], nw_ref[...]
    cw, cbias = cw_ref[...], cb_ref[...]
    dt_in = misc_ref[...] + dtb_ref[...]
    alog = alog_ref[...]

    ext_ref[SUBLANES:, :] = xbc_ref[...]
    def rows8(v):
        return v.reshape(L // SUBLANES, SUBLANES, SSM_CONV_DIM)

    conv = cbias + cw[SSM_CONV - 1] * rows8(ext_ref[SUBLANES:, :])
    for s in range(1, SSM_CONV):
        conv = conv + cw[SSM_CONV - 1 - s] * rows8(ext_ref[SUBLANES - s:SUBLANES - s + L, :])
    ext_ref[0:SUBLANES, :] = ext_ref[L:L + SUBLANES, :]
    xa = _silu(conv).reshape(L, SSM_CONV_DIM)
    xs = xa[:, :SSM_D_INNER]
    bm = xa[:, SSM_D_INNER:SSM_D_INNER + G * N]
    cm = xa[:, SSM_D_INNER + G * N:]

    lane = _lane_iota((L, LANES))
    lane1 = _lane_iota((1, LANES))
    dt = jax.nn.softplus(dt_in)
    a_l = jnp.where(lane1 < SSM_HEADS, -jnp.exp(alog), 0.0)
    da = dt * a_l
    row = lax.broadcasted_iota(jnp.int32, (L, L), 0)
    col = lax.broadcasted_iota(jnp.int32, (L, L), 1)
    causal = col <= row
    tri = jnp.where(causal, 1.0, 0.0).astype(bf16)
    hi, mid, lo = _split3(da)
    cs = _dot(tri, hi) + _dot(tri, mid) + _dot(tri, lo)
    cs_t = cs.T
    cs_last = cs[L - 1:L, :]
    w_end = dt * jnp.exp(cs_last - cs)
    e_cs = jnp.exp(cs)
    cdec = jnp.exp(cs_last)

    for g in range(G):
        bm_g = bm[:, g * N:(g + 1) * N]
        cm_g = cm[:, g * N:(g + 1) * N].astype(bf16)
        cb = _dot_nt(cm_g, bm_g.astype(bf16))
        bm_t = bm_g.T.astype(bf16)
        for jj in range(2):
            j = 2 * g + jj
            sl = slice(j * LANES, (j + 1) * LANES)
            psl = slice(jj * LANES, (jj + 1) * LANES)
            xs_p = xs[:, sl]
            xdt = (xs_p * _pair_bcast(dt, j, lane)).astype(bf16)
            yd = []
            for hh in (2 * j, 2 * j + 1):
                seg = cs[:, hh:hh + 1] - cs_t[hh:hh + 1, :]
                decay = jnp.exp(jnp.where(causal, seg, -jnp.inf))
                yd.append(_dot((cb * decay).astype(bf16), xdt))
            y_diag = jnp.where(lane < SSM_HEAD_DIM, yd[0], yd[1])
            prev = st_ref[g, :, psl]
            y_off = _dot(cm_g, prev.astype(bf16)) * _pair_bcast(e_cs, j, lane)
            y_ref[:, sl] = y_diag + y_off + dvec[:, sl] * xs_p
            xw = (xs_p * _pair_bcast(w_end, j, lane)).astype(bf16)
            cd = jnp.where(lane1 < SSM_HEAD_DIM, cdec[:, 2 * j:2 * j + 1], cdec[:, 2 * j + 1:2 * j + 2])
            st_ref[g, :, psl] = prev * cd + _dot(bm_t, xw)

    gw = SSM_D_INNER // G
    for g in range(G):
        sl = slice(g * gw, (g + 1) * gw)
        yz = y_ref[:, sl] * zs[:, sl]
        yz = yz * lax.rsqrt(jnp.mean(yz * yz, axis=-1, keepdims=True) + EPS) * nw[:, sl]
        o_ref[:, sl] = yz.astype(bf16)


def _ssd(rest3, conv_w, conv_b, dt_bias, a_log, dvec, norm_w, l):
    bsz, s, _ = rest3.shape
    L = SSM_CHUNK
    nc = s // L
    misc_blk = (R_LAT + MLA_Q_RANK + MLA_KV_RANK) // LANES
    vec = lambda n: pl.BlockSpec((None, 1, n), lambda b, c: (l, 0, 0))
    return pl.pallas_call(
        _ssd_kernel,
        grid=(bsz, nc),
        in_specs=[
            pl.BlockSpec((None, L, SSM_CONV_DIM), lambda b, c: (b, c, R_XBC // SSM_CONV_DIM)),
            pl.BlockSpec((None, L, SSM_D_INNER), lambda b, c: (b, c, R_Z // SSM_D_INNER)),
            pl.BlockSpec((None, L, LANES), lambda b, c: (b, c, misc_blk)),
            pl.BlockSpec((None, SSM_CONV, SUBLANES, SSM_CONV_DIM), lambda b, c: (l, 0, 0, 0)),
            pl.BlockSpec((None, SUBLANES, SSM_CONV_DIM), lambda b, c: (l, 0, 0)),
            vec(LANES), vec(LANES), vec(SSM_D_INNER), vec(SSM_D_INNER),
        ],
        out_specs=pl.BlockSpec((None, L, SSM_D_INNER), lambda b, c: (b, c, 0)),
        out_shape=jax.ShapeDtypeStruct((bsz, s, SSM_D_INNER), bf16),
        scratch_shapes=[
            pltpu.VMEM((L + SUBLANES, SSM_CONV_DIM), f32),
            pltpu.VMEM((SSM_GROUPS, SSM_STATE, 4 * SSM_HEAD_DIM), f32),
            pltpu.VMEM((L, SSM_D_INNER), f32),
        ],
        compiler_params=_cparams(("arbitrary", "arbitrary")),
        name="ssd",
    )(rest3, rest3, rest3, conv_w, conv_b, dt_bias, a_log, dvec, norm_w)


def _merge_kernel(ya_ref, yb_ref, ys_ref, ga_ref, gb_ref, gc_ref, x_ref, mod_ref,
                  wa_ref, wb_ref, wc_ref, wo_ref, o_ref):
    merged = (ga_ref[...] * _dot(ya_ref[...], wa_ref[...])
              + gb_ref[...] * _dot(yb_ref[...], wb_ref[...])
              + gc_ref[...] * _dot(ys_ref[...], wc_ref[...]))
    o_ref[...] = x_ref[...] + mod_ref[2:3, :] * _dot(merged.astype(bf16), wo_ref[...])


def _merge(ya, yb, ys, gqkv, x2, mod, wa, wb, wc, wo, l, rows_per_batch):
    t, d = x2.shape
    tm = min(1024, rows_per_batch)
    tpb = rows_per_batch // tm
    row = lambda n: pl.BlockSpec((tm, n), lambda i: (i, 0))
    wspec = lambda k: pl.BlockSpec((None, k, d), lambda i: (l, 0, 0))
    return pl.pallas_call(
        _merge_kernel,
        grid=(t // tm,),
        in_specs=[
            row(ya.shape[1]), row(yb.shape[1]), row(ys.shape[1]),
            pl.BlockSpec((tm, d), lambda i: (i, 0)),
            pl.BlockSpec((tm, d), lambda i: (i, 1)),
            pl.BlockSpec((tm, d), lambda i: (i, 2)),
            row(d),
            pl.BlockSpec((None, None, 6, d), lambda i: (l, i // tpb, 0, 0)),
            wspec(ya.shape[1]), wspec(yb.shape[1]), wspec(ys.shape[1]), wspec(d),
        ],
        out_specs=row(d),
        out_shape=jax.ShapeDtypeStruct((t, d), f32),
        compiler_params=_cparams(("arbitrary",)),
        name="merge_out",
    )(ya, yb, ys, gqkv, gqkv, gqkv, x2, mod, wa, wb, wc, wo)


FFN_TF = 256
FFN_NF = FFN_DIM // FFN_TF
FFN_BLOCK_WEIGHTS = (1, 1, 1, 1)


def _ffn_kernel(x_ref, nw_ref, mod_ref, wa_ref, wv_ref, cwa_ref, cwv_ref, cba_ref, cbv_ref, wd_ref,
                o_ref, h_ref, ua_ref, uv_ref, acc_ref):
    fi = pl.program_id(1)
    tm = x_ref.shape[0]

    @pl.when(fi == 0)
    def _():
        x = x_ref[...]
        y = x * lax.rsqrt(jnp.mean(x * x, axis=-1, keepdims=True) + EPS) * nw_ref[...]
        h_ref[...] = (y * (1.0 + mod_ref[4:5, :]) + mod_ref[3:4, :]).astype(bf16)
        acc_ref[...] = jnp.zeros(acc_ref.shape, f32)
        ua_ref[0:SUBLANES, :] = jnp.zeros((SUBLANES, FFN_TF), f32)
        uv_ref[0:SUBLANES, :] = jnp.zeros((SUBLANES, FFN_TF), f32)

    wa, wv, wd = wa_ref[...], wv_ref[...], wd_ref[...]
    cwa, cwv, cba, cbv = cwa_ref[...], cwv_ref[...], cba_ref[...], cbv_ref[...]

    sizes =[tm * w // sum(FFN_BLOCK_WEIGHTS) for w in FFN_BLOCK_WEIGHTS]
    assert sum(sizes) == tm and all(n % 16 == 0 for n in sizes)
    starts = [sum(sizes[:i]) for i in range(len(sizes))]

    def conv(u_ref, cw, cb, base, n):
        out = cb + cw[FFN_CONV - 1:FFN_CONV, :] * u_ref[base:base + n, :]
        for s in range(1, FFN_CONV):
            out = out + cw[FFN_CONV - 1 - s:FFN_CONV - s, :] * u_ref[base - s:base - s + n, :]
        return out

    def up(rb):
        r0, n = starts[rb], sizes[rb]
        hb = h_ref[r0:r0 + n, :]
        ua_ref[SUBLANES + r0:SUBLANES + r0 + n, :] = _dot(hb, wa)
        uv_ref[SUBLANES + r0:SUBLANES + r0 + n, :] = _dot(hb, wv)

    up(0)
    for rb in range(len(sizes)):
        if rb + 1 < len(sizes):
            up(rb + 1)
        r0, n = starts[rb], sizes[rb]
        base = SUBLANES + r0
        act = (_silu(conv(ua_ref, cwa, cba, base, n)) * conv(uv_ref, cwv, cbv, base, n)).astype(bf16)
        acc_ref[r0:r0 + n, :] += _dot(act, wd)

    @pl.when(fi == pl.num_programs(1) - 1)
    def _():
        o_ref[...] = x_ref[...] + mod_ref[5:6, :] * acc_ref[...]


def _ffn(x3, norm_w, mod, wup, conv_w, conv_b, wdown, l):
    bsz, s, d = x3.shape
    tf, nf = FFN_TF, FFN_NF
    return pl.pallas_call(
        _ffn_kernel,
        grid=(bsz, nf),
        in_specs=[
            pl.BlockSpec((None, s, d), lambda b, f: (b, 0, 0)),
            pl.BlockSpec((None, 1, d), lambda b, f: (l, 0, 0)),
            pl.BlockSpec((None, None, 6, d), lambda b, f: (l, b, 0, 0)),
            pl.BlockSpec((None, d, tf), lambda b, f: (l, 0, f)),
            pl.BlockSpec((None, d, tf), lambda b, f: (l, 0, nf + f)),
            pl.BlockSpec((None, FFN_CONV, tf), lambda b, f: (l, 0, f)),
            pl.BlockSpec((None, FFN_CONV, tf), lambda b, f: (l, 0, nf + f)),
            pl.BlockSpec((None, 1, tf), lambda b, f: (l, 0, f)),
            pl.BlockSpec((None, 1, tf), lambda b, f: (l, 0, nf + f)),
            pl.BlockSpec((None, tf, d), lambda b, f: (l, f, 0)),
        ],
        out_specs=pl.BlockSpec((None, s, d), lambda b, f: (b, 0, 0)),
        out_shape=jax.ShapeDtypeStruct((bsz, s, d), f32),
        scratch_shapes=[
            pltpu.VMEM((s, d), bf16),
            pltpu.VMEM((s + SUBLANES, tf), f32), pltpu.VMEM((s + SUBLANES, tf), f32),
            pltpu.VMEM((s, d), f32),
        ],
        compiler_params=_cparams(("arbitrary", "arbitrary")),
        name="ffn",
    )(x3, norm_w, mod, wup, wup, conv_w, conv_w, conv_b, conv_b, wdown)


def _split_w_in(w_in):
    nl, d, _ = w_in.shape
    sizes = (512, 512, 512, MLA_Q_RANK, MLA_KV_RANK, MLA_ROPE, SSM_D_INNER, SSM_CONV_DIM, SSM_HEADS, 3 * D_MODEL)
    offs = np.concatenate([[0], np.cumsum(sizes)])
    qa, ka, va, cq, ckv, kr, z, xbc, dt, gates = [w_in[..., int(offs[i]):int(offs[i + 1])] for i in range(10)]
    zeros = lambda n: jnp.zeros((nl, d, n), w_in.dtype)
    misc = jnp.concatenate([dt, zeros(ROPE_X1 - SSM_HEADS), kr[..., :HALF_ROPE], zeros(ROPE_X2 - ROPE_X1 - HALF_ROPE),
                            kr[..., HALF_ROPE:], zeros(LANES - ROPE_X2 - HALF_ROPE)], axis=-1)
    qkv = w_in[..., :int(offs[3])]
    lat = jnp.concatenate([cq, ckv, misc], axis=-1)
    return tuple(w.astype(bf16) for w in (gates, qkv, xbc, z, lat))


def _pad_lanes(v, n):
    return jnp.pad(v, [(0, 0)] * (v.ndim - 1) + [(0, n - v.shape[-1])])


def _to_head_slot(v):
    return jnp.zeros(v.shape[:-1] + (LANES,), v.dtype).at[..., LANE_OF_DIM[:v.shape[-1]]].set(v)


def kernel(x, c, positions, rel_bias, ada_w, ada_b, norm_mix, norm_ffn, w_in, da_q_norm, da_k_norm, da_lambda, da_subln, mla_q_a_norm, mla_kv_a_norm, mla_w_uq, mla_w_ukv, mla_q_norm, mla_k_norm, ssm_conv_w, ssm_conv_b, ssm_dt_bias, ssm_a_log, ssm_d, ssm_norm, w_branch_a, w_branch_b, w_branch_c, w_out, ffn_w_up, ffn_conv_w, ffn_conv_b, ffn_w_down):
    bsz, seq, d = x.shape
    nl = w_in.shape[0]
    t = bsz * seq
    tq = min(ATT_TQ, seq)

    w_parts = _split_w_in(w_in)
    qk_w = jnp.stack([jnp.tile(da_q_norm, (1, 2 * DA_HEADS)) * (DA_HEAD_DIM ** -0.5 * LOG2E),
                      jnp.tile(da_k_norm, (1, 2 * DA_HEADS))], axis=1).reshape(nl, 2, 1, IN_TN)
    blk = np.arange(IN_TN) // DA_HEAD_DIM
    bd = jnp.asarray(blk[:, None] == blk[None, :], bf16)
    wuq = _to_head_slot(mla_w_uq.reshape(nl, MLA_Q_RANK, MLA_HEADS, MLA_QK))
    wuq = wuq.reshape(nl, MLA_Q_RANK, MLA_HEADS * LANES).astype(bf16)
    wukv = mla_w_ukv.reshape(nl, MLA_KV_RANK, MLA_HEADS, MLA_NOPE + MLA_V)
    wuk = _to_head_slot(wukv[..., :MLA_NOPE]).reshape(nl, MLA_KV_RANK, MLA_HEADS * LANES).astype(bf16)
    wuv = wukv[..., MLA_NOPE:].reshape(nl, MLA_KV_RANK, MLA_HEADS * MLA_V).astype(bf16)
    qn = (_to_head_slot(mla_q_norm) * (MLA_QK ** -0.5 * LOG2E)).reshape(nl, 1, LANES)
    kn = _to_head_slot(mla_k_norm).reshape(nl, 1, LANES)
    dt_bias = _pad_lanes(ssm_dt_bias, LANES).reshape(nl, 1, LANES)
    a_log = _pad_lanes(ssm_a_log, LANES).reshape(nl, 1, LANES)
    dvec = jnp.repeat(ssm_d, SSM_HEAD_DIM, axis=-1).reshape(nl, 1, SSM_D_INNER)
    conv_w8 = jnp.broadcast_to(ssm_conv_w[:, :, None, :], (nl, SSM_CONV, SUBLANES, SSM_CONV_DIM))
    conv_b8 = jnp.broadcast_to(ssm_conv_b[:, None, :], (nl, SUBLANES, SSM_CONV_DIM))
    v3 = lambda a: a.reshape(nl, 1, a.shape[-1])
    wa, wb, wc, wo = (w.astype(bf16) for w in (w_branch_a, w_branch_b, w_branch_c, w_out))
    wup, wdown = ffn_w_up.astype(bf16), ffn_w_down.astype(bf16)

    mod = _ada_mod(c, ada_w, ada_b).reshape(nl, bsz, 6, d)
    ctab, stab = _rope_tables(positions)
    bias = _bias_tiles(rel_bias, tq)

    x2 = x.reshape(t, d)
    for l in range(nl):
        lam_init = 0.8 - 0.6 * math.exp(-0.3 * l)
        gqkv, rest = _inproj(x2, v3(norm_mix), mod, w_parts, qk_w, bd, l, seq)
        q_b, k_b, v_b = _mla_prep(rest, v3(mla_q_a_norm), v3(mla_kv_a_norm), wuq, wuk, wuv, qn, kn, ctab, stab, l)
        ya = _diff_attn(gqkv.reshape(bsz, seq, GQKV_COLS), bias, da_lambda, v3(da_subln), l, lam_init)
        yb = _mla_attn(q_b.reshape(bsz, seq, -1), k_b.reshape(bsz, seq, -1), v_b.reshape(bsz, seq, -1),
                       min(MLA_TQ, seq))
        ys = _ssd(rest.reshape(bsz, seq, REST_COLS), conv_w8, conv_b8, dt_bias, a_log, dvec, v3(ssm_norm), l)
        x2 = _merge(ya.reshape(t, -1), yb.reshape(t, -1), ys.reshape(t, -1), gqkv, x2, mod, wa, wb, wc, wo, l, seq)
        x2 = _ffn(x2.reshape(bsz, seq, d), v3(norm_ffn), mod, wup, ffn_conv_w, v3(ffn_conv_b), wdown, l).reshape(t, d)
    return x2.reshape(bsz, seq, d)
```

```python
import functools
import math

import jax
import jax.numpy as jnp
import numpy as np
from jax import lax
from jax.experimental import pallas as pl
from jax.experimental.pallas import tpu as pltpu

f32 = jnp.float32
bf16 = jnp.bfloat16

D_MODEL = 1024
DA_HEADS = 4
DA_HEAD_DIM = 64
MLA_HEADS = 8
MLA_Q_RANK = 256
MLA_KV_RANK = 128
MLA_NOPE = 64
MLA_ROPE = 32
MLA_V = 64
MLA_QK = MLA_NOPE + MLA_ROPE
SSM_D_INNER = 1024
SSM_HEAD_DIM = 64
SSM_HEADS = 16
SSM_GROUPS = 4
SSM_STATE = 128
SSM_CONV = 4
SSM_CHUNK = 128
SSM_CONV_DIM = SSM_D_INNER + 2 * SSM_GROUPS * SSM_STATE
FFN_DIM = 2816
FFN_CONV = 3
REL_BUCKETS = 32
REL_MAX_DIST = 128
ROPE_THETA = 10000.0
EPS = 1e-6

LANES = 128
SUBLANES = 8
VMEM_LIMIT = 56 * 1024 * 1024

IN_TN = 512
G_COLS = 3 * D_MODEL
GQKV_COLS = G_COLS + 3 * 512
R_XBC = 0
R_Z = R_XBC + SSM_CONV_DIM
R_LAT = R_Z + SSM_D_INNER
REST_COLS = R_LAT + 512
ROPE_X1 = 32
ROPE_X2 = ROPE_X1 + LANES // 2
HALF_ROPE = MLA_ROPE // 2
LANE_OF_DIM = np.concatenate([np.arange(0, 32), np.arange(48, 80),
                              np.arange(ROPE_X1, ROPE_X1 + HALF_ROPE), np.arange(ROPE_X2, ROPE_X2 + HALF_ROPE)])
NEG_BIG = -1e30
LOG2E = math.log2(math.e)

ATT_TQ = 256
MLA_TQ = 256


def _cparams(sem):
    return pltpu.CompilerParams(dimension_semantics=sem, vmem_limit_bytes=VMEM_LIMIT)


def _lane_iota(shape):
    return lax.broadcasted_iota(jnp.int32, shape, len(shape) - 1)


def _dot(a, b):
    return jnp.dot(a, b, preferred_element_type=f32)


def _dot_nt(a, b):
    return lax.dot_general(a, b, (((1,), (1,)), ((), ())), preferred_element_type=f32)


def _sigmoid(x):
    return 1.0 / (1.0 + jnp.exp(-x))


def _silu(x):
    return x * _sigmoid(x)


def _ada_kernel(c_ref, w_ref, b_ref, o_ref):
    ca = _silu(c_ref[...]).astype(bf16)
    o_ref[...] = _dot(ca, w_ref[...].astype(bf16)) + b_ref[...]


def _ada_mod(c, ada_w, ada_b):
    nl, d, n = ada_w.shape
    bsz = c.shape[0]
    tn = 1024
    return pl.pallas_call(
        _ada_kernel,
        grid=(nl, n // tn),
        in_specs=[
            pl.BlockSpec((bsz, d), lambda l, j: (0, 0)),
            pl.BlockSpec((None, d, tn), lambda l, j: (l, 0, j)),
            pl.BlockSpec((None, 1, tn), lambda l, j: (l, 0, j)),
        ],
        out_specs=pl.BlockSpec((None, bsz, tn), lambda l, j: (l, 0, j)),
        out_shape=jax.ShapeDtypeStruct((nl, bsz, n), f32),
        compiler_params=_cparams(("arbitrary", "arbitrary")),
        name="ada_mod",
    )(c, ada_w, ada_b.reshape(nl, 1, n))


def _rope_kernel(pos_ref, freq_ref, c_ref, s_ref):
    ang = pos_ref[...].astype(f32) * freq_ref[...]
    cos, sin = jnp.cos(ang), jnp.sin(ang)
    lane = _lane_iota(ang.shape)
    first = (lane >= ROPE_X1) & (lane < ROPE_X1 + HALF_ROPE)
    second = (lane >= ROPE_X2) & (lane < ROPE_X2 + HALF_ROPE)
    c_ref[...] = jnp.where(first | second, cos, 1.0)
    s_ref[...] = jnp.where(first, -sin, jnp.where(second, sin, 0.0))


def _rope_tables(positions):
    t = positions.size
    tm = min(t, 2048)
    inv_freq = ROPE_THETA ** (-jnp.arange(0, MLA_ROPE, 2, dtype=f32) / MLA_ROPE)
    freq = jnp.zeros((LANES,), f32).at[LANE_OF_DIM[MLA_NOPE:]].set(jnp.tile(inv_freq, 2))
    out = jax.ShapeDtypeStruct((t, LANES), f32)
    spec = pl.BlockSpec((tm, LANES), lambda i: (i, 0))
    return pl.pallas_call(
        _rope_kernel,
        grid=(t // tm,),
        in_specs=[spec, pl.BlockSpec((1, LANES), lambda i: (0, 0))],
        out_specs=[spec, spec],
        out_shape=[out, out],
        compiler_params=_cparams(("arbitrary",)),
        name="rope_tables",
    )(jnp.broadcast_to(positions.reshape(t, 1), (t, LANES)), freq.reshape(1, LANES))


def _bias_kernel(rb_ref, o_ref):
    h, which = pl.program_id(0), pl.program_id(1)
    tq = o_ref.shape[-1]
    row = lax.broadcasted_iota(jnp.int32, (tq, tq), 0)
    col = lax.broadcasted_iota(jnp.int32, (tq, tq), 1)
    rel = col - row + which * tq
    d = jnp.maximum(rel, 0)
    max_exact = REL_BUCKETS // 2
    large = max_exact + (jnp.log(jnp.maximum(d, 1).astype(f32) / max_exact)
                         / math.log(REL_MAX_DIST / max_exact) * (REL_BUCKETS - max_exact)).astype(jnp.int32)
    large = jnp.minimum(large, REL_BUCKETS - 1)
    bucket = jnp.where(d < max_exact, d, large)
    val = jnp.full((tq, tq), rb_ref[0, h], f32)
    for k in range(1, REL_BUCKETS):
        val = jnp.where(bucket == k, rb_ref[k, h], val)
    val = (val - rb_ref[REL_BUCKETS - 1, h]) * LOG2E
    val = jnp.where(rel < 0, NEG_BIG, val)
    o_ref[...] = jnp.where(which >= 2, 0.0, val)


def _bias_tiles(rel_bias, tq):
    return pl.pallas_call(
        _bias_kernel,
        grid=(DA_HEADS, 3),
        in_specs=[pl.BlockSpec(memory_space=pltpu.SMEM)],
        out_specs=pl.BlockSpec((None, None, tq, tq), lambda h, w: (h, w, 0, 0)),
        out_shape=jax.ShapeDtypeStruct((DA_HEADS, 3, tq, tq), f32),
        compiler_params=_cparams(("arbitrary", "arbitrary")),
        name="bias_tiles",
    )(rel_bias)


def _inproj_kernel(x_ref, nw_ref, mod_ref, wg_ref, wqkv_ref, wxbc_ref, wz_ref, wlat_ref, qkw_ref, bd_ref,
                   gqkv_ref, rest_ref, h_ref):
    j = pl.program_id(1)
    jq = G_COLS // IN_TN
    jx = GQKV_COLS // IN_TN
    jz = jx + SSM_CONV_DIM // IN_TN
    jl = jz + SSM_D_INNER // IN_TN

    @pl.when(j == 0)
    def _():
        x = x_ref[...]
        y = x * lax.rsqrt(jnp.mean(x * x, axis=-1, keepdims=True) + EPS) * nw_ref[...]
        h_ref[...] = (y * (1.0 + mod_ref[1:2, :]) + mod_ref[0:1, :]).astype(bf16)

    def proj(w_ref):
        return _dot(h_ref[...], w_ref[...])

    @pl.when(j < jq)
    def _():
        gqkv_ref[...] = (0.5 * jnp.tanh(0.5 * proj(wg_ref)) + 0.5).astype(bf16)

    @pl.when((j >= jq) & (j < jq + 2))
    def _():
        acc = proj(wqkv_ref)
        ss = _dot((acc * acc).astype(bf16), bd_ref[...]) * (1.0 / DA_HEAD_DIM)
        gqkv_ref[...] = (acc * lax.rsqrt(ss + EPS) * qkw_ref[...]).astype(bf16)

    @pl.when(j == jq + 2)
    def _():
        gqkv_ref[...] = proj(wqkv_ref).astype(bf16)

    @pl.when((j >= jx) & (j < jz))
    def _():
        rest_ref[...] = proj(wxbc_ref)

    @pl.when((j >= jz) & (j < jl))
    def _():
        rest_ref[...] = proj(wz_ref)

    @pl.when(j >= jl)
    def _():
        rest_ref[...] = proj(wlat_ref)


def _inproj(x2, norm_w, mod, w_parts, qk_w, bd, l, rows_per_batch):
    t, d = x2.shape
    tm = rows_per_batch
    nj = (GQKV_COLS + REST_COLS) // IN_TN
    nb = GQKV_COLS // IN_TN
    jq = G_COLS // IN_TN
    jz = nb + SSM_CONV_DIM // IN_TN

    def wspec(w, first):
        last = w.shape[-1] // IN_TN - 1
        return pl.BlockSpec((None, d, IN_TN), lambda i, j: (l, 0, jnp.clip(j - first, 0, last)))
    return pl.pallas_call(
        _inproj_kernel,
        grid=(t // tm, nj),
        in_specs=[
            pl.BlockSpec((tm, d), lambda i, j: (i, 0)),
            pl.BlockSpec((None, 1, d), lambda i, j: (l, 0, 0)),
            pl.BlockSpec((None, None, 6, d), lambda i, j: (l, i, 0, 0)),
            wspec(w_parts[0], 0), wspec(w_parts[1], jq), wspec(w_parts[2], nb), wspec(w_parts[3], jz),
            wspec(w_parts[4], nj - 1),
            pl.BlockSpec((None, None, 1, IN_TN), lambda i, j: (l, jnp.clip(j - jq, 0, 1), 0, 0)),
            pl.BlockSpec((IN_TN, IN_TN), lambda i, j: (0, 0)),
        ],
        out_specs=[
            pl.BlockSpec((tm, IN_TN), lambda i, j: (i, jnp.minimum(j, nb - 1))),
            pl.BlockSpec((tm, IN_TN), lambda i, j: (i, jnp.maximum(j - nb, 0))),
        ],
        out_shape=[jax.ShapeDtypeStruct((t, GQKV_COLS), bf16), jax.ShapeDtypeStruct((t, REST_COLS), f32)],
        scratch_shapes=[pltpu.VMEM((tm, d), bf16)],
        compiler_params=_cparams(("arbitrary", "arbitrary")),
        name="inproj",
    )(x2, norm_w, mod, *w_parts, qk_w, bd)


def _rope_apply(x, c, s):
    return x * c + pltpu.roll(x, LANES // 2, 1) * s


def _mla_prep_kernel(lat_ref, qan_ref, kvan_ref, wuq_ref, wuk_ref, wuv_ref, qn_ref, kn_ref,
                     c_ref, s_ref, q_ref, k_ref, v_ref):
    lat = lat_ref[...]
    c, s = c_ref[...], s_ref[...]
    cq = lat[:, :MLA_Q_RANK]
    cqn = (cq * lax.rsqrt(jnp.mean(cq * cq, axis=-1, keepdims=True) + EPS) * qan_ref[...]).astype(bf16)
    ckv = lat[:, MLA_Q_RANK:MLA_Q_RANK + MLA_KV_RANK]
    ckvn = (ckv * lax.rsqrt(jnp.mean(ckv * ckv, axis=-1, keepdims=True) + EPS) * kvan_ref[...]).astype(bf16)
    misc = lat[:, MLA_Q_RANK + MLA_KV_RANK:]
    lane = _lane_iota(misc.shape)
    in_rope = (((lane >= ROPE_X1) & (lane < ROPE_X1 + HALF_ROPE))
               | ((lane >= ROPE_X2) & (lane < ROPE_X2 + HALF_ROPE)))
    krb = jnp.where(in_rope, misc, 0.0)
    qf = _dot(cqn, wuq_ref[...])
    kf = _dot(ckvn, wuk_ref[...])
    v_ref[...] = _dot(ckvn, wuv_ref[...]).astype(bf16)
    inv = 1.0 / MLA_QK
    qn, kn = qn_ref[...], kn_ref[...]
    for h in range(MLA_HEADS):
        sl = slice(h * LANES, (h + 1) * LANES)
        qh = qf[:, sl]
        qh = qh * lax.rsqrt(jnp.sum(qh * qh, axis=-1, keepdims=True) * inv + EPS) * qn
        q_ref[:, sl] = _rope_apply(qh, c, s).astype(bf16)
        kh = kf[:, sl] + krb
        kh = kh * lax.rsqrt(jnp.sum(kh * kh, axis=-1, keepdims=True) * inv + EPS) * kn
        k_ref[:, sl] = _rope_apply(kh, c, s).astype(bf16)


def _mla_prep(rest, qan, kvan, wuq, wuk, wuv, qn, kn, ctab, stab, l):
    t = rest.shape[0]
    tm = min(2048, t)
    lat_blk = R_LAT // 512
    hq = MLA_HEADS * LANES
    vec = lambda n: pl.BlockSpec((None, 1, n), lambda i: (l, 0, 0))
    tab = pl.BlockSpec((tm, LANES), lambda i: (i, 0))
    return pl.pallas_call(
        _mla_prep_kernel,
        grid=(t // tm,),
        in_specs=[
            pl.BlockSpec((tm, 512), lambda i: (i, lat_blk)),
            vec(MLA_Q_RANK), vec(MLA_KV_RANK),
            pl.BlockSpec((None, MLA_Q_RANK, hq), lambda i: (l, 0, 0)),
            pl.BlockSpec((None, MLA_KV_RANK, hq), lambda i: (l, 0, 0)),
            pl.BlockSpec((None, MLA_KV_RANK, MLA_HEADS * MLA_V), lambda i: (l, 0, 0)),
            vec(LANES), vec(LANES), tab, tab,
        ],
        out_specs=[
            pl.BlockSpec((tm, hq), lambda i: (i, 0)),
            pl.BlockSpec((tm, hq), lambda i: (i, 0)),
            pl.BlockSpec((tm, MLA_HEADS * MLA_V), lambda i: (i, 0)),
        ],
        out_shape=[jax.ShapeDtypeStruct((t, hq), bf16), jax.ShapeDtypeStruct((t, hq), bf16),
                   jax.ShapeDtypeStruct((t, MLA_HEADS * MLA_V), bf16)],
        compiler_params=_cparams(("arbitrary",)),
        name="mla_prep",
    )(rest, qan, kvan, wuq, wuk, wuv, qn, kn, ctab, stab)


def _flash_t(q_list, k_ref, kcols, vt_ref, vrows, tile_fn, ntiles, qi, tq,
             m_ref, l_ref, acc_ref, s_refs, e_refs, al_refs):
    nmap = len(q_list)
    m_ref[...] = jnp.full(m_ref.shape, -jnp.inf, f32)
    l_ref[...] = jnp.zeros(l_ref.shape, f32)
    acc_ref[...] = jnp.zeros(acc_ref.shape, f32)
    last = qi

    def qk(t, slot):
        start = pl.multiple_of(t * tq, tq)
        w = jnp.minimum(qi - t, ntiles - 1)
        vals = []
        for a in range(nmap):
            k = k_ref[pl.ds(start, tq), kcols[a]:kcols[a] + LANES]
            vals.append(_dot_nt(k, q_list[a]) + tile_fn(a, w))
        for a in range(nmap):
            s_refs[slot][a] = vals[a]

    def sm(slot):
        for a in range(nmap):
            s = s_refs[slot][a]
            m_prev = m_ref[a]
            m_new = jnp.maximum(m_prev, jnp.max(s, axis=0, keepdims=True))
            alpha = jnp.exp2(m_prev - m_new)
            e = jnp.exp2(s - m_new)
            m_ref[a] = m_new
            e_refs[slot][a] = e.astype(bf16)
            al_refs[slot][a] = alpha

    def pv(t, slot):
        for a in range(nmap):
            vt = vt_ref[t, vrows[a]:vrows[a] + VT_BLOCK, :]
            res = _dot(vt, e_refs[slot][a])
            alpha = al_refs[slot][a]
            acc_ref[a] = alpha * acc_ref[a] + res[:LANES]
            l_ref[a] = alpha * l_ref[a] + res[LANES:LANES + 1]

    def body(t, cur):
        qk(jnp.minimum(t + 1, last), 1 - cur)
        pv(t - 1, 1 - cur)
        sm(cur)

    qk(0, 0)
    qk(jnp.minimum(1, last), 1)
    sm(0)

    def pair(i, carry):
        t = 2 * i + 1
        body(t, 1)
        body(t + 1, 0)
        return carry

    lax.fori_loop(0, last // 2, pair, 0)
    odd = lax.rem(last, 2) == 1

    @pl.when(odd)
    def _():
        body(last, 1)
        pv(last, 1)

    @pl.when(jnp.logical_not(odd))
    def _():
        pv(last, 0)


VT_BLOCK = LANES + 16


def _fill_vt(v_ref, vt_ref, tq):
    ones = jnp.ones((VT_BLOCK - LANES, tq), bf16)
    for c in range(vt_ref.shape[0]):
        vt = v_ref[c * tq:(c + 1) * tq, :].astype(f32).T.astype(bf16)
        for b in range(vt.shape[0] // LANES):
            vt_ref[c, b * VT_BLOCK:b * VT_BLOCK + LANES, :] = vt[b * LANES:(b + 1) * LANES, :]
            vt_ref[c, b * VT_BLOCK + LANES:(b + 1) * VT_BLOCK, :] = ones


def _attn_scratch(nmap, nq, hv, tq):
    sbuf = pltpu.VMEM((nmap, tq, tq), f32)
    ebuf = pltpu.VMEM((nmap, tq, tq), bf16)
    row = pltpu.VMEM((nmap, 1, tq), f32)
    return [pltpu.VMEM((nq, hv // LANES * VT_BLOCK, tq), bf16), row, row, pltpu.VMEM((nmap, LANES, tq), f32),
            sbuf, sbuf, ebuf, ebuf, row, row]


def _diff_attn_kernel(q_ref, k_ref, v_ref, bias_ref, lp_ref, sub_ref, o_ref,
                      vt_ref, m_ref, l_ref, acc_ref, s0_ref, s1_ref, e0_ref, e1_ref, al0_ref, al1_ref,
                      *, lam_init, tq):
    qi = pl.program_id(1)

    @pl.when(qi == 0)
    def _():
        _fill_vt(v_ref, vt_ref, tq)

    lane = _lane_iota((tq, LANES))
    q_list, kcols, vrows = [], [], []
    for h in range(DA_HEADS):
        q = q_ref[:, h * LANES:(h + 1) * LANES]
        zero = jnp.zeros_like(q)
        q_list += [jnp.where(lane < DA_HEAD_DIM, q, zero), jnp.where(lane >= DA_HEAD_DIM, q, zero)]
        kcols += [h * LANES, h * LANES]
        vrows += [h * VT_BLOCK, h * VT_BLOCK]
    lp, sub = lp_ref[...], sub_ref[...]
    _flash_t(q_list, k_ref, kcols, vt_ref, vrows, lambda a, w: bias_ref[a // 2, w], 3, qi, tq,
             m_ref, l_ref, acc_ref, (s0_ref, s1_ref), (e0_ref, e1_ref), (al0_ref, al1_ref))
    lam = (jnp.exp(jnp.sum(lp[0:1] * lp[1:2], axis=-1, keepdims=True))
           - jnp.exp(jnp.sum(lp[2:3] * lp[3:4], axis=-1, keepdims=True)) + lam_init)
    for h in range(DA_HEADS):
        o_t = acc_ref[2 * h] / l_ref[2 * h] - lam * (acc_ref[2 * h + 1] / l_ref[2 * h + 1])
        o = o_t.T
        o = o * lax.rsqrt(jnp.mean(o * o, axis=-1, keepdims=True) + EPS) * sub
        o_ref[:, h * LANES:(h + 1) * LANES] = (o * (1.0 - lam_init)).astype(bf16)


def _diff_attn(qkv3, bias, da_lambda, da_subln, l, lam_init):
    bsz, s, _ = qkv3.shape
    tq = bias.shape[-1]
    nq = s // tq
    hw = DA_HEADS * LANES
    qblk = G_COLS // hw
    kern = functools.partial(_diff_attn_kernel, lam_init=lam_init, tq=tq)
    return pl.pallas_call(
        kern,
        grid=(bsz, nq),
        in_specs=[
            pl.BlockSpec((None, tq, hw), lambda b, i: (b, i, qblk)),
            pl.BlockSpec((None, s, hw), lambda b, i: (b, 0, qblk + 1)),
            pl.BlockSpec((None, s, hw), lambda b, i: (b, 0, qblk + 2)),
            pl.BlockSpec((DA_HEADS, 3, tq, tq), lambda b, i: (0, 0, 0, 0)),
            pl.BlockSpec((None, 4, DA_HEAD_DIM), lambda b, i: (l, 0, 0)),
            pl.BlockSpec((None, 1, LANES), lambda b, i: (l, 0, 0)),
        ],
        out_specs=pl.BlockSpec((None, tq, hw), lambda b, i: (b, i, 0)),
        out_shape=jax.ShapeDtypeStruct((bsz, s, hw), bf16),
        scratch_shapes=_attn_scratch(2 * DA_HEADS, nq, hw, tq),
        compiler_params=_cparams(("arbitrary", "arbitrary")),
        name="diff_attn",
    )(qkv3, qkv3, qkv3, bias, da_lambda, da_subln)


def _mla_attn_kernel(q_ref, k_ref, v_ref, o_ref,
                     vt_ref, m_ref, l_ref, acc_ref, s0_ref, s1_ref, e0_ref, e1_ref, al0_ref, al1_ref, mask_ref,
                     *, tq):
    qi = pl.program_id(1)

    @pl.when(qi == 0)
    def _():
        _fill_vt(v_ref, vt_ref, tq)
        krow = lax.broadcasted_iota(jnp.int32, (tq, tq), 0)
        qcol = lax.broadcasted_iota(jnp.int32, (tq, tq), 1)
        mask_ref[0] = jnp.where(krow <= qcol, 0.0, NEG_BIG)
        mask_ref[1] = jnp.zeros((tq, tq), f32)

    q_list = [q_ref[:, h * LANES:(h + 1) * LANES] for h in range(MLA_HEADS)]
    kcols = [h * LANES for h in range(MLA_HEADS)]
    vrows = [(h // 2) * VT_BLOCK for h in range(MLA_HEADS)]
    _flash_t(q_list, k_ref, kcols, vt_ref, vrows, lambda a, w: mask_ref[w], 2, qi, tq,
             m_ref, l_ref, acc_ref, (s0_ref, s1_ref), (e0_ref, e1_ref), (al0_ref, al1_ref))
    row = lax.broadcasted_iota(jnp.int32, (LANES, tq), 0)
    for p in range(MLA_HEADS // 2):
        o_t = jnp.where(row < MLA_V, acc_ref[2 * p] / l_ref[2 * p], acc_ref[2 * p + 1] / l_ref[2 * p + 1])
        o_ref[:, p * LANES:(p + 1) * LANES] = o_t.T.astype(bf16)


def _mla_attn(q3, k3, v3, tq):
    bsz, s, hq = q3.shape
    nq = s // tq
    hv = v3.shape[-1]
    kern = functools.partial(_mla_attn_kernel, tq=tq)
    return pl.pallas_call(
        kern,
        grid=(bsz, nq),
        in_specs=[
            pl.BlockSpec((None, tq, hq), lambda b, i: (b, i, 0)),
            pl.BlockSpec((None, s, hq), lambda b, i: (b, 0, 0)),
            pl.BlockSpec((None, s, hv), lambda b, i: (b, 0, 0)),
        ],
        out_specs=pl.BlockSpec((None, tq, hv), lambda b, i: (b, i, 0)),
        out_shape=jax.ShapeDtypeStruct((bsz, s, hv), bf16),
        scratch_shapes=_attn_scratch(MLA_HEADS, nq, hv, tq) + [pltpu.VMEM((2, tq, tq), f32)],
        compiler_params=_cparams(("arbitrary", "arbitrary")),
        name="mla_attn",
    )(q3, k3, v3)


def _split3(x):
    hi = x.astype(bf16)
    r1 = x - hi.astype(f32)
    mid = r1.astype(bf16)
    lo = (r1 - mid.astype(f32)).astype(bf16)
    return hi, mid, lo


def _pair_bcast(x, j, lane):
    return jnp.where(lane < SSM_HEAD_DIM, x[:, 2 * j:2 * j + 1], x[:, 2 * j + 1:2 * j + 2])


def _ssd_kernel(xbc_ref, z_ref, misc_ref, cw_ref, cb_ref, dtb_ref, alog_ref, dvec_ref, nw_ref,
                o_ref, ext_ref, st_ref, y_ref):
    c = pl.program_id(1)
    L = SSM_CHUNK
    G, N = SSM_GROUPS, SSM_STATE

    @pl.when(c == 0)
    def _():
        ext_ref[0:SUBLANES, :] = jnp.zeros((SUBLANES, SSM_CONV_DIM), f32)
        st_ref[...] = jnp.zeros(st_ref.shape, f32)

    zs, dvec, nw = _silu(z_ref[...]), dvec_ref[...], nw_ref[...]
    cw, cbias = cw_ref[...], cb_ref[...]
    dt_in = misc_ref[...] + dtb_ref[...]
    alog = alog_ref[...]

    ext_ref[SUBLANES:, :] = xbc_ref[...]
    def rows8(v):
        return v.reshape(L // SUBLANES, SUBLANES, SSM_CONV_DIM)

    conv = cbias + cw[SSM_CONV - 1] * rows8(ext_ref[SUBLANES:, :])
    for s in range(1, SSM_CONV):
        conv = conv + cw[SSM_CONV - 1 - s] * rows8(ext_ref[SUBLANES - s:SUBLANES - s + L, :])
    ext_ref[0:SUBLANES, :] = ext_ref[L:L + SUBLANES, :]
    xa = _silu(conv).reshape(L, SSM_CONV_DIM)
    xs = xa[:, :SSM_D_INNER]
    bm = xa[:, SSM_D_INNER:SSM_D_INNER + G * N]
    cm = xa[:, SSM_D_INNER + G * N:]

    lane = _lane_iota((L, LANES))
    lane1 = _lane_iota((1, LANES))
    dt = jax.nn.softplus(dt_in)
    a_l = jnp.where(lane1 < SSM_HEADS, -jnp.exp(alog), 0.0)
    da = dt * a_l
    row = lax.broadcasted_iota(jnp.int32, (L, L), 0)
    col = lax.broadcasted_iota(jnp.int32, (L, L), 1)
    causal = col <= row
    tri = jnp.where(causal, 1.0, 0.0).astype(bf16)
    hi, mid, lo = _split3(da)
    cs = _dot(tri, hi) + _dot(tri, mid) + _dot(tri, lo)
    cs_t = cs.T
    cs_last = cs[L - 1:L, :]
    w_end = dt * jnp.exp(cs_last - cs)
    e_cs = jnp.exp(cs)
    cdec = jnp.exp(cs_last)

    for g in range(G):
        bm_g = bm[:, g * N:(g + 1) * N]
        cm_g = cm[:, g * N:(g + 1) * N].astype(bf16)
        cb = _dot_nt(cm_g, bm_g.astype(bf16))
        bm_t = bm_g.T.astype(bf16)
        for jj in range(2):
            j = 2 * g + jj
            sl = slice(j * LANES, (j + 1) * LANES)
            psl = slice(jj * LANES, (jj + 1) * LANES)
            xs_p = xs[:, sl]
            xdt = (xs_p * _pair_bcast(dt, j, lane)).astype(bf16)
            yd = []
            for hh in (2 * j, 2 * j + 1):
                seg = cs[:, hh:hh + 1] - cs_t[hh:hh + 1, :]
                decay = jnp.exp(jnp.where(causal, seg, -jnp.inf))
                yd.append(_dot((cb * decay).astype(bf16), xdt))
            y_diag = jnp.where(lane < SSM_HEAD_DIM, yd[0], yd[1])
            prev = st_ref[g, :, psl]
            y_off = _dot(cm_g, prev.astype(bf16)) * _pair_bcast(e_cs, j, lane)
            y_ref[:, sl] = y_diag + y_off + dvec[:, sl] * xs_p
            xw = (xs_p * _pair_bcast(w_end, j, lane)).astype(bf16)
            cd = jnp.where(lane1 < SSM_HEAD_DIM, cdec[:, 2 * j:2 * j + 1], cdec[:, 2 * j + 1:2 * j + 2])
            st_ref[g, :, psl] = prev * cd + _dot(bm_t, xw)

    gw = SSM_D_INNER // G
    for g in range(G):
        sl = slice(g * gw, (g + 1) * gw)
        yz = y_ref[:, sl] * zs[:, sl]
        yz = yz * lax.rsqrt(jnp.mean(yz * yz, axis=-1, keepdims=True) + EPS) * nw[:, sl]
        o_ref[:, sl] = yz.astype(bf16)


def _ssd(rest3, conv_w, conv_b, dt_bias, a_log, dvec, norm_w, l):
    bsz, s, _ = rest3.shape
    L = SSM_CHUNK
    nc = s // L
    misc_blk = (R_LAT + MLA_Q_RANK + MLA_KV_RANK) // LANES
    vec = lambda n: pl.BlockSpec((None, 1, n), lambda b, c: (l, 0, 0))
    return pl.pallas_call(
        _ssd_kernel,
        grid=(bsz, nc),
        in_specs=[
            pl.BlockSpec((None, L, SSM_CONV_DIM), lambda b, c: (b, c, R_XBC // SSM_CONV_DIM)),
            pl.BlockSpec((None, L, SSM_D_INNER), lambda b, c: (b, c, R_Z // SSM_D_INNER)),
            pl.BlockSpec((None, L, LANES), lambda b, c: (b, c, misc_blk)),
            pl.BlockSpec((None, SSM_CONV, SUBLANES, SSM_CONV_DIM), lambda b, c: (l, 0, 0, 0)),
            pl.BlockSpec((None, SUBLANES, SSM_CONV_DIM), lambda b, c: (l, 0, 0)),
            vec(LANES), vec(LANES), vec(SSM_D_INNER), vec(SSM_D_INNER),
        ],
        out_specs=pl.BlockSpec((None, L, SSM_D_INNER), lambda b, c: (b, c, 0)),
        out_shape=jax.ShapeDtypeStruct((bsz, s, SSM_D_INNER), bf16),
        scratch_shapes=[
            pltpu.VMEM((L + SUBLANES, SSM_CONV_DIM), f32),
            pltpu.VMEM((SSM_GROUPS, SSM_STATE, 4 * SSM_HEAD_DIM), f32),
            pltpu.VMEM((L, SSM_D_INNER), f32),
        ],
        compiler_params=_cparams(("arbitrary", "arbitrary")),
        name="ssd",
    )(rest3, rest3, rest3, conv_w, conv_b, dt_bias, a_log, dvec, norm_w)


def _merge_kernel(ya_ref, yb_ref, ys_ref, ga_ref, gb_ref, gc_ref, x_ref, mod_ref,
                  wa_ref, wb_ref, wc_ref, wo_ref, o_ref):
    merged = (ga_ref[...] * _dot(ya_ref[...], wa_ref[...])
              + gb_ref[...] * _dot(yb_ref[...], wb_ref[...])
              + gc_ref[...] * _dot(ys_ref[...], wc_ref[...]))
    o_ref[...] = x_ref[...] + mod_ref[2:3, :] * _dot(merged.astype(bf16), wo_ref[...])


def _merge(ya, yb, ys, gqkv, x2, mod, wa, wb, wc, wo, l, rows_per_batch):
    t, d = x2.shape
    tm = min(1024, rows_per_batch)
    tpb = rows_per_batch // tm
    row = lambda n: pl.BlockSpec((tm, n), lambda i: (i, 0))
    wspec = lambda k: pl.BlockSpec((None, k, d), lambda i: (l, 0, 0))
    return pl.pallas_call(
        _merge_kernel,
        grid=(t // tm,),
        in_specs=[
            row(ya.shape[1]), row(yb.shape[1]), row(ys.shape[1]),
            pl.BlockSpec((tm, d), lambda i: (i, 0)),
            pl.BlockSpec((tm, d), lambda i: (i, 1)),
            pl.BlockSpec((tm, d), lambda i: (i, 2)),
            row(d),
            pl.BlockSpec((None, None, 6, d), lambda i: (l, i // tpb, 0, 0)),
            wspec(ya.shape[1]), wspec(yb.shape[1]), wspec(ys.shape[1]), wspec(d),
        ],
        out_specs=row(d),
        out_shape=jax.ShapeDtypeStruct((t, d), f32),
        compiler_params=_cparams(("arbitrary",)),
        name="merge_out",
    )(ya, yb, ys, gqkv, gqkv, gqkv, x2, mod, wa, wb, wc, wo)


FFN_TF = 256
FFN_NF = FFN_DIM // FFN_TF
FFN_BLOCK_WEIGHTS = (1, 1, 1, 1)


def _ffn_kernel(x_ref, nw_ref, mod_ref, wa_ref, wv_ref, cwa_ref, cwv_ref, cba_ref, cbv_ref, wd_ref,
                o_ref, h_ref, ua_ref, uv_ref, acc_ref):
    fi = pl.program_id(1)
    tm = x_ref.shape[0]

    @pl.when(fi == 0)
    def _():
        x = x_ref[...]
        y = x * lax.rsqrt(jnp.mean(x * x, axis=-1, keepdims=True) + EPS) * nw_ref[...]
        h_ref[...] = (y * (1.0 + mod_ref[4:5, :]) + mod_ref[3:4, :]).astype(bf16)
        acc_ref[...] = jnp.zeros(acc_ref.shape, f32)
        ua_ref[0:SUBLANES, :] = jnp.zeros((SUBLANES, FFN_TF), f32)
        uv_ref[0:SUBLANES, :] = jnp.zeros((SUBLANES, FFN_TF), f32)

    wa, wv, wd = wa_ref[...], wv_ref[...], wd_ref[...]
    cwa, cwv, cba, cbv = cwa_ref[...], cwv_ref[...], cba_ref[...], cbv_ref[...]

    sizes =[tm * w // sum(FFN_BLOCK_WEIGHTS) for w in FFN_BLOCK_WEIGHTS]
    assert sum(sizes) == tm and all(n % 16 == 0 for n in sizes)
    starts = [sum(sizes[:i]) for i in range(len(sizes))]

    def conv(u_ref, cw, cb, base, n):
        out = cb + cw[FFN_CONV - 1:FFN_CONV, :] * u_ref[base:base + n, :]
        for s in range(1, FFN_CONV):
            out = out + cw[FFN_CONV - 1 - s:FFN_CONV - s, :] * u_ref[base - s:base - s + n, :]
        return out

    def up(rb):
        r0, n = starts[rb], sizes[rb]
        hb = h_ref[r0:r0 + n, :]
        ua_ref[SUBLANES + r0:SUBLANES + r0 + n, :] = _dot(hb, wa)
        uv_ref[SUBLANES + r0:SUBLANES + r0 + n, :] = _dot(hb, wv)

    up(0)
    for rb in range(len(sizes)):
        if rb + 1 < len(sizes):
            up(rb + 1)
        r0, n = starts[rb], sizes[rb]
        base = SUBLANES + r0
        act = (_silu(conv(ua_ref, cwa, cba, base, n)) * conv(uv_ref, cwv, cbv, base, n)).astype(bf16)
        acc_ref[r0:r0 + n, :] += _dot(act, wd)

    @pl.when(fi == pl.num_programs(1) - 1)
    def _():
        o_ref[...] = x_ref[...] + mod_ref[5:6, :] * acc_ref[...]


def _ffn(x3, norm_w, mod, wup, conv_w, conv_b, wdown, l):
    bsz, s, d = x3.shape
    tf, nf = FFN_TF, FFN_NF
    return pl.pallas_call(
        _ffn_kernel,
        grid=(bsz, nf),
        in_specs=[
            pl.BlockSpec((None, s, d), lambda b, f: (b, 0, 0)),
            pl.BlockSpec((None, 1, d), lambda b, f: (l, 0, 0)),
            pl.BlockSpec((None, None, 6, d), lambda b, f: (l, b, 0, 0)),
            pl.BlockSpec((None, d, tf), lambda b, f: (l, 0, f)),
            pl.BlockSpec((None, d, tf), lambda b, f: (l, 0, nf + f)),
            pl.BlockSpec((None, FFN_CONV, tf), lambda b, f: (l, 0, f)),
            pl.BlockSpec((None, FFN_CONV, tf), lambda b, f: (l, 0, nf + f)),
            pl.BlockSpec((None, 1, tf), lambda b, f: (l, 0, f)),
            pl.BlockSpec((None, 1, tf), lambda b, f: (l, 0, nf + f)),
            pl.BlockSpec((None, tf, d), lambda b, f: (l, f, 0)),
        ],
        out_specs=pl.BlockSpec((None, s, d), lambda b, f: (b, 0, 0)),
        out_shape=jax.ShapeDtypeStruct((bsz, s, d), f32),
        scratch_shapes=[
            pltpu.VMEM((s, d), bf16),
            pltpu.VMEM((s + SUBLANES, tf), f32), pltpu.VMEM((s + SUBLANES, tf), f32),
            pltpu.VMEM((s, d), f32),
        ],
        compiler_params=_cparams(("arbitrary", "arbitrary")),
        name="ffn",
    )(x3, norm_w, mod, wup, wup, conv_w, conv_w, conv_b, conv_b, wdown)


def _split_w_in(w_in):
    nl, d, _ = w_in.shape
    sizes = (512, 512, 512, MLA_Q_RANK, MLA_KV_RANK, MLA_ROPE, SSM_D_INNER, SSM_CONV_DIM, SSM_HEADS, 3 * D_MODEL)
    offs = np.concatenate([[0], np.cumsum(sizes)])
    qa, ka, va, cq, ckv, kr, z, xbc, dt, gates = [w_in[..., int(offs[i]):int(offs[i + 1])] for i in range(10)]
    zeros = lambda n: jnp.zeros((nl, d, n), w_in.dtype)
    misc = jnp.concatenate([dt, zeros(ROPE_X1 - SSM_HEADS), kr[..., :HALF_ROPE], zeros(ROPE_X2 - ROPE_X1 - HALF_ROPE),
                            kr[..., HALF_ROPE:], zeros(LANES - ROPE_X2 - HALF_ROPE)], axis=-1)
    qkv = w_in[..., :int(offs[3])]
    lat = jnp.concatenate([cq, ckv, misc], axis=-1)
    return tuple(w.astype(bf16) for w in (gates, qkv, xbc, z, lat))


def _pad_lanes(v, n):
    return jnp.pad(v, [(0, 0)] * (v.ndim - 1) + [(0, n - v.shape[-1])])


def _to_head_slot(v):
    return jnp.zeros(v.shape[:-1] + (LANES,), v.dtype).at[..., LANE_OF_DIM[:v.shape[-1]]].set(v)


def kernel(x, c, positions, rel_bias, ada_w, ada_b, norm_mix, norm_ffn, w_in, da_q_norm, da_k_norm, da_lambda, da_subln, mla_q_a_norm, mla_kv_a_norm, mla_w_uq, mla_w_ukv, mla_q_norm, mla_k_norm, ssm_conv_w, ssm_conv_b, ssm_dt_bias, ssm_a_log, ssm_d, ssm_norm, w_branch_a, w_branch_b, w_branch_c, w_out, ffn_w_up, ffn_conv_w, ffn_conv_b, ffn_w_down):
    bsz, seq, d = x.shape
    nl = w_in.shape[0]
    t = bsz * seq
    tq = min(ATT_TQ, seq)

    w_parts = _split_w_in(w_in)
    qk_w = jnp.stack([jnp.tile(da_q_norm, (1, 2 * DA_HEADS)) * (DA_HEAD_DIM ** -0.5 * LOG2E),
                      jnp.tile(da_k_norm, (1, 2 * DA_HEADS))], axis=1).reshape(nl, 2, 1, IN_TN)
    blk = np.arange(IN_TN) // DA_HEAD_DIM
    bd = jnp.asarray(blk[:, None] == blk[None, :], bf16)
    wuq = _to_head_slot(mla_w_uq.reshape(nl, MLA_Q_RANK, MLA_HEADS, MLA_QK))
    wuq = wuq.reshape(nl, MLA_Q_RANK, MLA_HEADS * LANES).astype(bf16)
    wukv = mla_w_ukv.reshape(nl, MLA_KV_RANK, MLA_HEADS, MLA_NOPE + MLA_V)
    wuk = _to_head_slot(wukv[..., :MLA_NOPE]).reshape(nl, MLA_KV_RANK, MLA_HEADS * LANES).astype(bf16)
    wuv = wukv[..., MLA_NOPE:].reshape(nl, MLA_KV_RANK, MLA_HEADS * MLA_V).astype(bf16)
    qn = (_to_head_slot(mla_q_norm) * (MLA_QK ** -0.5 * LOG2E)).reshape(nl, 1, LANES)
    kn = _to_head_slot(mla_k_norm).reshape(nl, 1, LANES)
    dt_bias = _pad_lanes(ssm_dt_bias, LANES).reshape(nl, 1, LANES)
    a_log = _pad_lanes(ssm_a_log, LANES).reshape(nl, 1, LANES)
    dvec = jnp.repeat(ssm_d, SSM_HEAD_DIM, axis=-1).reshape(nl, 1, SSM_D_INNER)
    conv_w8 = jnp.broadcast_to(ssm_conv_w[:, :, None, :], (nl, SSM_CONV, SUBLANES, SSM_CONV_DIM))
    conv_b8 = jnp.broadcast_to(ssm_conv_b[:, None, :], (nl, SUBLANES, SSM_CONV_DIM))
    v3 = lambda a: a.reshape(nl, 1, a.shape[-1])
    wa, wb, wc, wo = (w.astype(bf16) for w in (w_branch_a, w_branch_b, w_branch_c, w_out))
    wup, wdown = ffn_w_up.astype(bf16), ffn_w_down.astype(bf16)

    mod = _ada_mod(c, ada_w, ada_b).reshape(nl, bsz, 6, d)
    ctab, stab = _rope_tables(positions)
    bias = _bias_tiles(rel_bias, tq)

    x2 = x.reshape(t, d)
    for l in range(nl):
        lam_init = 0.8 - 0.6 * math.exp(-0.3 * l)
        gqkv, rest = _inproj(x2, v3(norm_mix), mod, w_parts, qk_w, bd, l, seq)
        q_b, k_b, v_b = _mla_prep(rest, v3(mla_q_a_norm), v3(mla_kv_a_norm), wuq, wuk, wuv, qn, kn, ctab, stab, l)
        ya = _diff_attn(gqkv.reshape(bsz, seq, GQKV_COLS), bias, da_lambda, v3(da_subln), l, lam_init)
        yb = _mla_attn(q_b.reshape(bsz, seq, -1), k_b.reshape(bsz, seq, -1), v_b.reshape(bsz, seq, -1),
                       min(MLA_TQ, seq))
        ys = _ssd(rest.reshape(bsz, seq, REST_COLS), conv_w8, conv_b8, dt_bias, a_log, dvec, v3(ssm_norm), l)
        x2 = _merge(ya.reshape(t, -1), yb.reshape(t, -1), ys.reshape(t, -1), gqkv, x2, mod, wa, wb, wc, wo, l, seq)
        x2 = _ffn(x2.reshape(bsz, seq, d), v3(norm_ffn), mod, wup, ffn_conv_w, v3(ffn_conv_b), wdown, l).reshape(t, d)
    return x2.reshape(bsz, seq, d)
```
